```python
import math
import jax, jax.numpy as jnp
from jax import lax
import numpy as np

D_MODEL = 1024
BATCH = 4
SEQ = 4096
DEPTH = 2
DEC_BATCH = 32
DEC_SEQ = 64
PAST_LEN = 1024

CHUNK = 64
HEAD_DIM = 64
A_HEADS = 8
A_WIDTH = A_HEADS * HEAD_DIM
A_LEFT_CHUNKS = 8
A_BAND = (A_LEFT_CHUNKS + 1) * CHUNK
A_WIN = A_LEFT_CHUNKS * CHUNK
A_REL_CLIP = 128
B_HEADS = 8
B_HEAD_DIM = 64
B_WIDTH = B_HEADS * B_HEAD_DIM
B_GROUPS = 2
B_STATE = 64
B_CONV = 4
B_XBC = B_WIDTH + 2 * B_GROUPS * B_STATE
C_HEADS = 8
C_WIDTH = C_HEADS * HEAD_DIM
C_BLOCK = 128
FORGET_BIAS_INIT = 3.0
M_HEADS = 4
M_WIDTH = M_HEADS * HEAD_DIM
N_MEM = 256
N_BRANCH = 4
EPS = 1e-6
NEG = -1e30
F32 = jnp.float32

SPLITS = (('a_q', A_WIDTH), ('a_k', A_WIDTH), ('a_v', A_WIDTH), ('a_z', A_WIDTH),
          ('b_z', B_WIDTH), ('b_xbc', B_XBC), ('b_dt', B_HEADS),
          ('c_q', C_WIDTH), ('c_k', C_WIDTH), ('c_v', C_WIDTH), ('c_f', C_HEADS), ('c_z', C_WIDTH),
          ('m_q', M_WIDTH), ('m_z', M_WIDTH),
          ('gate', N_BRANCH * D_MODEL))
D_IN = sum(w for _, w in SPLITS)

kernel_name = 'hybrid_streaming_encoder_step'


def rmsnorm(x, g):
    xf = x.astype(F32)
    y = xf * lax.rsqrt(jnp.mean(xf * xf, axis=-1, keepdims=True) + EPS)
    return (y * g.astype(F32)).astype(x.dtype)


def heads(t, n):
    return t.reshape(t.shape[:-1] + (n, t.shape[-1] // n))


def split_proj(h, w_in):
    u = h @ w_in
    parts = {}
    off = 0
    for name, width in SPLITS:
        parts[name] = u[..., off:off + width]
        off += width
    return parts


def qkv_heads(u, pre, n, gq, gk):
    q = rmsnorm(heads(u[pre + '_q'], n), gq)
    k = rmsnorm(heads(u[pre + '_k'], n), gk)
    v = heads(u[pre + '_v'], n)
    return q, k, v


def rel_bias(table, dist):
    idx = jnp.clip(dist, -A_REL_CLIP, A_REL_CLIP) + A_REL_CLIP
    return jnp.moveaxis(table[idx].astype(F32), -1, 0)


def chunk_band_prompt(q, k, v, table):
    b, S, H, dh = q.shape
    nc = S // CHUNK
    nb = A_LEFT_CHUNKS + 1

    def band(t):
        pad = jnp.zeros((b, A_WIN, H, dh), t.dtype)
        tc = jnp.concatenate([pad, t], axis=1).reshape(b, nc + A_LEFT_CHUNKS, CHUNK, H, dh)
        return jnp.stack([tc[:, j:j + nc] for j in range(nb)], axis=2).reshape(b, nc, A_BAND, H, dh)

    kb, vb = band(k), band(v)
    qc = q.reshape(b, nc, CHUNK, H, dh)
    s = jnp.einsum('bcqhd,bckhd->bchqk', qc, kb).astype(F32) * (dh ** -0.5)
    dist = A_WIN + jnp.arange(CHUNK)[:, None] - jnp.arange(A_BAND)[None, :]
    s = s + rel_bias(table, dist)
    key_chunk = jnp.arange(nc)[:, None] - A_LEFT_CHUNKS + jnp.arange(A_BAND)[None, :] // CHUNK
    s = jnp.where((key_chunk >= 0)[None, :, None, None, :], s, NEG)
    p = jax.nn.softmax(s, axis=-1).astype(v.dtype)
    return jnp.einsum('bchqk,bckhd->bcqhd', p, vb).reshape(b, S, H, dh)


def chunk_band_sample(q, k_new, v_new, k_cache, v_cache, table):
    Lc, T, dh = k_cache.shape[1], q.shape[1], q.shape[-1]
    k = jnp.concatenate([k_cache.astype(k_new.dtype), k_new], axis=1)
    v = jnp.concatenate([v_cache.astype(v_new.dtype), v_new], axis=1)
    s = jnp.einsum('bqhd,bkhd->bhqk', q, k).astype(F32) * (dh ** -0.5)
    dist = (Lc + jnp.arange(T))[:, None] - jnp.arange(Lc + T)[None, :]
    s = s + rel_bias(table, dist)
    p = jax.nn.softmax(s, axis=-1).astype(v.dtype)
    return jnp.einsum('bhqk,bkhd->bqhd', p, v)


def forget_log(u, fbias):
    return jax.nn.log_sigmoid(u['c_f'].astype(F32) + fbias.astype(F32))


def fox_prompt(q, k, v, logf):
    b, S, H, dh = q.shape
    cum = jnp.cumsum(logf, axis=1).transpose(0, 2, 1)
    kpos = jnp.arange(S)

    def block(i):
        start = i * C_BLOCK
        qb = lax.dynamic_slice_in_dim(q, start, C_BLOCK, axis=1)
        cb = lax.dynamic_slice_in_dim(cum, start, C_BLOCK, axis=2)
        s = jnp.einsum('bqhd,bkhd->bhqk', qb, k).astype(F32) * (dh ** -0.5)
        s = s + cb[..., :, None] - cum[..., None, :]
        qpos = start + jnp.arange(C_BLOCK)
        s = jnp.where(kpos[None, :] <= qpos[:, None], s, NEG)
        p = jax.nn.softmax(s, axis=-1).astype(v.dtype)
        return jnp.einsum('bhqk,bkhd->bqhd', p, v)

    o = lax.map(block, jnp.arange(S // C_BLOCK))
    return jnp.moveaxis(o, 0, 1).reshape(b, S, H, dh)


def fox_sample(q, k_new, v_new, logf_new, k_cache, v_cache, logf_cache):
    T, dh = q.shape[1], q.shape[-1]
    P = k_cache.shape[1]
    k = jnp.concatenate([k_cache.astype(k_new.dtype), k_new], axis=1)
    v = jnp.concatenate([v_cache.astype(v_new.dtype), v_new], axis=1)
    cum_c = jnp.cumsum(logf_cache.astype(F32), axis=1)
    cum_n = cum_c[:, -1:] + jnp.cumsum(logf_new, axis=1)
    cum_k = jnp.concatenate([cum_c, cum_n], axis=1).transpose(0, 2, 1)
    cum_q = cum_n.transpose(0, 2, 1)
    s = jnp.einsum('bqhd,bkhd->bhqk', q, k).astype(F32) * (dh ** -0.5)
    s = s + cum_q[..., :, None] - cum_k[..., None, :]
    visible = jnp.arange(P + T)[None, :] <= (P + jnp.arange(T))[:, None]
    s = jnp.where(visible, s, NEG)
    p = jax.nn.softmax(s, axis=-1).astype(v.dtype)
    return jnp.einsum('bhqk,bkhd->bqhd', p, v)


def causal_conv(xbc, conv_state, w, bias):
    L = xbc.shape[1]
    xp = jnp.concatenate([conv_state.astype(xbc.dtype), xbc], axis=1)
    y = bias
    for tap in range(B_CONV):
        y = y + xp[:, tap:tap + L] * w[tap]
    return jax.nn.silu(y), xp[:, -(B_CONV - 1):]


def ssd(x, dt, A, Bm, Cm, h0):
    b, L, H, P = x.shape
    N = Bm.shape[-1]
    Q = CHUNK if L % CHUNK == 0 else L
    nc = L // Q
    rep = H // B_GROUPS
    Bc = jnp.repeat(Bm, rep, axis=2).reshape(b, nc, Q, H, N)
    Cc = jnp.repeat(Cm, rep, axis=2).reshape(b, nc, Q, H, N)
    xc = x.reshape(b, nc, Q, H, P)
    dtc = dt.reshape(b, nc, Q, H)
    acum = jnp.cumsum(dtc * A, axis=2)
    seg = acum[:, :, :, None, :] - acum[:, :, None, :, :]
    causal = jnp.tril(jnp.ones((Q, Q), bool))
    Lmat = jnp.exp(jnp.where(causal[None, None, :, :, None], seg, -jnp.inf))
    xdt = xc * dtc[..., None]
    CB = jnp.einsum('bclhn,bcshn->bclsh', Cc, Bc)
    y_diag = jnp.einsum('bclsh,bcshp->bclhp', CB * Lmat, xdt)
    decay = jnp.exp(acum[:, :, -1:, :] - acum)
    states = jnp.einsum('bcshn,bcsh,bcshp->bchpn', Bc, decay, xdt)
    chunk_decay = jnp.exp(acum[:, :, -1, :])

    def step(h, inp):
        st, dec = inp
        return dec[:, :, None, None] * h + st, h

    h_last, h_prev = lax.scan(step, h0, (jnp.moveaxis(states, 1, 0), jnp.moveaxis(chunk_decay, 1, 0)))
    h_prev = jnp.moveaxis(h_prev, 0, 1)
    y_off = jnp.einsum('bclhn,bchpn,bclh->bclhp', Cc, h_prev, jnp.exp(acum))
    return (y_diag + y_off).reshape(b, L, H, P), h_last


def mamba_branch(u, conv_state, h0, conv_w, conv_b, dt_bias, a_log, d_skip, norm_g):
    xbc, new_conv = causal_conv(u['b_xbc'], conv_state, conv_w, conv_b)
    b, L, _ = xbc.shape
    xs = heads(xbc[..., :B_WIDTH], B_HEADS)
    Bm = heads(xbc[..., B_WIDTH:B_WIDTH + B_GROUPS * B_STATE], B_GROUPS)
    Cm = heads(xbc[..., B_WIDTH + B_GROUPS * B_STATE:], B_GROUPS)
    dt = jax.nn.softplus(u['b_dt'].astype(F32) + dt_bias.astype(F32))
    A = -jnp.exp(a_log.astype(F32))
    xf = xs.astype(F32)
    y, h_last = ssd(xf, dt, A, Bm.astype(F32), Cm.astype(F32), h0.astype(F32))
    y = y + d_skip.astype(F32)[:, None] * xf
    y = y.reshape(b, L, B_WIDTH).astype(xbc.dtype)
    y = rmsnorm(y * jax.nn.silu(u['b_z']), norm_g)
    return y, new_conv, h_last.astype(xbc.dtype)


def memory_kv(mem, g, w_mkv, gk):
    kv = rmsnorm(mem, g) @ w_mkv
    k = rmsnorm(heads(kv[..., :M_WIDTH], M_HEADS), gk)
    v = heads(kv[..., M_WIDTH:], M_HEADS)
    return k, v


def cross_attend(q, k, v):
    dh = q.shape[-1]
    s = jnp.einsum('bqhd,bkhd->bhqk', q, k.astype(q.dtype)).astype(F32) * (dh ** -0.5)
    p = jax.nn.softmax(s, axis=-1).astype(q.dtype)
    return jnp.einsum('bhqk,bkhd->bqhd', p, v.astype(q.dtype))


def merge(u, o_a, o_b, o_c, o_m, w_pa, w_pb, w_pc, w_pm, w_out):
    def flat(o):
        return o.reshape(o.shape[:2] + (-1,))
    gates = jax.nn.sigmoid(u['gate'].astype(F32)).astype(o_b.dtype)
    g_a, g_b, g_c, g_m = jnp.split(gates, N_BRANCH, axis=-1)
    p_a = (flat(o_a) * jax.nn.silu(u['a_z'])) @ w_pa
    p_b = o_b @ w_pb
    p_c = (flat(o_c) * jax.nn.silu(u['c_z'])) @ w_pc
    p_m = (flat(o_m) * jax.nn.silu(u['m_z'])) @ w_pm
    return (g_a * p_a + g_b * p_b + g_c * p_c + g_m * p_m) @ w_out


def setup_inputs(seed: int = 0) -> dict:
    key = jax.random.key(seed)
    ks = iter(jax.random.split(key, 48))

    def nrm(shape, scale=1.0):
        return scale * jax.random.normal(next(ks), shape, F32)

    la = min(A_WIN, PAST_LEN)
    dt0 = jnp.exp(jax.random.uniform(next(ks), (DEPTH, B_HEADS), F32, math.log(1e-3), math.log(1e-1)))
    return {
        'x_prompt': nrm((BATCH, SEQ, D_MODEL)),
        'x_sample': nrm((DEC_BATCH, DEC_SEQ, D_MODEL)),
        'mem_prompt': nrm((BATCH, N_MEM, D_MODEL)),
        'cache_a_k': nrm((DEPTH, DEC_BATCH, la, A_HEADS, HEAD_DIM)),
        'cache_a_v': nrm((DEPTH, DEC_BATCH, la, A_HEADS, HEAD_DIM)),
        'cache_c_k': nrm((DEPTH, DEC_BATCH, PAST_LEN, C_HEADS, HEAD_DIM)),
        'cache_c_v': nrm((DEPTH, DEC_BATCH, PAST_LEN, C_HEADS, HEAD_DIM)),
        'cache_c_logf': jax.nn.log_sigmoid(FORGET_BIAS_INIT + nrm((DEPTH, DEC_BATCH, PAST_LEN, C_HEADS), 0.5)),
        'state_b_ssm': nrm((DEPTH, DEC_BATCH, B_HEADS, B_HEAD_DIM, B_STATE), 0.1),
        'state_b_conv': nrm((DEPTH, DEC_BATCH, B_CONV - 1, B_XBC)),
        'cache_mem_k': nrm((DEPTH, DEC_BATCH, N_MEM, M_HEADS, HEAD_DIM)),
        'cache_mem_v': nrm((DEPTH, DEC_BATCH, N_MEM, M_HEADS, HEAD_DIM)),
        'g_norm': 1.0 + nrm((DEPTH, D_MODEL), 0.02),
        'w_in': nrm((DEPTH, D_MODEL, D_IN), D_MODEL ** -0.5),
        'a_qnorm': 1.0 + nrm((DEPTH, HEAD_DIM), 0.02),
        'a_knorm': 1.0 + nrm((DEPTH, HEAD_DIM), 0.02),
        'a_rel': nrm((DEPTH, 2 * A_REL_CLIP + 1, A_HEADS), 0.1),
        'b_conv_w': nrm((DEPTH, B_CONV, B_XBC), B_CONV ** -0.5),
        'b_conv_b': nrm((DEPTH, B_XBC), 0.02),
        'b_dt_bias': dt0 + jnp.log(-jnp.expm1(-dt0)),
        'b_a_log': jnp.log(jax.random.uniform(next(ks), (DEPTH, B_HEADS), F32, 1.0, 16.0)),
        'b_d': 1.0 + nrm((DEPTH, B_HEADS), 0.02),
        'b_norm': 1.0 + nrm((DEPTH, B_WIDTH), 0.02),
        'c_qnorm': 1.0 + nrm((DEPTH, HEAD_DIM), 0.02),
        'c_knorm': 1.0 + nrm((DEPTH, HEAD_DIM), 0.02),
        'c_fbias': FORGET_BIAS_INIT + nrm((DEPTH, C_HEADS), 0.1),
        'm_norm': 1.0 + nrm((DEPTH, D_MODEL), 0.02),
        'w_mkv': nrm((DEPTH, D_MODEL, 2 * M_WIDTH), D_MODEL ** -0.5),
        'm_qnorm': 1.0 + nrm((DEPTH, HEAD_DIM), 0.02),
        'm_knorm': 1.0 + nrm((DEPTH, HEAD_DIM), 0.02),
        'w_pa': nrm((DEPTH, A_WIDTH, D_MODEL), A_WIDTH ** -0.5),
        'w_pb': nrm((DEPTH, B_WIDTH, D_MODEL), B_WIDTH ** -0.5),
        'w_pc': nrm((DEPTH, C_WIDTH, D_MODEL), C_WIDTH ** -0.5),
        'w_pm': nrm((DEPTH, M_WIDTH, D_MODEL), M_WIDTH ** -0.5),
        'w_out': nrm((DEPTH, D_MODEL, D_MODEL), D_MODEL ** -0.5),
    }


def reference(x_prompt, x_sample, mem_prompt, cache_a_k, cache_a_v, cache_c_k, cache_c_v, cache_c_logf,
              state_b_ssm, state_b_conv, cache_mem_k, cache_mem_v,
              g_norm, w_in, a_qnorm, a_knorm, a_rel, b_conv_w, b_conv_b, b_dt_bias, b_a_log, b_d, b_norm,
              c_qnorm, c_knorm, c_fbias, m_norm, w_mkv, m_qnorm, m_knorm, w_pa, w_pb, w_pc, w_pm, w_out):
    xp, xd = x_prompt, x_sample
    bp = xp.shape[0]
    pa_k, pa_v, pc_k, pc_v, pc_f, pb_s, pb_c, pm_k, pm_v = [], [], [], [], [], [], [], [], []
    sa_k, sa_v, sc_k, sc_v, sc_f, sb_s, sb_c = [], [], [], [], [], [], []
    for l in range(DEPTH):
        u = split_proj(rmsnorm(xp, g_norm[l]), w_in[l])
        qa, ka, va = qkv_heads(u, 'a', A_HEADS, a_qnorm[l], a_knorm[l])
        o_a = chunk_band_prompt(qa, ka, va, a_rel[l])
        o_b, conv_p, ssm_p = mamba_branch(
            u, jnp.zeros((bp, B_CONV - 1, B_XBC), xp.dtype), jnp.zeros((bp, B_HEADS, B_HEAD_DIM, B_STATE), F32),
            b_conv_w[l], b_conv_b[l], b_dt_bias[l], b_a_log[l], b_d[l], b_norm[l])
        qc, kc, vc = qkv_heads(u, 'c', C_HEADS, c_qnorm[l], c_knorm[l])
        logf = forget_log(u, c_fbias[l])
        o_c = fox_prompt(qc, kc, vc, logf)
        mk, mv = memory_kv(mem_prompt, m_norm[l], w_mkv[l], m_knorm[l])
        qm = rmsnorm(heads(u['m_q'], M_HEADS), m_qnorm[l])
        o_m = cross_attend(qm, mk, mv)
        xp = xp + merge(u, o_a, o_b, o_c, o_m, w_pa[l], w_pb[l], w_pc[l], w_pm[l], w_out[l])
        la = min(A_WIN, ka.shape[1])
        pa_k.append(ka[:, -la:]); pa_v.append(va[:, -la:])
        pc_k.append(kc); pc_v.append(vc); pc_f.append(logf)
        pb_s.append(ssm_p); pb_c.append(conv_p)
        pm_k.append(mk); pm_v.append(mv)

        u = split_proj(rmsnorm(xd, g_norm[l]), w_in[l])
        qa, ka, va = qkv_heads(u, 'a', A_HEADS, a_qnorm[l], a_knorm[l])
        o_a = chunk_band_sample(qa, ka, va, cache_a_k[l], cache_a_v[l], a_rel[l])
        o_b, conv_s, ssm_s = mamba_branch(
            u, state_b_conv[l], state_b_ssm[l],
            b_conv_w[l], b_conv_b[l], b_dt_bias[l], b_a_log[l], b_d[l], b_norm[l])
        qc, kc, vc = qkv_heads(u, 'c', C_HEADS, c_qnorm[l], c_knorm[l])
        logf = forget_log(u, c_fbias[l])
        o_c = fox_sample(qc, kc, vc, logf, cache_c_k[l], cache_c_v[l], cache_c_logf[l])
        qm = rmsnorm(heads(u['m_q'], M_HEADS), m_qnorm[l])
        o_m = cross_attend(qm, cache_mem_k[l], cache_mem_v[l])
        xd = xd + merge(u, o_a, o_b, o_c, o_m, w_pa[l], w_pb[l], w_pc[l], w_pm[l], w_out[l])
        sa_k.append(ka); sa_v.append(va)
        sc_k.append(kc); sc_v.append(vc); sc_f.append(logf)
        sb_s.append(ssm_s); sb_c.append(conv_s)

    st = jnp.stack
    return (xp, xd,
            st(pa_k), st(pa_v), st(pc_k), st(pc_v), st(pc_f), st(pb_s), st(pb_c), st(pm_k), st(pm_v),
            st(sa_k), st(sa_v), st(sc_k), st(sc_v), st(sc_f), st(sb_s), st(sb_c))
```

```python
import functools
import math

import numpy as np
import jax
import jax.numpy as jnp
from jax import lax
from jax.experimental import pallas as pl
from jax.experimental.pallas import tpu as pltpu

F32 = jnp.float32
BF16 = jnp.bfloat16

EPS = 1e-6
NEG = -1e30
LOG2E = 1.4426950408889634

HEAD_DIM = 64
CHUNK = 64
A_HEADS = 8
A_WIDTH = 512
A_LEFT_CHUNKS = 8
A_WIN = A_LEFT_CHUNKS * CHUNK
A_REL_CLIP = 128
B_HEADS = 8
B_WIDTH = 512
B_GROUPS = 2
B_STATE = 64
B_CONV = 4
B_XBC = B_WIDTH + 2 * B_GROUPS * B_STATE
C_HEADS = 8
C_WIDTH = 512
M_HEADS = 4
M_WIDTH = 256
N_BRANCH = 4

LANES = 128
SUBLANES = 8
VMEM_LIMIT = 48 * 1024 * 1024

_SPLITS = (('a_q', A_WIDTH), ('a_k', A_WIDTH), ('a_v', A_WIDTH), ('a_z', A_WIDTH),
           ('b_z', B_WIDTH), ('b_xbc', B_XBC), ('b_dt', B_HEADS),
           ('c_q', C_WIDTH), ('c_k', C_WIDTH), ('c_v', C_WIDTH), ('c_f', C_HEADS), ('c_z', C_WIDTH),
           ('m_q', M_WIDTH), ('m_z', M_WIDTH), ('gate', N_BRANCH * 1024))
_SRC = {}
_off = 0
for _n, _w in _SPLITS:
    _SRC[_n] = (_off, _w)
    _off += _w

_U_ORDER = ('a_q', 'a_k', 'a_v', 'a_z', 'c_q', 'c_k', 'c_v', 'c_z', 'b_z', 'b_xbc', 'm_q', 'm_z', 'b_dt', 'c_f')
_U_OFF = {}
_off = 0
for _n in _U_ORDER:
    _U_OFF[_n] = _off
    _off += _SRC[_n][1]
U_SMALL = _U_OFF['b_dt']
U_WIDTH = 6144
DT_LANE = 0
CF_LANE = B_HEADS
AUG_PER_HEAD = 6


def _cp(sem):
    return pltpu.CompilerParams(dimension_semantics=sem, vmem_limit_bytes=VMEM_LIMIT)


def _pick(n, cands):
    for c in cands:
        if n % c == 0:
            return c
    raise ValueError(f"no tile for {n} in {cands}")


def _dot(a, b):
    return jnp.dot(a, b, preferred_element_type=F32)


def _dot_nt(a, b):
    return lax.dot_general(a, b, (((1,), (1,)), ((), ())), preferred_element_type=F32)


def _dot_tn(a, b):
    return lax.dot_general(a, b, (((0,), (0,)), ((), ())), preferred_element_type=F32)


def _split3(x):
    hi = x.astype(BF16)
    r = x - hi.astype(F32)
    mid = r.astype(BF16)
    lo = (r - mid.astype(F32)).astype(BF16)
    return hi, mid, lo


def _dot_sel(x, sel):
    hi, mid, lo = _split3(x)
    return _dot(hi, sel) + _dot(mid, sel) + _dot(lo, sel)


def _sel_dot(sel, x):
    hi, mid, lo = _split3(x)
    return _dot(sel, hi) + _dot(sel, mid) + _dot(sel, lo)


def _sigmoid(x):
    return 1.0 / (1.0 + jnp.exp(-x))


def _silu(x):
    return x * _sigmoid(x)


def _softplus(x):
    return jnp.maximum(x, 0.0) + jnp.log1p(jnp.exp(-jnp.abs(x)))


def _log_sigmoid(x):
    return jnp.minimum(x, 0.0) - jnp.log1p(jnp.exp(-jnp.abs(x)))


def _head_norm(x, bd):
    x2 = x * x
    hi = x2.astype(BF16)
    lo = (x2 - hi.astype(F32)).astype(BF16)
    ms = _dot(hi, bd) + _dot(lo, bd)
    return x * lax.rsqrt(ms + EPS)


def _lane_iota(shape):
    return lax.broadcasted_iota(jnp.int32, shape, len(shape) - 1)


def _row_iota(shape):
    return lax.broadcasted_iota(jnp.int32, shape, len(shape) - 2)


def _proj_kernel(x_ref, g_ref, w_ref, o_ref, h_ref, *, act):
    @pl.when(pl.program_id(1) == 0)
    def _():
        x = x_ref[...]
        ms = jnp.mean(x * x, axis=-1, keepdims=True)
        h_ref[...] = (x * lax.rsqrt(ms + EPS) * g_ref[...]).astype(BF16)

    u = _dot(h_ref[...], w_ref[...])
    if act == 'sigmoid':
        u = _sigmoid(u)
    o_ref[...] = u.astype(o_ref.dtype)


def _proj(x, g_row, w_bf, *, act, out_dtype, name):
    m, d = x.shape
    n = w_bf.shape[1]
    tm = _pick(m, (1024, 512, 256, 128))
    tn = _pick(n, (1024, 512, 256, 128))
    return pl.pallas_call(
        functools.partial(_proj_kernel, act=act),
        grid=(m // tm, n // tn),
        in_specs=[pl.BlockSpec((tm, d), lambda i, j: (i, 0)),
                  pl.BlockSpec((1, d), lambda i, j: (0, 0)),
                  pl.BlockSpec((d, tn), lambda i, j: (0, j))],
        out_specs=pl.BlockSpec((tm, tn), lambda i, j: (i, j)),
        out_shape=jax.ShapeDtypeStruct((m, n), out_dtype),
        scratch_shapes=[pltpu.VMEM((tm, d), BF16)],
        compiler_params=_cp(("parallel", "arbitrary")),
        name=name,
    )(x, g_row, w_bf)


def _prep_kernel(aq, ak, av, cq, ck, cv, mq, sm, gaq, gak, gcq, gck, gmq, fb, bd,
                 o_aq, o_akf, o_akb, o_avb, o_cq, o_ckf, o_ckb, o_cvb, o_mq, o_lf):
    bdv = bd[...]
    o_aq[...] = (_head_norm(aq[...], bdv) * gaq[...]).astype(BF16)
    akn = _head_norm(ak[...], bdv) * gak[...]
    o_akf[...] = akn
    o_akb[...] = akn.astype(BF16)
    o_avb[...] = av[...].astype(BF16)
    o_cq[...] = (_head_norm(cq[...], bdv) * gcq[...]).astype(BF16)
    ckn = _head_norm(ck[...], bdv) * gck[...]
    o_ckf[...] = ckn
    o_ckb[...] = ckn.astype(BF16)
    o_cvb[...] = cv[...].astype(BF16)
    o_mq[...] = (_head_norm(mq[...], bd[:M_WIDTH, :M_WIDTH]) * gmq[...]).astype(BF16)
    o_lf[...] = _log_sigmoid(sm[...] + fb[...])


def _prep(u, gaq, gak, gcq, gck, gmq, fb, bd):
    m = u.shape[0]
    ts = _pick(m, (512, 256, 128))

    def ucol(name, width):
        idx = _U_OFF[name] // width
        assert idx * width == _U_OFF[name]
        return pl.BlockSpec((ts, width), lambda i: (i, idx))

    def row(width):
        return pl.BlockSpec((1, width), lambda i: (0, 0))

    def out(width):
        return pl.BlockSpec((ts, width), lambda i: (i, 0))

    sds = jax.ShapeDtypeStruct
    return pl.pallas_call(
        _prep_kernel,
        grid=(m // ts,),
        in_specs=[ucol('a_q', 512), ucol('a_k', 512), ucol('a_v', 512),
                  ucol('c_q', 512), ucol('c_k', 512), ucol('c_v', 512),
                  ucol('m_q', 256), ucol('b_dt', LANES),
                  row(512), row(512), row(512), row(512), row(256), row(LANES),
                  pl.BlockSpec((512, 512), lambda i: (0, 0))],
        out_specs=[out(512), out(512), out(512), out(512), out(512), out(512), out(512), out(512),
                   out(256), out(LANES)],
        out_shape=[sds((m, 512), BF16), sds((m, 512), F32), sds((m, 512), BF16), sds((m, 512), BF16),
                   sds((m, 512), BF16), sds((m, 512), F32), sds((m, 512), BF16), sds((m, 512), BF16),
                   sds((m, 256), BF16), sds((m, LANES), F32)],
        compiler_params=_cp(("parallel",)),
        name="prep",
    )(u, u, u, u, u, u, u, u, gaq, gak, gcq, gck, gmq, fb, bd)


def _memk_kernel(kv, g, bd, o):
    o[...] = _head_norm(kv[...], bd[...]) * g[...]


def _memk(kv, g_row, bd256):
    m = kv.shape[0]
    ts = _pick(m, (512, 256, 128))
    return pl.pallas_call(
        _memk_kernel,
        grid=(m // ts,),
        in_specs=[pl.BlockSpec((ts, M_WIDTH), lambda i: (i, 0)),
                  pl.BlockSpec((1, M_WIDTH), lambda i: (0, 0)),
                  pl.BlockSpec((M_WIDTH, M_WIDTH), lambda i: (0, 0))],
        out_specs=pl.BlockSpec((ts, M_WIDTH), lambda i: (i, 0)),
        out_shape=jax.ShapeDtypeStruct((m, M_WIDTH), F32),
        compiler_params=_cp(("parallel",)),
        name="memk",
    )(kv, g_row, bd256)


def _cum_kernel(lf_ref, tri_ref, selq_ref, selk_ref, qc_ref, kc_ref, cq_ref, ck_ref, *, nchunk):
    tri = tri_ref[...]

    def body(c, carry):
        r = pl.multiple_of(c * LANES, LANES)
        x = lf_ref[pl.ds(r, LANES), :]
        cs = _sel_dot(tri, x) + carry
        z = cs * LOG2E
        zh, zm, zl = _split3(z)
        q = _dot(zh, selq_ref[0]) + _dot(zm, selq_ref[1]) + _dot(zl, selq_ref[2]) + qc_ref[...]
        k = _dot(zh, selk_ref[0]) + _dot(zm, selk_ref[1]) + _dot(zl, selk_ref[2]) + kc_ref[...]
        cq_ref[pl.ds(r, LANES), :] = q.astype(BF16)
        ck_ref[pl.ds(r, LANES), :] = k.astype(BF16)
        return cs[LANES - 1:LANES, :]

    lax.fori_loop(0, nchunk, body, jnp.zeros((1, LANES), F32))


def _cum(lf, consts):
    nseq, length, _ = lf.shape
    assert length % LANES == 0
    tri, selq, selk, qc, kc = consts
    full = lambda shape: pl.BlockSpec(shape, lambda s: (0,) * len(shape))
    seq = pl.BlockSpec((None, length, LANES), lambda s: (s, 0, 0))
    return pl.pallas_call(
        functools.partial(_cum_kernel, nchunk=length // LANES),
        grid=(nseq,),
        in_specs=[seq, full((LANES, LANES)), full((3, LANES, LANES)), full((3, LANES, LANES)),
                  full((1, LANES)), full((1, LANES))],
        out_specs=[seq, seq],
        out_shape=[jax.ShapeDtypeStruct(lf.shape, BF16)] * 2,
        compiler_params=_cp(("parallel",)),
        name="cum",
    )(lf, tri, selq, selk, qc, kc)


def _cum_consts():
    tri = np.tril(np.ones((LANES, LANES), np.float32))
    selq = np.zeros((3, LANES, LANES), np.float32)
    selk = np.zeros((3, LANES, LANES), np.float32)
    qc = np.zeros((1, LANES), np.float32)
    kc = np.zeros((1, LANES), np.float32)
    for h in range(C_HEADS):
        for p in range(3):
            selq[p, CF_LANE + h, AUG_PER_HEAD * h + p] = 1.0
            selk[p, CF_LANE + h, AUG_PER_HEAD * h + 3 + p] = -1.0
            qc[0, AUG_PER_HEAD * h + 3 + p] = 1.0
            kc[0, AUG_PER_HEAD * h + p] = 1.0
    return (jnp.asarray(tri, BF16), jnp.asarray(selq, BF16), jnp.asarray(selk, BF16),
            jnp.asarray(qc), jnp.asarray(kc))


BAND_Q = 4 * CHUNK
BAND_K = 3 * BAND_Q
TAB_PAD = 384


def _bias_kernel(tab_ref, idx_ref, neg_ref, o_ref):
    tn = idx_ref.shape[1]
    onehot = (lax.broadcasted_iota(jnp.int32, (TAB_PAD, tn), 0) == idx_ref[...]).astype(BF16)
    o_ref[...] = _dot_sel(tab_ref[...], onehot) * LOG2E + neg_ref[...]


def _band_bias(table):
    r = np.arange(BAND_Q)[:, None]
    t = np.arange(BAND_K)[None, :]
    idx = (np.clip(A_WIN + r - t, -A_REL_CLIP, A_REL_CLIP) + A_REL_CLIP).astype(np.int32).reshape(1, -1)
    cb = t // CHUNK - r // CHUNK
    neg = np.where((cb >= 0) & (cb <= A_LEFT_CHUNKS), 0.0, NEG).astype(np.float32).reshape(1, -1)
    n = idx.shape[1]
    tn = 8192
    tab = jnp.zeros((A_HEADS, TAB_PAD), F32).at[:, :table.shape[0]].set(table.T)
    out = pl.pallas_call(
        _bias_kernel,
        grid=(n // tn,),
        in_specs=[pl.BlockSpec((A_HEADS, TAB_PAD), lambda j: (0, 0)),
                  pl.BlockSpec((1, tn), lambda j: (0, j)),
                  pl.BlockSpec((1, tn), lambda j: (0, j))],
        out_specs=pl.BlockSpec((A_HEADS, tn), lambda j: (0, j)),
        out_shape=jax.ShapeDtypeStruct((A_HEADS, n), F32),
        compiler_params=_cp(("parallel",)),
        name="band_bias",
    )(tab, jnp.asarray(idx), jnp.asarray(neg))
    return out.reshape(A_HEADS, BAND_Q, BAND_K)


def _half_mask(par):
    lane = _lane_iota((1, LANES))
    return (lane < HEAD_DIM) if par == 0 else (lane >= HEAD_DIM)


def _softmax_pv(s_blocks, v_blocks):
    m = s_blocks[0].max(axis=1, keepdims=True)
    for s in s_blocks[1:]:
        m = jnp.maximum(m, s.max(axis=1, keepdims=True))
    l = None
    acc = None
    for s, v in zip(s_blocks, v_blocks):
        p = jnp.exp2(s - m)
        ps = p.sum(axis=1, keepdims=True)
        pv = _dot(p.astype(BF16), v)
        l = ps if l is None else l + ps
        acc = pv if acc is None else acc + pv
    return acc / l


def _band_prompt_kernel(q_ref, k0, k1, k2, v0, v1, v2, bias_ref, o_ref):
    g = pl.program_id(1)
    lane = _lane_iota((1, LANES))
    krefs = (k0, k1, k2)
    vrefs = (v0, v1, v2)
    for hp in range(A_HEADS // 2):
        cols = slice(hp * LANES, (hp + 1) * LANES)
        qp = q_ref[:, cols]
        ks = [kr[:, cols] for kr in krefs]
        vs = [vr[:, cols] for vr in vrefs]
        outs = []
        for par in range(2):
            h = 2 * hp + par
            qm = jnp.where(_half_mask(par), qp, jnp.zeros_like(qp))
            sb = []
            for j in range(3):
                s = _dot_nt(qm, ks[j]) + bias_ref[h, :, j * BAND_Q:(j + 1) * BAND_Q]
                if j < 2:
                    s = jnp.where(g + j >= 2, s, NEG)
                sb.append(s)
            outs.append(_softmax_pv(sb, vs))
        o_ref[:, cols] = jnp.where(lane < HEAD_DIM, outs[0], outs[1])


def _band_prompt(qs, kb, vb, bias, nb, seq):
    ng = seq // BAND_Q

    def kv(j):
        return pl.BlockSpec((BAND_Q, A_WIDTH), lambda b, g: (b * ng + jnp.maximum(g - 2 + j, 0), 0))

    cur = pl.BlockSpec((BAND_Q, A_WIDTH), lambda b, g: (b * ng + g, 0))
    return pl.pallas_call(
        _band_prompt_kernel,
        grid=(nb, ng),
        in_specs=[cur, kv(0), kv(1), kv(2), kv(0), kv(1), kv(2),
                  pl.BlockSpec((A_HEADS, BAND_Q, BAND_K), lambda b, g: (0, 0, 0))],
        out_specs=cur,
        out_shape=jax.ShapeDtypeStruct((nb * seq, A_WIDTH), F32),
        compiler_params=_cp(("parallel", "parallel")),
        name="band_prompt",
    )(qs, kb, kb, kb, vb, vb, vb, bias)


def _band_sample_kernel(q_ref, kc_ref, kn_ref, vc_ref, vn_ref, bias_ref, o_ref, *, t):
    lane = _lane_iota((1, LANES))
    for hp in range(A_HEADS // 2):
        cols = slice(hp * LANES, (hp + 1) * LANES)
        qp = q_ref[:, cols]
        pad = jnp.zeros((LANES - t, LANES), BF16)
        kc = kc_ref[:, cols].astype(BF16)
        vc = vc_ref[:, cols].astype(BF16)
        kn = jnp.concatenate([kn_ref[:, cols], pad], axis=0)
        vn = jnp.concatenate([vn_ref[:, cols], pad], axis=0)
        lc = kc.shape[0]
        outs = []
        for par in range(2):
            h = 2 * hp + par
            qm = jnp.where(_half_mask(par), qp, jnp.zeros_like(qp))
            s0 = _dot_nt(qm, kc) + bias_ref[h, :, :lc]
            s1 = _dot_nt(qm, kn) + bias_ref[h, :, lc:]
            outs.append(_softmax_pv([s0, s1], [vc, vn]))
        o_ref[:, cols] = jnp.where(lane < HEAD_DIM, outs[0], outs[1])


def _band_sample(qs, kb, vb, cache_k, cache_v, bias_s, row0, nseq, t):
    lc = cache_k.shape[0] // nseq
    blk0 = row0 // t
    new = pl.BlockSpec((t, A_WIDTH), lambda s: (blk0 + s, 0))
    cache = pl.BlockSpec((lc, A_WIDTH), lambda s: (s, 0))
    return pl.pallas_call(
        functools.partial(_band_sample_kernel, t=t),
        grid=(nseq,),
        in_specs=[new, cache, new, cache, new,
                  pl.BlockSpec(bias_s.shape, lambda s: (0, 0, 0))],
        out_specs=pl.BlockSpec((t, A_WIDTH), lambda s: (s, 0)),
        out_shape=jax.ShapeDtypeStruct((nseq * t, A_WIDTH), F32),
        compiler_params=_cp(("parallel",)),
        name="band_sample",
    )(qs, cache_k, kb, cache_v, vb, bias_s)


FOX_TQ = 256
FOX_TK = 512


def _aug_mask(h):
    lane = _lane_iota((1, LANES))
    return (lane >= AUG_PER_HEAD * h) & (lane < AUG_PER_HEAD * (h + 1))


def _fox_prompt_kernel(q_ref, cq_ref, k_ref, ck_ref, v_ref, o_ref, m_ref, l_ref, acc_ref):
    hp = pl.program_id(1)
    i = pl.program_id(2)
    lane = _lane_iota((1, LANES))
    qp = q_ref[...]
    cq = cq_ref[...]
    qcs = []
    for par in range(2):
        qm = jnp.where(_half_mask(par), qp, jnp.zeros_like(qp))
        qa = jnp.where(_aug_mask(2 * hp + par), cq, jnp.zeros_like(cq))
        qcs.append(jnp.concatenate([qm, qa], axis=1))
    m_ref[...] = jnp.full(m_ref.shape, NEG, F32)
    l_ref[...] = jnp.zeros(l_ref.shape, F32)
    acc_ref[...] = jnp.zeros(acc_ref.shape, F32)

    def block(off, width, masked):
        kc = jnp.concatenate([k_ref[pl.ds(off, width), :], ck_ref[pl.ds(off, width), :]], axis=1)
        v = v_ref[pl.ds(off, width), :]
        for par in range(2):
            s = _dot_nt(qcs[par], kc)
            if masked:
                vis = _lane_iota((FOX_TQ, width)) <= _row_iota((FOX_TQ, width))
                s = jnp.where(vis, s, NEG)
            m_old = m_ref[par]
            m_new = jnp.maximum(m_old, s.max(axis=1, keepdims=True))
            alpha = jnp.exp2(m_old - m_new)
            p = jnp.exp2(s - m_new)
            l_ref[par] = alpha * l_ref[par] + p.sum(axis=1, keepdims=True)
            acc_ref[par] = alpha * acc_ref[par] + _dot(p.astype(BF16), v)
            m_ref[par] = m_new

    ratio = FOX_TK // FOX_TQ
    nfull = i // ratio

    def body(j, carry):
        block(pl.multiple_of(j * FOX_TK, FOX_TK), FOX_TK, False)
        return carry

    lax.fori_loop(0, nfull, body, 0)
    for r in range(1, ratio):
        @pl.when(i % ratio >= r)
        def _():
            block(pl.multiple_of(nfull * FOX_TK + (r - 1) * FOX_TQ, FOX_TQ), FOX_TQ, False)
    block(pl.multiple_of(i * FOX_TQ, FOX_TQ), FOX_TQ, True)
    o_ref[...] = jnp.where(lane < HEAD_DIM, acc_ref[0] / l_ref[0], acc_ref[1] / l_ref[1])


def _fox_prompt(qs, kb, vb, cq, ck, nb, seq):
    nq = seq // FOX_TQ
    qspec = pl.BlockSpec((FOX_TQ, LANES), lambda b, hp, i: (b * nq + i, hp))
    cqspec = pl.BlockSpec((FOX_TQ, LANES), lambda b, hp, i: (b * nq + i, 0))
    kspec = pl.BlockSpec((seq, LANES), lambda b, hp, i: (b, hp))
    ckspec = pl.BlockSpec((seq, LANES), lambda b, hp, i: (b, 0))
    return pl.pallas_call(
        _fox_prompt_kernel,
        grid=(nb, C_HEADS // 2, nq),
        in_specs=[qspec, cqspec, kspec, ckspec, kspec],
        out_specs=qspec,
        out_shape=jax.ShapeDtypeStruct((nb * seq, C_WIDTH), F32),
        scratch_shapes=[pltpu.VMEM((2, FOX_TQ, 1), F32), pltpu.VMEM((2, FOX_TQ, 1), F32),
                        pltpu.VMEM((2, FOX_TQ, LANES), F32)],
        compiler_params=_cp(("parallel", "parallel", "arbitrary")),
        name="fox_prompt",
    )(qs, cq, kb, ck, vb)


def _fox_sample_kernel(q_ref, cq_ref, kc_ref, kn_ref, ck_ref, vc_ref, vn_ref, o_ref, *, t, past):
    hp = pl.program_id(1)
    lane = _lane_iota((1, LANES))
    qp = q_ref[...]
    cq = cq_ref[past:past + t, :]
    pad = jnp.zeros((LANES - t, LANES), BF16)
    kc = jnp.concatenate([kc_ref[...].astype(BF16), ck_ref[:past, :]], axis=1)
    kn = jnp.concatenate([jnp.concatenate([kn_ref[...], pad], axis=0), ck_ref[past:, :]], axis=1)
    vc = vc_ref[...].astype(BF16)
    vn = jnp.concatenate([vn_ref[...], pad], axis=0)
    vis = _lane_iota((t, LANES)) <= _row_iota((t, LANES))
    outs = []
    for par in range(2):
        qm = jnp.where(_half_mask(par), qp, jnp.zeros_like(qp))
        qa = jnp.where(_aug_mask(2 * hp + par), cq, jnp.zeros_like(cq))
        qc = jnp.concatenate([qm, qa], axis=1)
        s0 = _dot_nt(qc, kc)
        s1 = jnp.where(vis, _dot_nt(qc, kn), NEG)
        outs.append(_softmax_pv([s0, s1], [vc, vn]))
    o_ref[...] = jnp.where(lane < HEAD_DIM, outs[0], outs[1])


def _fox_sample(qs, kb, vb, cq, ck, cache_k, cache_v, row0, nseq, t, past):
    blk0 = row0 // t
    new = pl.BlockSpec((t, LANES), lambda s, hp: (blk0 + s, hp))
    cache = pl.BlockSpec((past, LANES), lambda s, hp: (s, hp))
    aug = pl.BlockSpec((None, past + LANES, LANES), lambda s, hp: (s, 0, 0))
    return pl.pallas_call(
        functools.partial(_fox_sample_kernel, t=t, past=past),
        grid=(nseq, C_HEADS // 2),
        in_specs=[new, aug, cache, new, aug, cache, new],
        out_specs=pl.BlockSpec((t, LANES), lambda s, hp: (s, hp)),
        out_shape=jax.ShapeDtypeStruct((nseq * t, C_WIDTH), F32),
        compiler_params=_cp(("parallel", "parallel")),
        name="fox_sample",
    )(qs, cq, cache_k, kb, ck, cache_v, vb)


def _cross_kernel(q_ref, k_ref, v_ref, o_ref):
    lane = _lane_iota((1, LANES))
    for hp in range(M_HEADS // 2):
        cols = slice(hp * LANES, (hp + 1) * LANES)
        qp = q_ref[:, cols]
        k = k_ref[:, cols].astype(BF16)
        v = v_ref[:, cols].astype(BF16)
        outs = []
        for par in range(2):
            qm = jnp.where(_half_mask(par), qp, jnp.zeros_like(qp))
            outs.append(_softmax_pv([_dot_nt(qm, k)], [v]))
        o_ref[:, cols] = jnp.where(lane < HEAD_DIM, outs[0], outs[1])


def _cross(qs, mk, mv, row0, nrows, rows_per_seq, name):
    n_mem = mk.shape[0] // (nrows // rows_per_seq)
    tq = _pick(rows_per_seq, (512, 256, 128, 64))
    per = rows_per_seq // tq
    blk0 = row0 // tq
    return pl.pallas_call(
        _cross_kernel,
        grid=(nrows // tq,),
        in_specs=[pl.BlockSpec((tq, M_WIDTH), lambda i: (blk0 + i, 0)),
                  pl.BlockSpec((n_mem, M_WIDTH), lambda i: (i // per, 0)),
                  pl.BlockSpec((n_mem, M_WIDTH), lambda i: (i // per, 0))],
        out_specs=pl.BlockSpec((tq, M_WIDTH), lambda i: (i, 0)),
        out_shape=jax.ShapeDtypeStruct((nrows, M_WIDTH), F32),
        compiler_params=_cp(("parallel",)),
        name=name,
    )(qs, mk, mv)


SSD_Q = 128
CONV_PAD = SUBLANES


def _ssd_kernel(xbc_ref, z_ref, sm_ref, cs0_ref, hs0_ref, cw_ref, cb_ref, dtb_ref, alog_ref, dsk_ref,
                gn_ref, exp_ref, tri_ref, qm_ref,
                o_ref, hs_out_ref, cs_out_ref, hs_ref, xext_ref, *, nv, nchunk):
    c = pl.program_id(1)
    q = SSD_Q

    @pl.when(c == 0)
    def _():
        hs_ref[...] = hs0_ref[...].T * qm_ref[...]
        xext_ref[0:CONV_PAD, :] = cs0_ref[...]

    xext_ref[CONV_PAD:CONV_PAD + nv, :] = xbc_ref[...]
    if nv < q:
        xext_ref[CONV_PAD + nv:CONV_PAD + q, :] = jnp.zeros((q - nv, B_XBC), F32)
    y = cb_ref[...]
    for tap in range(B_CONV):
        y = y + xext_ref[pl.ds(CONV_PAD - (B_CONV - 1) + tap, q), :] * cw_ref[tap:tap + 1, :]
    tail = xext_ref[nv:nv + CONV_PAD, :]
    xext_ref[0:CONV_PAD, :] = tail
    cs_out_ref[...] = tail
    xa = _silu(y)
    xs = xa[:, :B_WIDTH]
    bm = xa[:, B_WIDTH:B_WIDTH + LANES].astype(BF16)
    cm = xa[:, B_WIDTH + LANES:]

    lane = _lane_iota((1, LANES))
    dt = _softplus(sm_ref[...] + dtb_ref[...])
    if nv < q:
        dt = jnp.concatenate([dt, jnp.zeros((q - nv, LANES), F32)], axis=0)
    head_lane = (lane >= DT_LANE) & (lane < DT_LANE + B_HEADS)
    a_row = jnp.where(head_lane, -jnp.exp(alog_ref[...]), 0.0)
    dt = jnp.where(head_lane, dt, 0.0)
    acum = _sel_dot(tri_ref[...], dt * a_row)
    expand = exp_ref[...]
    dt_e = _dot_sel(dt, expand)
    acum_e = _dot_sel(acum, expand)
    alast_e = acum_e[q - 1:q, :]
    xdt = xs * dt_e

    hs = hs_ref[...]
    y_off = _dot(cm.astype(BF16), hs.astype(BF16)) * jnp.exp(acum_e)
    upd = _dot_tn(bm, (xdt * jnp.exp(alast_e - acum_e)).astype(BF16))
    hs_new = (jnp.exp(alast_e) * hs + upd) * qm_ref[...]
    hs_ref[...] = hs_new

    @pl.when(c == nchunk - 1)
    def _():
        hs_out_ref[...] = hs_new.T

    acum_t = acum.T
    causal = _lane_iota((q, q)) <= _row_iota((q, q))
    xdt_b = xdt.astype(BF16)
    heads_per_group = B_HEADS // B_GROUPS
    y_pairs = []
    pair = []
    for h in range(B_HEADS):
        grp = h // heads_per_group
        if h % heads_per_group == 0:
            cg = jnp.where(_half_mask(grp), cm, 0.0).astype(BF16)
            cbm = _dot_nt(cg, bm)
        seg = acum[:, DT_LANE + h:DT_LANE + h + 1] - acum_t[DT_LANE + h:DT_LANE + h + 1, :]
        lmat = jnp.exp(jnp.where(causal, seg, -jnp.inf))
        hp = h // 2
        pair.append(_dot((cbm * lmat).astype(BF16), xdt_b[:, hp * LANES:(hp + 1) * LANES]))
        if h % 2 == 1:
            y_pairs.append(jnp.where(lane < HEAD_DIM, pair[0], pair[1]))
            pair = []
    y_diag = jnp.concatenate(y_pairs, axis=1)

    yt = y_off + y_diag + dsk_ref[...] * xs
    if nv < q:
        yt = yt[:nv]
    yz = yt * _silu(z_ref[...])
    ms = jnp.mean(yz * yz, axis=-1, keepdims=True)
    o_ref[...] = yz * lax.rsqrt(ms + EPS) * gn_ref[...]


def _ssd(u, cs0, hs0, consts, row0, nseq, rows_per_seq):
    nv = min(SSD_Q, rows_per_seq)
    nchunk = rows_per_seq // nv
    blk0 = row0 // nv
    cw, cb, dtb, alog, dsk, gn, expand, tri, qmask = consts

    def ucol(name, width):
        idx = _U_OFF[name] // width
        assert idx * width == _U_OFF[name]
        return pl.BlockSpec((nv, width), lambda s, c: (blk0 + s * nchunk + c, idx))

    full = lambda a: pl.BlockSpec(a.shape, lambda s, c: (0,) * a.ndim)
    cs_spec = pl.BlockSpec((None, CONV_PAD, B_XBC), lambda s, c: (s, 0, 0))
    hs_spec = pl.BlockSpec((None, B_WIDTH, LANES), lambda s, c: (s, 0, 0))
    return pl.pallas_call(
        functools.partial(_ssd_kernel, nv=nv, nchunk=nchunk),
        grid=(nseq, nchunk),
        in_specs=[ucol('b_xbc', B_XBC), ucol('b_z', B_WIDTH), ucol('b_dt', LANES), cs_spec, hs_spec,
                  full(cw), full(cb), full(dtb), full(alog), full(dsk), full(gn), full(expand), full(tri),
                  full(qmask)],
        out_specs=[pl.BlockSpec((nv, B_WIDTH), lambda s, c: (s * nchunk + c, 0)), hs_spec, cs_spec],
        out_shape=[jax.ShapeDtypeStruct((nseq * rows_per_seq, B_WIDTH), F32),
                   jax.ShapeDtypeStruct((nseq, B_WIDTH, LANES), F32),
                   jax.ShapeDtypeStruct((nseq, CONV_PAD, B_XBC), F32)],
        scratch_shapes=[pltpu.VMEM((LANES, B_WIDTH), F32), pltpu.VMEM((SSD_Q + CONV_PAD, B_XBC), F32)],
        compiler_params=_cp(("parallel", "arbitrary")),
        name="ssd",
    )(u, u, u, cs0, hs0, cw, cb, dtb, alog, dsk, gn, expand, tri, qmask)


def _ssd_static_consts():
    expand = np.zeros((LANES, B_WIDTH), np.float32)
    for h in range(B_HEADS):
        expand[DT_LANE + h, h * HEAD_DIM:(h + 1) * HEAD_DIM] = 1.0
    tri = np.tril(np.ones((SSD_Q, SSD_Q), np.float32))
    qmask = np.zeros((LANES, B_WIDTH), np.float32)
    half = B_WIDTH // B_GROUPS
    for g in range(B_GROUPS):
        qmask[g * B_STATE:(g + 1) * B_STATE, g * half:(g + 1) * half] = 1.0
    return jnp.asarray(expand, BF16), jnp.asarray(tri, BF16), jnp.asarray(qmask)


def _stack_state(h):
    r = h.reshape(h.shape[0], B_WIDTH, B_STATE)
    return jnp.concatenate([r, r], axis=2)


def _unstack_state(hst):
    n = hst.shape[0]
    first = (np.arange(B_WIDTH) < B_WIDTH // B_GROUPS)[None, :, None]
    return jnp.where(first, hst[:, :, :B_STATE], hst[:, :, B_STATE:]).reshape(n, B_HEADS, HEAD_DIM, B_STATE)


def _merge_kernel(x, oa, az, ob, oc, cz, om, mz, gt, wpa, wpb, wpc, wpm, wout, o):
    d = x.shape[1]
    pa = _dot((oa[...] * _silu(az[...])).astype(BF16), wpa[...])
    pb = _dot(ob[...].astype(BF16), wpb[...])
    pc = _dot((oc[...] * _silu(cz[...])).astype(BF16), wpc[...])
    pm = _dot((om[...] * _silu(mz[...])).astype(BF16), wpm[...])
    mix = (gt[:, 0:d].astype(F32) * pa + gt[:, d:2 * d].astype(F32) * pb
           + gt[:, 2 * d:3 * d].astype(F32) * pc + gt[:, 3 * d:4 * d].astype(F32) * pm)
    o[...] = x[...] + _dot(mix.astype(BF16), wout[...])


def _merge(x, u, gates, oa, ob, oc, om, wpa, wpb, wpc, wpm, wout):
    m, d = x.shape
    tm = _pick(m, (512, 256, 128))

    def ucol(name, width):
        idx = _U_OFF[name] // width
        assert idx * width == _U_OFF[name]
        return pl.BlockSpec((tm, width), lambda i: (i, idx))

    def rows(width):
        return pl.BlockSpec((tm, width), lambda i: (i, 0))

    full = lambda a: pl.BlockSpec(a.shape, lambda i: (0, 0))
    return pl.pallas_call(
        _merge_kernel,
        grid=(m // tm,),
        in_specs=[rows(d), rows(A_WIDTH), ucol('a_z', A_WIDTH), rows(B_WIDTH), rows(C_WIDTH),
                  ucol('c_z', C_WIDTH), rows(M_WIDTH), ucol('m_z', M_WIDTH), rows(N_BRANCH * d),
                  full(wpa), full(wpb), full(wpc), full(wpm), full(wout)],
        out_specs=rows(d),
        out_shape=jax.ShapeDtypeStruct((m, d), F32),
        compiler_params=_cp(("parallel",)),
        name="merge",
    )(x, oa, u, ob, oc, u, om, u, gates, wpa, wpb, wpc, wpm, wout)


def _row(v, width=None):
    v = v.reshape(1, -1).astype(F32)
    if width is not None and v.shape[1] < width:
        v = jnp.pad(v, ((0, 0), (0, width - v.shape[1])))
    return v


def _lane_row(v, lane0):
    return jnp.zeros((1, LANES), F32).at[0, lane0:lane0 + v.shape[0]].set(v.astype(F32))


def _block_diag_mean(width):
    seg = np.arange(width) // HEAD_DIM
    return jnp.asarray((seg[:, None] == seg[None, :]).astype(np.float32) / HEAD_DIM, BF16)


def kernel(x_prompt, x_sample, mem_prompt, cache_a_k, cache_a_v, cache_c_k, cache_c_v, cache_c_logf, state_b_ssm, state_b_conv, cache_mem_k, cache_mem_v, g_norm, w_in, a_qnorm, a_knorm, a_rel, b_conv_w, b_conv_b, b_dt_bias, b_a_log, b_d, b_norm, c_qnorm, c_knorm, c_fbias, m_norm, w_mkv, m_qnorm, m_knorm, w_pa, w_pb, w_pc, w_pm, w_out):
    nb, seq, d = x_prompt.shape
    ns, tdec, _ = x_sample.shape
    depth = g_norm.shape[0]
    n_mem = mem_prompt.shape[1]
    past = cache_c_k.shape[2]
    la_cache = cache_a_k.shape[2]
    mp = nb * seq
    md = ns * tdec
    la = min(A_WIN, seq)
    assert seq % FOX_TK == 0 and seq % BAND_Q == 0 and seq % SSD_Q == 0
    assert tdec == CHUNK and la_cache == A_WIN and past % LANES == 0

    x = jnp.concatenate([x_prompt.reshape(mp, d), x_sample.reshape(md, d)], axis=0)
    mem = mem_prompt.reshape(nb * n_mem, d)

    bd512 = _block_diag_mean(512)
    bd256 = _block_diag_mean(M_WIDTH)
    cum_consts = _cum_consts()
    expand, tri_q, qmask = _ssd_static_consts()
    scale = HEAD_DIM ** -0.5 * LOG2E

    outs = {k: [] for k in ('pa_k', 'pa_v', 'pc_k', 'pc_v', 'pc_f', 'pb_s', 'pb_c', 'pm_k', 'pm_v',
                            'sa_k', 'sa_v', 'sc_k', 'sc_v', 'sc_f', 'sb_s', 'sb_c')}

    for l in range(depth):
        wl = w_in[l]
        pieces = [wl[:, _SRC[n][0]:_SRC[n][0] + _SRC[n][1]] for n in _U_ORDER]
        used = sum(p.shape[1] for p in pieces)
        w_u = jnp.concatenate(pieces + [jnp.zeros((d, U_WIDTH - used), F32)], axis=1).astype(BF16)
        g0, gw = _SRC['gate']
        w_g = wl[:, g0:g0 + gw].astype(BF16)
        g_row = _row(g_norm[l])

        u = _proj(x, g_row, w_u, act=None, out_dtype=F32, name="proj_u")
        gates = _proj(x, g_row, w_g, act='sigmoid', out_dtype=BF16, name="proj_gate")

        (a_qs, a_kf, a_kb, a_vb, c_qs, c_kf, c_kb, c_vb, m_qs, lf) = _prep(
            u, _row(jnp.tile(a_qnorm[l], A_HEADS)) * scale, _row(jnp.tile(a_knorm[l], A_HEADS)),
            _row(jnp.tile(c_qnorm[l], C_HEADS)) * scale, _row(jnp.tile(c_knorm[l], C_HEADS)),
            _row(jnp.tile(m_qnorm[l], M_HEADS)) * scale, _lane_row(c_fbias[l], CF_LANE), bd512)

        def ucols(name):
            return u[:, _U_OFF[name]:_U_OFF[name] + _SRC[name][1]]

        kv = _proj(mem, _row(m_norm[l]), w_mkv[l].astype(BF16), act=None, out_dtype=F32, name="proj_mem")
        mk = _memk(kv[:, :M_WIDTH], _row(jnp.tile(m_knorm[l], M_HEADS)), bd256)
        mv = kv[:, M_WIDTH:]

        bias = _band_bias(a_rel[l])
        o_a_p = _band_prompt(a_qs, a_kb, a_vb, bias, nb, seq)
        bias_s = jnp.concatenate(
            [bias[:, :tdec, :la_cache + tdec],
             jnp.full((A_HEADS, tdec, LANES - tdec), NEG, F32)], axis=2)
        o_a_s = _band_sample(a_qs, a_kb, a_vb, cache_a_k[l].reshape(ns * la_cache, A_WIDTH),
                             cache_a_v[l].reshape(ns * la_cache, A_WIDTH), bias_s, mp, ns, tdec)

        lf_p = lf[:mp].reshape(nb, seq, LANES)
        cq_p, ck_p = _cum(lf_p, cum_consts)
        o_c_p = _fox_prompt(c_qs, c_kb, c_vb, cq_p.reshape(mp, LANES), ck_p.reshape(mp, LANES), nb, seq)
        lf_cache = jnp.pad(cache_c_logf[l].astype(F32), ((0, 0), (0, 0), (CF_LANE, LANES - CF_LANE - C_HEADS)))
        lf_s = jnp.concatenate([lf_cache, lf[mp:].reshape(ns, tdec, LANES),
                                jnp.zeros((ns, LANES - tdec, LANES), F32)], axis=1)
        cq_s, ck_s = _cum(lf_s, cum_consts)
        o_c_s = _fox_sample(c_qs, c_kb, c_vb, cq_s, ck_s, cache_c_k[l].reshape(ns * past, C_WIDTH),
                            cache_c_v[l].reshape(ns * past, C_WIDTH), mp, ns, tdec, past)

        ssd_consts = (jnp.pad(b_conv_w[l], ((0, SUBLANES - B_CONV), (0, 0))), _row(b_conv_b[l]),
                      _lane_row(b_dt_bias[l], DT_LANE), _lane_row(b_a_log[l], DT_LANE),
                      _row(jnp.repeat(b_d[l], HEAD_DIM)), _row(b_norm[l]), expand, tri_q, qmask)
        o_b_p, hs_p, cs_p = _ssd(u, jnp.zeros((nb, CONV_PAD, B_XBC), F32),
                                 jnp.zeros((nb, B_WIDTH, LANES), F32), ssd_consts, 0, nb, seq)
        cs0 = jnp.pad(state_b_conv[l].astype(F32), ((0, 0), (CONV_PAD - (B_CONV - 1), 0), (0, 0)))
        o_b_s, hs_s, cs_s = _ssd(u, cs0, _stack_state(state_b_ssm[l].astype(F32)), ssd_consts, mp, ns, tdec)

        o_m_p = _cross(m_qs, mk, mv, 0, mp, seq, "cross_prompt")
        o_m_s = _cross(m_qs, cache_mem_k[l].reshape(ns * n_mem, M_WIDTH),
                       cache_mem_v[l].reshape(ns * n_mem, M_WIDTH), mp, md, tdec, "cross_sample")

        cat = lambda p, s: jnp.concatenate([p, s], axis=0)
        x = _merge(x, u, gates, cat(o_a_p, o_a_s), cat(o_b_p, o_b_s), cat(o_c_p, o_c_s), cat(o_m_p, o_m_s),
                   w_pa[l].astype(BF16), w_pb[l].astype(BF16), w_pc[l].astype(BF16), w_pm[l].astype(BF16),
                   w_out[l].astype(BF16))

        a_v = ucols('a_v')
        c_v = ucols('c_v')
        outs['pa_k'].append(a_kf[:mp].reshape(nb, seq, A_HEADS, HEAD_DIM)[:, -la:])
        outs['pa_v'].append(a_v[:mp].reshape(nb, seq, A_HEADS, HEAD_DIM)[:, -la:])
        outs['pc_k'].append(c_kf[:mp].reshape(nb, seq, C_HEADS, HEAD_DIM))
        outs['pc_v'].append(c_v[:mp].reshape(nb, seq, C_HEADS, HEAD_DIM))
        outs['pc_f'].append(lf[:mp, CF_LANE:CF_LANE + C_HEADS].reshape(nb, seq, C_HEADS))
        outs['pb_s'].append(_unstack_state(hs_p))
        outs['pb_c'].append(cs_p[:, CONV_PAD - (B_CONV - 1):])
        outs['pm_k'].append(mk.reshape(nb, n_mem, M_HEADS, HEAD_DIM))
        outs['pm_v'].append(mv.reshape(nb, n_mem, M_HEADS, HEAD_DIM))
        outs['sa_k'].append(a_kf[mp:].reshape(ns, tdec, A_HEADS, HEAD_DIM))
        outs['sa_v'].append(a_v[mp:].reshape(ns, tdec, A_HEADS, HEAD_DIM))
        outs['sc_k'].append(c_kf[mp:].reshape(ns, tdec, C_HEADS, HEAD_DIM))
        outs['sc_v'].append(c_v[mp:].reshape(ns, tdec, C_HEADS, HEAD_DIM))
        outs['sc_f'].append(lf[mp:, CF_LANE:CF_LANE + C_HEADS].reshape(ns, tdec, C_HEADS))
        outs['sb_s'].append(_unstack_state(hs_s))
        outs['sb_c'].append(cs_s[:, CONV_PAD - (B_CONV - 1):])

    st = jnp.stack
    return (x[:mp].reshape(nb, seq, d), x[mp:].reshape(ns, tdec, d),
            st(outs['pa_k']), st(outs['pa_v']), st(outs['pc_k']), st(outs['pc_v']), st(outs['pc_f']),
            st(outs['pb_s']), st(outs['pb_c']), st(outs['pm_k']), st(outs['pm_v']),
            st(outs['sa_k']), st(outs['sa_v']), st(outs['sc_k']), st(outs['sc_v']), st(outs['sc_f']),
            st(outs['sb_s']), st(outs['sb_c']))
```

```python
import functools
import math

import numpy as np
import jax
import jax.numpy as jnp
from jax import lax
from jax.experimental import pallas as pl
from jax.experimental.pallas import tpu as pltpu

F32 = jnp.float32
BF16 = jnp.bfloat16

EPS = 1e-6
NEG = -1e30
LOG2E = 1.4426950408889634

HEAD_DIM = 64
CHUNK = 64
A_HEADS = 8
A_WIDTH = 512
A_LEFT_CHUNKS = 8
A_WIN = A_LEFT_CHUNKS * CHUNK
A_REL_CLIP = 128
B_HEADS = 8
B_WIDTH = 512
B_GROUPS = 2
B_STATE = 64
B_CONV = 4
B_XBC = B_WIDTH + 2 * B_GROUPS * B_STATE
C_HEADS = 8
C_WIDTH = 512
M_HEADS = 4
M_WIDTH = 256
N_BRANCH = 4

LANES = 128
SUBLANES = 8
VMEM_LIMIT = 48 * 1024 * 1024

_SPLITS = (('a_q', A_WIDTH), ('a_k', A_WIDTH), ('a_v', A_WIDTH), ('a_z', A_WIDTH),
           ('b_z', B_WIDTH), ('b_xbc', B_XBC), ('b_dt', B_HEADS),
           ('c_q', C_WIDTH), ('c_k', C_WIDTH), ('c_v', C_WIDTH), ('c_f', C_HEADS), ('c_z', C_WIDTH),
           ('m_q', M_WIDTH), ('m_z', M_WIDTH), ('gate', N_BRANCH * 1024))
_SRC = {}
_off = 0
for _n, _w in _SPLITS:
    _SRC[_n] = (_off, _w)
    _off += _w

_U_ORDER = ('a_q', 'a_k', 'a_v', 'a_z', 'c_q', 'c_k', 'c_v', 'c_z', 'b_z', 'b_xbc', 'm_q', 'm_z', 'b_dt', 'c_f')
_U_OFF = {}
_off = 0
for _n in _U_ORDER:
    _U_OFF[_n] = _off
    _off += _SRC[_n][1]
U_SMALL = _U_OFF['b_dt']
U_WIDTH = 6144
DT_LANE = 0
CF_LANE = B_HEADS
AUG_PER_HEAD = 6


def _cp(sem):
    return pltpu.CompilerParams(dimension_semantics=sem, vmem_limit_bytes=VMEM_LIMIT)


def _pick(n, cands):
    for c in cands:
        if n % c == 0:
            return c
    raise ValueError(f"no tile for {n} in {cands}")


def _dot(a, b):
    return jnp.dot(a, b, preferred_element_type=F32)


def _dot_nt(a, b):
    return lax.dot_general(a, b, (((1,), (1,)), ((), ())), preferred_element_type=F32)


def _dot_tn(a, b):
    return lax.dot_general(a, b, (((0,), (0,)), ((), ())), preferred_element_type=F32)


def _split3(x):
    hi = x.astype(BF16)
    r = x - hi.astype(F32)
    mid = r.astype(BF16)
    lo = (r - mid.astype(F32)).astype(BF16)
    return hi, mid, lo


def _dot_sel(x, sel):
    hi, mid, lo = _split3(x)
    return _dot(hi, sel) + _dot(mid, sel) + _dot(lo, sel)


def _sel_dot(sel, x):
    hi, mid, lo = _split3(x)
    return _dot(sel, hi) + _dot(sel, mid) + _dot(sel, lo)


def _sigmoid(x):
    return 1.0 / (1.0 + jnp.exp(-x))


def _silu(x):
    return x * _sigmoid(x)


def _softplus(x):
    return jnp.maximum(x, 0.0) + jnp.log1p(jnp.exp(-jnp.abs(x)))


def _log_sigmoid(x):
    return jnp.minimum(x, 0.0) - jnp.log1p(jnp.exp(-jnp.abs(x)))


def _head_norm(x, bd):
    x2 = x * x
    hi = x2.astype(BF16)
    lo = (x2 - hi.astype(F32)).astype(BF16)
    ms = _dot(hi, bd) + _dot(lo, bd)
    return x * lax.rsqrt(ms + EPS)


def _lane_iota(shape):
    return lax.broadcasted_iota(jnp.int32, shape, len(shape) - 1)


def _row_iota(shape):
    return lax.broadcasted_iota(jnp.int32, shape, len(shape) - 2)


def _proj_kernel(x_ref, g_ref, w_ref, o_ref, h_ref, *, act, w_rows):
    @pl.when(pl.program_id(1) == 0)
    def _():
        x = x_ref[...]
        ms = jnp.mean(x * x, axis=-1, keepdims=True)
        h_ref[...] = (x * lax.rsqrt(ms + EPS) * g_ref[...]).astype(BF16)

    u = _dot_nt(h_ref[...], w_ref[...]) if w_rows else _dot(h_ref[...], w_ref[...])
    if act == 'sigmoid':
        u = _sigmoid(u)
    o_ref[...] = u.astype(o_ref.dtype)


def _proj(x, g_row, w_bf, *, act, out_dtype, name, w_rows):
    m, d = x.shape
    n = w_bf.shape[0] if w_rows else w_bf.shape[1]
    tm = _pick(m, (1024, 512, 256, 128))
    tn = _pick(n, (1024, 512, 256, 128))
    w_spec = pl.BlockSpec((tn, d), lambda i, j: (j, 0)) if w_rows else pl.BlockSpec((d, tn), lambda i, j: (0, j))
    return pl.pallas_call(
        functools.partial(_proj_kernel, act=act, w_rows=w_rows),
        grid=(m // tm, n // tn),
        in_specs=[pl.BlockSpec((tm, d), lambda i, j: (i, 0)),
                  pl.BlockSpec((1, d), lambda i, j: (0, 0)),
                  w_spec],
        out_specs=pl.BlockSpec((tm, tn), lambda i, j: (i, j)),
        out_shape=jax.ShapeDtypeStruct((m, n), out_dtype),
        scratch_shapes=[pltpu.VMEM((tm, d), BF16)],
        compiler_params=_cp(("parallel", "arbitrary")),
        name=name,
    )(x, g_row, w_bf)


def _prep_kernel(aq, ak, av, cq, ck, cv, mq, sm, gaq, gak, gcq, gck, gmq, fb, bd,
                 o_aq, o_akf, o_akb, o_avb, o_cq, o_ckf, o_ckb, o_cvb, o_mq, o_lf):
    bdv = bd[...]
    o_aq[...] = (_head_norm(aq[...], bdv) * gaq[...]).astype(BF16)
    akn = _head_norm(ak[...], bdv) * gak[...]
    o_akf[...] = akn
    o_akb[...] = akn.astype(BF16)
    o_avb[...] = av[...].astype(BF16)
    o_cq[...] = (_head_norm(cq[...], bdv) * gcq[...]).astype(BF16)
    ckn = _head_norm(ck[...], bdv) * gck[...]
    o_ckf[...] = ckn
    o_ckb[...] = ckn.astype(BF16)
    o_cvb[...] = cv[...].astype(BF16)
    o_mq[...] = (_head_norm(mq[...], bd[:M_WIDTH, :M_WIDTH]) * gmq[...]).astype(BF16)
    o_lf[...] = _log_sigmoid(sm[...] + fb[...])


def _prep(u, gaq, gak, gcq, gck, gmq, fb, bd):
    m = u.shape[0]
    ts = _pick(m, (512, 256, 128))

    def ucol(name, width):
        idx = _U_OFF[name] // width
        assert idx * width == _U_OFF[name]
        return pl.BlockSpec((ts, width), lambda i: (i, idx))

    def row(width):
        return pl.BlockSpec((1, width), lambda i: (0, 0))

    def out(width):
        return pl.BlockSpec((ts, width), lambda i: (i, 0))

    sds = jax.ShapeDtypeStruct
    return pl.pallas_call(
        _prep_kernel,
        grid=(m // ts,),
        in_specs=[ucol('a_q', 512), ucol('a_k', 512), ucol('a_v', 512),
                  ucol('c_q', 512), ucol('c_k', 512), ucol('c_v', 512),
                  ucol('m_q', 256), ucol('b_dt', LANES),
                  row(512), row(512), row(512), row(512), row(256), row(LANES),
                  pl.BlockSpec((512, 512), lambda i: (0, 0))],
        out_specs=[out(512), out(512), out(512), out(512), out(512), out(512), out(512), out(512),
                   out(256), out(LANES)],
        out_shape=[sds((m, 512), BF16), sds((m, 512), F32), sds((m, 512), BF16), sds((m, 512), BF16),
                   sds((m, 512), BF16), sds((m, 512), F32), sds((m, 512), BF16), sds((m, 512), BF16),
                   sds((m, 256), BF16), sds((m, LANES), F32)],
        compiler_params=_cp(("parallel",)),
        name="prep",
    )(u, u, u, u, u, u, u, u, gaq, gak, gcq, gck, gmq, fb, bd)


def _memk_kernel(kv, g, bd, o):
    o[...] = _head_norm(kv[...], bd[...]) * g[...]


def _memk(kv, g_row, bd256):
    m = kv.shape[0]
    ts = _pick(m, (512, 256, 128))
    return pl.pallas_call(
        _memk_kernel,
        grid=(m // ts,),
        in_specs=[pl.BlockSpec((ts, M_WIDTH), lambda i: (i, 0)),
                  pl.BlockSpec((1, M_WIDTH), lambda i: (0, 0)),
                  pl.BlockSpec((M_WIDTH, M_WIDTH), lambda i: (0, 0))],
        out_specs=pl.BlockSpec((ts, M_WIDTH), lambda i: (i, 0)),
        out_shape=jax.ShapeDtypeStruct((m, M_WIDTH), F32),
        compiler_params=_cp(("parallel",)),
        name="memk",
    )(kv, g_row, bd256)


def _cum_kernel(lf_ref, tri_ref, selq_ref, selk_ref, qc_ref, kc_ref, cq_ref, ck_ref, *, nchunk):
    tri = tri_ref[...]

    def body(c, carry):
        r = pl.multiple_of(c * LANES, LANES)
        x = lf_ref[pl.ds(r, LANES), :]
        cs = _sel_dot(tri, x) + carry
        z = cs * LOG2E
        zh, zm, zl = _split3(z)
        q = _dot(zh, selq_ref[0]) + _dot(zm, selq_ref[1]) + _dot(zl, selq_ref[2]) + qc_ref[...]
        k = _dot(zh, selk_ref[0]) + _dot(zm, selk_ref[1]) + _dot(zl, selk_ref[2]) + kc_ref[...]
        cq_ref[pl.ds(r, LANES), :] = q.astype(BF16)
        ck_ref[pl.ds(r, LANES), :] = k.astype(BF16)
        return cs[LANES - 1:LANES, :]

    lax.fori_loop(0, nchunk, body, jnp.zeros((1, LANES), F32))


def _cum(lf, consts):
    nseq, length, _ = lf.shape
    assert length % LANES == 0
    tri, selq, selk, qc, kc = consts
    full = lambda shape: pl.BlockSpec(shape, lambda s: (0,) * len(shape))
    seq = pl.BlockSpec((None, length, LANES), lambda s: (s, 0, 0))
    return pl.pallas_call(
        functools.partial(_cum_kernel, nchunk=length // LANES),
        grid=(nseq,),
        in_specs=[seq, full((LANES, LANES)), full((3, LANES, LANES)), full((3, LANES, LANES)),
                  full((1, LANES)), full((1, LANES))],
        out_specs=[seq, seq],
        out_shape=[jax.ShapeDtypeStruct(lf.shape, BF16)] * 2,
        compiler_params=_cp(("parallel",)),
        name="cum",
    )(lf, tri, selq, selk, qc, kc)


def _cum_consts():
    tri = np.tril(np.ones((LANES, LANES), np.float32))
    selq = np.zeros((3, LANES, LANES), np.float32)
    selk = np.zeros((3, LANES, LANES), np.float32)
    qc = np.zeros((1, LANES), np.float32)
    kc = np.zeros((1, LANES), np.float32)
    for h in range(C_HEADS):
        for p in range(3):
            selq[p, CF_LANE + h, AUG_PER_HEAD * h + p] = 1.0
            selk[p, CF_LANE + h, AUG_PER_HEAD * h + 3 + p] = -1.0
            qc[0, AUG_PER_HEAD * h + 3 + p] = 1.0
            kc[0, AUG_PER_HEAD * h + p] = 1.0
    return (jnp.asarray(tri, BF16), jnp.asarray(selq, BF16), jnp.asarray(selk, BF16),
            jnp.asarray(qc), jnp.asarray(kc))


BAND_Q = 4 * CHUNK
BAND_K = 3 * BAND_Q
TAB_PAD = 384


def _bias_kernel(tab_ref, idx_ref, neg_ref, o_ref):
    tn = idx_ref.shape[1]
    onehot = (lax.broadcasted_iota(jnp.int32, (TAB_PAD, tn), 0) == idx_ref[...]).astype(BF16)
    o_ref[...] = _dot_sel(tab_ref[...], onehot) * LOG2E + neg_ref[...]


def _band_bias(table):
    r = np.arange(BAND_Q)[:, None]
    t = np.arange(BAND_K)[None, :]
    idx = (np.clip(A_WIN + r - t, -A_REL_CLIP, A_REL_CLIP) + A_REL_CLIP).astype(np.int32).reshape(1, -1)
    cb = t // CHUNK - r // CHUNK
    neg = np.where((cb >= 0) & (cb <= A_LEFT_CHUNKS), 0.0, NEG).astype(np.float32).reshape(1, -1)
    n = idx.shape[1]
    tn = 8192
    tab = jnp.zeros((A_HEADS, TAB_PAD), F32).at[:, :table.shape[0]].set(table.T)
    out = pl.pallas_call(
        _bias_kernel,
        grid=(n // tn,),
        in_specs=[pl.BlockSpec((A_HEADS, TAB_PAD), lambda j: (0, 0)),
                  pl.BlockSpec((1, tn), lambda j: (0, j)),
                  pl.BlockSpec((1, tn), lambda j: (0, j))],
        out_specs=pl.BlockSpec((A_HEADS, tn), lambda j: (0, j)),
        out_shape=jax.ShapeDtypeStruct((A_HEADS, n), F32),
        compiler_params=_cp(("parallel",)),
        name="band_bias",
    )(tab, jnp.asarray(idx), jnp.asarray(neg))
    return out.reshape(A_HEADS, BAND_Q, BAND_K)


def _half_mask(par):
    lane = _lane_iota((1, LANES))
    return (lane < HEAD_DIM) if par == 0 else (lane >= HEAD_DIM)


def _softmax_pv(s_blocks, v_blocks):
    m = s_blocks[0].max(axis=1, keepdims=True)
    for s in s_blocks[1:]:
        m = jnp.maximum(m, s.max(axis=1, keepdims=True))
    l = None
    acc = None
    for s, v in zip(s_blocks, v_blocks):
        p = jnp.exp2(s - m)
        ps = p.sum(axis=1, keepdims=True)
        pv = _dot(p.astype(BF16), v)
        l = ps if l is None else l + ps
        acc = pv if acc is None else acc + pv
    return acc / l


def _band_prompt_kernel(q_ref, k0, k1, k2, v0, v1, v2, bias_ref, o_ref):
    g = pl.program_id(1)
    lane = _lane_iota((1, LANES))
    krefs = (k0, k1, k2)
    vrefs = (v0, v1, v2)
    for hp in range(A_HEADS // 2):
        cols = slice(hp * LANES, (hp + 1) * LANES)
        qp = q_ref[:, cols]
        ks = [kr[:, cols] for kr in krefs]
        vs = [vr[:, cols] for vr in vrefs]
        outs = []
        for par in range(2):
            h = 2 * hp + par
            qm = jnp.where(_half_mask(par), qp, jnp.zeros_like(qp))
            sb = []
            for j in range(3):
                s = _dot_nt(qm, ks[j]) + bias_ref[h, :, j * BAND_Q:(j + 1) * BAND_Q]
                if j < 2:
                    s = jnp.where(g + j >= 2, s, NEG)
                sb.append(s)
            outs.append(_softmax_pv(sb, vs))
        o_ref[:, cols] = jnp.where(lane < HEAD_DIM, outs[0], outs[1])


def _band_prompt(qs, kb, vb, bias, nb, seq):
    ng = seq // BAND_Q

    def kv(j):
        return pl.BlockSpec((BAND_Q, A_WIDTH), lambda b, g: (b * ng + jnp.maximum(g - 2 + j, 0), 0))

    cur = pl.BlockSpec((BAND_Q, A_WIDTH), lambda b, g: (b * ng + g, 0))
    return pl.pallas_call(
        _band_prompt_kernel,
        grid=(nb, ng),
        in_specs=[cur, kv(0), kv(1), kv(2), kv(0), kv(1), kv(2),
                  pl.BlockSpec((A_HEADS, BAND_Q, BAND_K), lambda b, g: (0, 0, 0))],
        out_specs=cur,
        out_shape=jax.ShapeDtypeStruct((nb * seq, A_WIDTH), F32),
        compiler_params=_cp(("parallel", "parallel")),
        name="band_prompt",
    )(qs, kb, kb, kb, vb, vb, vb, bias)


def _band_sample_kernel(q_ref, kc_ref, kn_ref, vc_ref, vn_ref, bias_ref, o_ref, *, t):
    lane = _lane_iota((1, LANES))
    for hp in range(A_HEADS // 2):
        cols = slice(hp * LANES, (hp + 1) * LANES)
        qp = q_ref[:, cols]
        pad = jnp.zeros((LANES - t, LANES), BF16)
        kc = kc_ref[:, cols].astype(BF16)
        vc = vc_ref[:, cols].astype(BF16)
        kn = jnp.concatenate([kn_ref[:, cols], pad], axis=0)
        vn = jnp.concatenate([vn_ref[:, cols], pad], axis=0)
        lc = kc.shape[0]
        outs = []
        for par in range(2):
            h = 2 * hp + par
            qm = jnp.where(_half_mask(par), qp, jnp.zeros_like(qp))
            s0 = _dot_nt(qm, kc) + bias_ref[h, :, :lc]
            s1 = _dot_nt(qm, kn) + bias_ref[h, :, lc:]
            outs.append(_softmax_pv([s0, s1], [vc, vn]))
        o_ref[:, cols] = jnp.where(lane < HEAD_DIM, outs[0], outs[1])


def _band_sample(qs, kb, vb, cache_k, cache_v, bias_s, row0, nseq, t):
    lc = cache_k.shape[0] // nseq
    blk0 = row0 // t
    new = pl.BlockSpec((t, A_WIDTH), lambda s: (blk0 + s, 0))
    cache = pl.BlockSpec((lc, A_WIDTH), lambda s: (s, 0))
    return pl.pallas_call(
        functools.partial(_band_sample_kernel, t=t),
        grid=(nseq,),
        in_specs=[new, cache, new, cache, new,
                  pl.BlockSpec(bias_s.shape, lambda s: (0, 0, 0))],
        out_specs=pl.BlockSpec((t, A_WIDTH), lambda s: (s, 0)),
        out_shape=jax.ShapeDtypeStruct((nseq * t, A_WIDTH), F32),
        compiler_params=_cp(("parallel",)),
        name="band_sample",
    )(qs, cache_k, kb, cache_v, vb, bias_s)


FOX_TQ = 256
FOX_TK = 512


def _aug_mask(h):
    lane = _lane_iota((1, LANES))
    return (lane >= AUG_PER_HEAD * h) & (lane < AUG_PER_HEAD * (h + 1))


def _fox_prompt_kernel(q_ref, cq_ref, k_ref, ck_ref, v_ref, o_ref, vt_ref, st_ref, m_ref, l_ref, acc_ref):
    hp = pl.program_id(1)
    i = pl.program_id(2)
    seq = v_ref.shape[0]

    @pl.when(i == 0)
    def _():
        for c in range(seq // FOX_TQ):
            vt_ref[c] = v_ref[c * FOX_TQ:(c + 1) * FOX_TQ, :].astype(F32).T.astype(BF16)

    qp = q_ref[...]
    cq = cq_ref[...]
    qcs = []
    for par in range(2):
        qm = jnp.where(_half_mask(par), qp, jnp.zeros_like(qp))
        qa = jnp.where(_aug_mask(2 * hp + par), cq, jnp.zeros_like(cq))
        qcs.append(jnp.concatenate([qm, qa], axis=1))
    qcat = jnp.concatenate(qcs, axis=0)
    m_ref[...] = jnp.full(m_ref.shape, NEG, F32)
    l_ref[...] = jnp.zeros(l_ref.shape, F32)
    acc_ref[...] = jnp.zeros(acc_ref.shape, F32)
    ratio = FOX_TK // FOX_TQ

    def scores(j, slot):
        off = pl.multiple_of(j * FOX_TK, FOX_TK)
        kc = jnp.concatenate([k_ref[pl.ds(off, FOX_TK), :], ck_ref[pl.ds(off, FOX_TK), :]], axis=1)
        st_ref[slot] = _dot_nt(kc, qcat)

    def reduce(j, slot, masked):
        st = st_ref[slot]
        if masked:
            shape = (FOX_TK, 2 * FOX_TQ)
            qpos = (_lane_iota(shape) & (FOX_TQ - 1)) + (i % ratio) * FOX_TQ
            st = jnp.where(_row_iota(shape) <= qpos, st, NEG)
        m_old = m_ref[...]
        m_new = jnp.maximum(m_old, st.max(axis=0, keepdims=True))
        alpha = jnp.exp2(m_old - m_new)
        p = jnp.exp2(st - m_new)
        l_ref[...] = alpha * l_ref[...] + p.sum(axis=0, keepdims=True)
        pb = p.astype(BF16)
        for par in range(2):
            cols = slice(par * FOX_TQ, (par + 1) * FOX_TQ)
            pv = None
            for c in range(ratio):
                vt = vt_ref[j * ratio + c, par * HEAD_DIM:(par + 1) * HEAD_DIM, :]
                d = _dot(vt, pb[c * FOX_TQ:(c + 1) * FOX_TQ, cols])
                pv = d if pv is None else pv + d
            acc_ref[par] = alpha[:, cols] * acc_ref[par] + pv
        m_ref[...] = m_new

    nfull = i // ratio
    scores(0, 0)

    def body(jj, carry):
        j = 2 * jj
        scores(j + 1, 1)
        reduce(j, 0, False)
        scores(j + 2, 0)
        reduce(j + 1, 1, False)
        return carry

    lax.fori_loop(0, nfull // 2, body, 0)

    @pl.when(nfull % 2 == 0)
    def _():
        reduce(nfull, 0, True)

    @pl.when(nfull % 2 == 1)
    def _():
        scores(nfull, 1)
        reduce(nfull - 1, 0, False)
        reduce(nfull, 1, True)

    l = l_ref[...]
    ot = jnp.concatenate([acc_ref[0] / l[:, :FOX_TQ], acc_ref[1] / l[:, FOX_TQ:]], axis=0)
    o_ref[...] = ot.T


def _fox_prompt(qs, kb, vb, cq, ck, nb, seq):
    nq = seq // FOX_TQ
    qspec = pl.BlockSpec((FOX_TQ, LANES), lambda b, hp, i: (b * nq + i, hp))
    cqspec = pl.BlockSpec((FOX_TQ, LANES), lambda b, hp, i: (b * nq + i, 0))
    kspec = pl.BlockSpec((seq, LANES), lambda b, hp, i: (b, hp))
    ckspec = pl.BlockSpec((seq, LANES), lambda b, hp, i: (b, 0))
    return pl.pallas_call(
        _fox_prompt_kernel,
        grid=(nb, C_HEADS // 2, nq),
        in_specs=[qspec, cqspec, kspec, ckspec, kspec],
        out_specs=qspec,
        out_shape=jax.ShapeDtypeStruct((nb * seq, C_WIDTH), F32),
        scratch_shapes=[pltpu.VMEM((nq, LANES, FOX_TQ), BF16),
                        pltpu.VMEM((2, FOX_TK, 2 * FOX_TQ), F32),
                        pltpu.VMEM((1, 2 * FOX_TQ), F32), pltpu.VMEM((1, 2 * FOX_TQ), F32),
                        pltpu.VMEM((2, HEAD_DIM, FOX_TQ), F32)],
        compiler_params=_cp(("arbitrary", "arbitrary", "arbitrary")),
        name="fox_prompt",
    )(qs, cq, kb, ck, vb)


def _fox_sample_kernel(q_ref, cq_ref, kc_ref, kn_ref, ck_ref, vc_ref, vn_ref, o_ref, *, t, past):
    hp = pl.program_id(1)
    lane = _lane_iota((1, LANES))
    qp = q_ref[...]
    cq = cq_ref[past:past + t, :]
    pad = jnp.zeros((LANES - t, LANES), BF16)
    kc = jnp.concatenate([kc_ref[...].astype(BF16), ck_ref[:past, :]], axis=1)
    kn = jnp.concatenate([jnp.concatenate([kn_ref[...], pad], axis=0), ck_ref[past:, :]], axis=1)
    vc = vc_ref[...].astype(BF16)
    vn = jnp.concatenate([vn_ref[...], pad], axis=0)
    vis = _lane_iota((t, LANES)) <= _row_iota((t, LANES))
    outs = []
    for par in range(2):
        qm = jnp.where(_half_mask(par), qp, jnp.zeros_like(qp))
        qa = jnp.where(_aug_mask(2 * hp + par), cq, jnp.zeros_like(cq))
        qc = jnp.concatenate([qm, qa], axis=1)
        s0 = _dot_nt(qc, kc)
        s1 = jnp.where(vis, _dot_nt(qc, kn), NEG)
        outs.append(_softmax_pv([s0, s1], [vc, vn]))
    o_ref[...] = jnp.where(lane < HEAD_DIM, outs[0], outs[1])


def _fox_sample(qs, kb, vb, cq, ck, cache_k, cache_v, row0, nseq, t, past):
    blk0 = row0 // t
    new = pl.BlockSpec((t, LANES), lambda s, hp: (blk0 + s, hp))
    cache = pl.BlockSpec((past, LANES), lambda s, hp: (s, hp))
    aug = pl.BlockSpec((None, past + LANES, LANES), lambda s, hp: (s, 0, 0))
    return pl.pallas_call(
        functools.partial(_fox_sample_kernel, t=t, past=past),
        grid=(nseq, C_HEADS // 2),
        in_specs=[new, aug, cache, new, aug, cache, new],
        out_specs=pl.BlockSpec((t, LANES), lambda s, hp: (s, hp)),
        out_shape=jax.ShapeDtypeStruct((nseq * t, C_WIDTH), F32),
        compiler_params=_cp(("parallel", "parallel")),
        name="fox_sample",
    )(qs, cq, cache_k, kb, ck, cache_v, vb)


def _cross_kernel(q_ref, k_ref, v_ref, o_ref):
    lane = _lane_iota((1, LANES))
    for hp in range(M_HEADS // 2):
        cols = slice(hp * LANES, (hp + 1) * LANES)
        qp = q_ref[:, cols]
        k = k_ref[:, cols].astype(BF16)
        v = v_ref[:, cols].astype(BF16)
        outs = []
        for par in range(2):
            qm = jnp.where(_half_mask(par), qp, jnp.zeros_like(qp))
            outs.append(_softmax_pv([_dot_nt(qm, k)], [v]))
        o_ref[:, cols] = jnp.where(lane < HEAD_DIM, outs[0], outs[1])


def _cross(qs, mk, mv, row0, nrows, rows_per_seq, name):
    n_mem = mk.shape[0] // (nrows // rows_per_seq)
    tq = _pick(rows_per_seq, (512, 256, 128, 64))
    per = rows_per_seq // tq
    blk0 = row0 // tq
    return pl.pallas_call(
        _cross_kernel,
        grid=(nrows // tq,),
        in_specs=[pl.BlockSpec((tq, M_WIDTH), lambda i: (blk0 + i, 0)),
                  pl.BlockSpec((n_mem, M_WIDTH), lambda i: (i // per, 0)),
                  pl.BlockSpec((n_mem, M_WIDTH), lambda i: (i // per, 0))],
        out_specs=pl.BlockSpec((tq, M_WIDTH), lambda i: (i, 0)),
        out_shape=jax.ShapeDtypeStruct((nrows, M_WIDTH), F32),
        compiler_params=_cp(("parallel",)),
        name=name,
    )(qs, mk, mv)


SSD_Q = 128
CONV_PAD = SUBLANES


def _ssd_kernel(xbc_ref, z_ref, sm_ref, cs0_ref, hs0_ref, cw_ref, cb_ref, dtb_ref, alog_ref, dsk_ref,
                gn_ref, exp_ref, tri_ref, qm_ref,
                o_ref, hs_out_ref, cs_out_ref, hs_ref, xext_ref, *, nv, nchunk):
    c = pl.program_id(1)
    q = SSD_Q

    @pl.when(c == 0)
    def _():
        hs_ref[...] = hs0_ref[...].T * qm_ref[...]
        xext_ref[0:CONV_PAD, :] = cs0_ref[...]

    xext_ref[CONV_PAD:CONV_PAD + nv, :] = xbc_ref[...]
    if nv < q:
        xext_ref[CONV_PAD + nv:CONV_PAD + q, :] = jnp.zeros((q - nv, B_XBC), F32)
    y = cb_ref[...]
    for tap in range(B_CONV):
        y = y + xext_ref[pl.ds(CONV_PAD - (B_CONV - 1) + tap, q), :] * cw_ref[tap:tap + 1, :]
    tail = xext_ref[nv:nv + CONV_PAD, :]
    xext_ref[0:CONV_PAD, :] = tail
    cs_out_ref[...] = tail
    xa = _silu(y)
    xs = xa[:, :B_WIDTH]
    bm = xa[:, B_WIDTH:B_WIDTH + LANES].astype(BF16)
    cm = xa[:, B_WIDTH + LANES:]

    lane = _lane_iota((1, LANES))
    dt = _softplus(sm_ref[...] + dtb_ref[...])
    if nv < q:
        dt = jnp.concatenate([dt, jnp.zeros((q - nv, LANES), F32)], axis=0)
    head_lane = (lane >= DT_LANE) & (lane < DT_LANE + B_HEADS)
    a_row = jnp.where(head_lane, -jnp.exp(alog_ref[...]), 0.0)
    dt = jnp.where(head_lane, dt, 0.0)
    acum = _sel_dot(tri_ref[...], dt * a_row)
    expand = exp_ref[...]
    dt_e = _dot_sel(dt, expand)
    acum_e = _dot_sel(acum, expand)
    alast_e = acum_e[q - 1:q, :]
    xdt = xs * dt_e

    hs = hs_ref[...]
    y_off = _dot(cm.astype(BF16), hs.astype(BF16)) * jnp.exp(acum_e)
    upd = _dot_tn(bm, (xdt * jnp.exp(alast_e - acum_e)).astype(BF16))
    hs_new = (jnp.exp(alast_e) * hs + upd) * qm_ref[...]
    hs_ref[...] = hs_new

    @pl.when(c == nchunk - 1)
    def _():
        hs_out_ref[...] = hs_new.T

    acum_t = acum.T
    causal = _lane_iota((q, q)) <= _row_iota((q, q))
    xdt_b = xdt.astype(BF16)
    heads_per_group = B_HEADS // B_GROUPS
    y_pairs = []
    pair = []
    for h in range(B_HEADS):
        grp = h // heads_per_group
        if h % heads_per_group == 0:
            cg = jnp.where(_half_mask(grp), cm, 0.0).astype(BF16)
            cbm = _dot_nt(cg, bm)
        seg = acum[:, DT_LANE + h:DT_LANE + h + 1] - acum_t[DT_LANE + h:DT_LANE + h + 1, :]
        lmat = jnp.exp(jnp.where(causal, seg, -jnp.inf))
        hp = h // 2
        pair.append(_dot((cbm * lmat).astype(BF16), xdt_b[:, hp * LANES:(hp + 1) * LANES]))
        if h % 2 == 1:
            y_pairs.append(jnp.where(lane < HEAD_DIM, pair[0], pair[1]))
            pair = []
    y_diag = jnp.concatenate(y_pairs, axis=1)

    yt = y_off + y_diag + dsk_ref[...] * xs
    if nv < q:
        yt = yt[:nv]
    yz = yt * _silu(z_ref[...])
    ms = jnp.mean(yz * yz, axis=-1, keepdims=True)
    o_ref[...] = yz * lax.rsqrt(ms + EPS) * gn_ref[...]


def _ssd(u, cs0, hs0, consts, row0, nseq, rows_per_seq):
    nv = min(SSD_Q, rows_per_seq)
    nchunk = rows_per_seq // nv
    blk0 = row0 // nv
    cw, cb, dtb, alog, dsk, gn, expand, tri, qmask = consts

    def ucol(name, width):
        idx = _U_OFF[name] // width
        assert idx * width == _U_OFF[name]
        return pl.BlockSpec((nv, width), lambda s, c: (blk0 + s * nchunk + c, idx))

    full = lambda a: pl.BlockSpec(a.shape, lambda s, c: (0,) * a.ndim)
    cs_spec = pl.BlockSpec((None, CONV_PAD, B_XBC), lambda s, c: (s, 0, 0))
    hs_spec = pl.BlockSpec((None, B_WIDTH, LANES), lambda s, c: (s, 0, 0))
    return pl.pallas_call(
        functools.partial(_ssd_kernel, nv=nv, nchunk=nchunk),
        grid=(nseq, nchunk),
        in_specs=[ucol('b_xbc', B_XBC), ucol('b_z', B_WIDTH), ucol('b_dt', LANES), cs_spec, hs_spec,
                  full(cw), full(cb), full(dtb), full(alog), full(dsk), full(gn), full(expand), full(tri),
                  full(qmask)],
        out_specs=[pl.BlockSpec((nv, B_WIDTH), lambda s, c: (s * nchunk + c, 0)), hs_spec, cs_spec],
        out_shape=[jax.ShapeDtypeStruct((nseq * rows_per_seq, B_WIDTH), F32),
                   jax.ShapeDtypeStruct((nseq, B_WIDTH, LANES), F32),
                   jax.ShapeDtypeStruct((nseq, CONV_PAD, B_XBC), F32)],
        scratch_shapes=[pltpu.VMEM((LANES, B_WIDTH), F32), pltpu.VMEM((SSD_Q + CONV_PAD, B_XBC), F32)],
        compiler_params=_cp(("parallel", "arbitrary")),
        name="ssd",
    )(u, u, u, cs0, hs0, cw, cb, dtb, alog, dsk, gn, expand, tri, qmask)


def _ssd_static_consts():
    expand = np.zeros((LANES, B_WIDTH), np.float32)
    for h in range(B_HEADS):
        expand[DT_LANE + h, h * HEAD_DIM:(h + 1) * HEAD_DIM] = 1.0
    tri = np.tril(np.ones((SSD_Q, SSD_Q), np.float32))
    qmask = np.zeros((LANES, B_WIDTH), np.float32)
    half = B_WIDTH // B_GROUPS
    for g in range(B_GROUPS):
        qmask[g * B_STATE:(g + 1) * B_STATE, g * half:(g + 1) * half] = 1.0
    return jnp.asarray(expand, BF16), jnp.asarray(tri, BF16), jnp.asarray(qmask)


def _stack_state(h):
    r = h.reshape(h.shape[0], B_WIDTH, B_STATE)
    return jnp.concatenate([r, r], axis=2)


def _unstack_state(hst):
    n = hst.shape[0]
    first = (np.arange(B_WIDTH) < B_WIDTH // B_GROUPS)[None, :, None]
    return jnp.where(first, hst[:, :, :B_STATE], hst[:, :, B_STATE:]).reshape(n, B_HEADS, HEAD_DIM, B_STATE)


def _merge_kernel(x, oa, az, ob, oc, cz, om, mz, gt, wpa, wpb, wpc, wpm, wout, o):
    d = x.shape[1]
    pa = _dot((oa[...] * _silu(az[...])).astype(BF16), wpa[...])
    pb = _dot(ob[...].astype(BF16), wpb[...])
    pc = _dot((oc[...] * _silu(cz[...])).astype(BF16), wpc[...])
    pm = _dot((om[...] * _silu(mz[...])).astype(BF16), wpm[...])
    mix = (gt[:, 0:d].astype(F32) * pa + gt[:, d:2 * d].astype(F32) * pb
           + gt[:, 2 * d:3 * d].astype(F32) * pc + gt[:, 3 * d:4 * d].astype(F32) * pm)
    o[...] = x[...] + _dot(mix.astype(BF16), wout[...])


def _merge(x, u, gates, oa, ob, oc, om, wpa, wpb, wpc, wpm, wout):
    m, d = x.shape
    tm = _pick(m, (512, 256, 128))

    def ucol(name, width):
        idx = _U_OFF[name] // width
        assert idx * width == _U_OFF[name]
        return pl.BlockSpec((tm, width), lambda i: (i, idx))

    def rows(width):
        return pl.BlockSpec((tm, width), lambda i: (i, 0))

    full = lambda a: pl.BlockSpec(a.shape, lambda i: (0, 0))
    return pl.pallas_call(
        _merge_kernel,
        grid=(m // tm,),
        in_specs=[rows(d), rows(A_WIDTH), ucol('a_z', A_WIDTH), rows(B_WIDTH), rows(C_WIDTH),
                  ucol('c_z', C_WIDTH), rows(M_WIDTH), ucol('m_z', M_WIDTH), rows(N_BRANCH * d),
                  full(wpa), full(wpb), full(wpc), full(wpm), full(wout)],
        out_specs=rows(d),
        out_shape=jax.ShapeDtypeStruct((m, d), F32),
        compiler_params=_cp(("parallel",)),
        name="merge",
    )(x, oa, u, ob, oc, u, om, u, gates, wpa, wpb, wpc, wpm, wout)


def _row(v, width=None):
    v = v.reshape(1, -1).astype(F32)
    if width is not None and v.shape[1] < width:
        v = jnp.pad(v, ((0, 0), (0, width - v.shape[1])))
    return v


def _lane_row(v, lane0):
    return jnp.zeros((1, LANES), F32).at[0, lane0:lane0 + v.shape[0]].set(v.astype(F32))


def _block_diag_mean(width):
    seg = np.arange(width) // HEAD_DIM
    return jnp.asarray((seg[:, None] == seg[None, :]).astype(np.float32) / HEAD_DIM, BF16)


def kernel(x_prompt, x_sample, mem_prompt, cache_a_k, cache_a_v, cache_c_k, cache_c_v, cache_c_logf, state_b_ssm, state_b_conv, cache_mem_k, cache_mem_v, g_norm, w_in, a_qnorm, a_knorm, a_rel, b_conv_w, b_conv_b, b_dt_bias, b_a_log, b_d, b_norm, c_qnorm, c_knorm, c_fbias, m_norm, w_mkv, m_qnorm, m_knorm, w_pa, w_pb, w_pc, w_pm, w_out):
    nb, seq, d = x_prompt.shape
    ns, tdec, _ = x_sample.shape
    depth = g_norm.shape[0]
    n_mem = mem_prompt.shape[1]
    past = cache_c_k.shape[2]
    la_cache = cache_a_k.shape[2]
    mp = nb * seq
    md = ns * tdec
    la = min(A_WIN, seq)
    assert seq % FOX_TK == 0 and seq % BAND_Q == 0 and seq % SSD_Q == 0
    assert tdec == CHUNK and la_cache == A_WIN and past % LANES == 0

    xp = x_prompt.reshape(mp, d)
    xs = x_sample.reshape(md, d)
    mem = mem_prompt.reshape(nb * n_mem, d)

    bd512 = _block_diag_mean(512)
    bd256 = _block_diag_mean(M_WIDTH)
    cum_consts = _cum_consts()
    expand, tri_q, qmask = _ssd_static_consts()
    scale = HEAD_DIM ** -0.5 * LOG2E

    outs = {k: [] for k in ('pa_k', 'pa_v', 'pc_k', 'pc_v', 'pc_f', 'pb_s', 'pb_c', 'pm_k', 'pm_v',
                            'sa_k', 'sa_v', 'sc_k', 'sc_v', 'sc_f', 'sb_s', 'sb_c')}

    for l in range(depth):
        wt = jnp.transpose(w_in[l])
        pieces = [wt[_SRC[n][0]:_SRC[n][0] + _SRC[n][1]] for n in _U_ORDER]
        used = sum(p.shape[0] for p in pieces)
        w_u = jnp.concatenate(pieces + [jnp.zeros((U_WIDTH - used, d), F32)], axis=0).astype(BF16)
        g0, gw = _SRC['gate']
        w_g = wt[g0:g0 + gw].astype(BF16)
        g_row = _row(g_norm[l])
        gains = (_row(jnp.tile(a_qnorm[l], A_HEADS)) * scale, _row(jnp.tile(a_knorm[l], A_HEADS)),
                 _row(jnp.tile(c_qnorm[l], C_HEADS)) * scale, _row(jnp.tile(c_knorm[l], C_HEADS)),
                 _row(jnp.tile(m_qnorm[l], M_HEADS)) * scale, _lane_row(c_fbias[l], CF_LANE), bd512)

        def tokens(x):
            u = _proj(x, g_row, w_u, act=None, out_dtype=F32, name="proj_u", w_rows=True)
            gates = _proj(x, g_row, w_g, act='sigmoid', out_dtype=BF16, name="proj_gate", w_rows=True)
            return (u, gates) + tuple(_prep(u, *gains))

        (u_p, gates_p, a_qs_p, a_kf_p, a_kb_p, a_vb_p, c_qs_p, c_kf_p, c_kb_p, c_vb_p, m_qs_p, lf_p) = tokens(xp)
        (u_s, gates_s, a_qs_s, a_kf_s, a_kb_s, a_vb_s, c_qs_s, c_kf_s, c_kb_s, c_vb_s, m_qs_s, lf_s) = tokens(xs)

        def ucols(u, name):
            return u[:, _U_OFF[name]:_U_OFF[name] + _SRC[name][1]]

        kv = _proj(mem, _row(m_norm[l]), w_mkv[l].astype(BF16), act=None, out_dtype=F32, name="proj_mem",
                   w_rows=False)
        mk = _memk(kv[:, :M_WIDTH], _row(jnp.tile(m_knorm[l], M_HEADS)), bd256)
        mv = kv[:, M_WIDTH:]

        bias = _band_bias(a_rel[l])
        o_a_p = _band_prompt(a_qs_p, a_kb_p, a_vb_p, bias, nb, seq)
        bias_s = jnp.concatenate(
            [bias[:, :tdec, :la_cache + tdec],
             jnp.full((A_HEADS, tdec, LANES - tdec), NEG, F32)], axis=2)
        o_a_s = _band_sample(a_qs_s, a_kb_s, a_vb_s, cache_a_k[l].reshape(ns * la_cache, A_WIDTH),
                             cache_a_v[l].reshape(ns * la_cache, A_WIDTH), bias_s, 0, ns, tdec)

        cq_p, ck_p = _cum(lf_p.reshape(nb, seq, LANES), cum_consts)
        o_c_p = _fox_prompt(c_qs_p, c_kb_p, c_vb_p, cq_p.reshape(mp, LANES), ck_p.reshape(mp, LANES), nb, seq)
        lf_cache = jnp.pad(cache_c_logf[l].astype(F32), ((0, 0), (0, 0), (CF_LANE, LANES - CF_LANE - C_HEADS)))
        lf_cat = jnp.concatenate([lf_cache, lf_s.reshape(ns, tdec, LANES),
                                  jnp.zeros((ns, LANES - tdec, LANES), F32)], axis=1)
        cq_s, ck_s = _cum(lf_cat, cum_consts)
        o_c_s = _fox_sample(c_qs_s, c_kb_s, c_vb_s, cq_s, ck_s, cache_c_k[l].reshape(ns * past, C_WIDTH),
                            cache_c_v[l].reshape(ns * past, C_WIDTH), 0, ns, tdec, past)

        ssd_consts = (jnp.pad(b_conv_w[l], ((0, SUBLANES - B_CONV), (0, 0))), _row(b_conv_b[l]),
                      _lane_row(b_dt_bias[l], DT_LANE), _lane_row(b_a_log[l], DT_LANE),
                      _row(jnp.repeat(b_d[l], HEAD_DIM)), _row(b_norm[l]), expand, tri_q, qmask)
        o_b_p, hs_p, cs_p = _ssd(u_p, jnp.zeros((nb, CONV_PAD, B_XBC), F32),
                                 jnp.zeros((nb, B_WIDTH, LANES), F32), ssd_consts, 0, nb, seq)
        cs0 = jnp.pad(state_b_conv[l].astype(F32), ((0, 0), (CONV_PAD - (B_CONV - 1), 0), (0, 0)))
        o_b_s, hs_s, cs_s = _ssd(u_s, cs0, _stack_state(state_b_ssm[l].astype(F32)), ssd_consts, 0, ns, tdec)

        o_m_p = _cross(m_qs_p, mk, mv, 0, mp, seq, "cross_prompt")
        o_m_s = _cross(m_qs_s, cache_mem_k[l].reshape(ns * n_mem, M_WIDTH),
                       cache_mem_v[l].reshape(ns * n_mem, M_WIDTH), 0, md, tdec, "cross_sample")

        w_merge = (w_pa[l].astype(BF16), w_pb[l].astype(BF16), w_pc[l].astype(BF16), w_pm[l].astype(BF16),
                   w_out[l].astype(BF16))
        xp = _merge(xp, u_p, gates_p, o_a_p, o_b_p, o_c_p, o_m_p, *w_merge)
        xs = _merge(xs, u_s, gates_s, o_a_s, o_b_s, o_c_s, o_m_s, *w_merge)

        outs['pa_k'].append(a_kf_p.reshape(nb, seq, A_HEADS, HEAD_DIM)[:, -la:])
        outs['pa_v'].append(ucols(u_p, 'a_v').reshape(nb, seq, A_HEADS, HEAD_DIM)[:, -la:])
        outs['pc_k'].append(c_kf_p.reshape(nb, seq, C_HEADS, HEAD_DIM))
        outs['pc_v'].append(ucols(u_p, 'c_v').reshape(nb, seq, C_HEADS, HEAD_DIM))
        outs['pc_f'].append(lf_p[:, CF_LANE:CF_LANE + C_HEADS].reshape(nb, seq, C_HEADS))
        outs['pb_s'].append(_unstack_state(hs_p))
        outs['pb_c'].append(cs_p[:, CONV_PAD - (B_CONV - 1):])
        outs['pm_k'].append(mk.reshape(nb, n_mem, M_HEADS, HEAD_DIM))
        outs['pm_v'].append(mv.reshape(nb, n_mem, M_HEADS, HEAD_DIM))
        outs['sa_k'].append(a_kf_s.reshape(ns, tdec, A_HEADS, HEAD_DIM))
        outs['sa_v'].append(ucols(u_s, 'a_v').reshape(ns, tdec, A_HEADS, HEAD_DIM))
        outs['sc_k'].append(c_kf_s.reshape(ns, tdec, C_HEADS, HEAD_DIM))
        outs['sc_v'].append(ucols(u_s, 'c_v').reshape(ns, tdec, C_HEADS, HEAD_DIM))
        outs['sc_f'].append(lf_s[:, CF_LANE:CF_LANE + C_HEADS].reshape(ns, tdec, C_HEADS))
        outs['sb_s'].append(_unstack_state(hs_s))
        outs['sb_c'].append(cs_s[:, CONV_PAD - (B_CONV - 1):])

    st = jnp.stack
    return (xp.reshape(nb, seq, d), xs.reshape(ns, tdec, d),
            st(outs['pa_k']), st(outs['pa_v']), st(outs['pc_k']), st(outs['pc_v']), st(outs['pc_f']),
            st(outs['pb_s']), st(outs['pb_c']), st(outs['pm_k']), st(outs['pm_v']),
            st(outs['sa_k']), st(outs['sa_v']), st(outs['sc_k']), st(outs['sc_v']), st(outs['sc_f']),
            st(outs['sb_s']), st(outs['sb_c']))
```

```python
import functools

import numpy as np
import jax
import jax.numpy as jnp
from jax import lax
from jax.experimental import pallas as pl
from jax.experimental.pallas import tpu as pltpu

F32 = jnp.float32
BF16 = jnp.bfloat16

EPS = 1e-6
NEG = -1e30
LOG2E = 1.4426950408889634

HEAD_DIM = 64
CHUNK = 64
A_HEADS = 8
A_WIDTH = 512
A_LEFT_CHUNKS = 8
A_WIN = A_LEFT_CHUNKS * CHUNK
A_REL_CLIP = 128
B_HEADS = 8
B_WIDTH = 512
B_GROUPS = 2
B_STATE = 64
B_CONV = 4
B_XBC = B_WIDTH + 2 * B_GROUPS * B_STATE
C_HEADS = 8
C_WIDTH = 512
M_HEADS = 4
M_WIDTH = 256
N_BRANCH = 4

LANES = 128
SUBLANES = 8
VMEM_LIMIT = 48 * 1024 * 1024

_SPLITS = (('a_q', A_WIDTH), ('a_k', A_WIDTH), ('a_v', A_WIDTH), ('a_z', A_WIDTH),
           ('b_z', B_WIDTH), ('b_xbc', B_XBC), ('b_dt', B_HEADS),
           ('c_q', C_WIDTH), ('c_k', C_WIDTH), ('c_v', C_WIDTH), ('c_f', C_HEADS), ('c_z', C_WIDTH),
           ('m_q', M_WIDTH), ('m_z', M_WIDTH), ('gate', N_BRANCH * 1024))
_SRC = {}
_off = 0
for _n, _w in _SPLITS:
    _SRC[_n] = (_off, _w)
    _off += _w

_U_ORDER = ('a_q', 'a_k', 'a_v', 'a_z', 'c_q', 'c_k', 'c_v', 'c_z', 'b_z', 'b_xbc', 'm_q', 'm_z', 'b_dt', 'c_f')
_U_OFF = {}
_off = 0
for _n in _U_ORDER:
    _U_OFF[_n] = _off
    _off += _SRC[_n][1]
U_WIDTH = 6144
DT_LANE = 0
CF_LANE = B_HEADS
AUG_PER_HEAD = 6


def _cp(sem):
    return pltpu.CompilerParams(dimension_semantics=sem, vmem_limit_bytes=VMEM_LIMIT)


def _pick(n, cands):
    for c in cands:
        if n % c == 0:
            return c
    raise ValueError(f"no tile for {n} in {cands}")


def _dot(a, b):
    return jnp.dot(a, b, preferred_element_type=F32)


def _dot_nt(a, b):
    return lax.dot_general(a, b, (((1,), (1,)), ((), ())), preferred_element_type=F32)


def _dot_tn(a, b):
    return lax.dot_general(a, b, (((0,), (0,)), ((), ())), preferred_element_type=F32)


def _split3(x):
    hi = x.astype(BF16)
    r = x - hi.astype(F32)
    mid = r.astype(BF16)
    lo = (r - mid.astype(F32)).astype(BF16)
    return hi, mid, lo


def _dot_sel(x, sel):
    hi, mid, lo = _split3(x)
    return _dot(hi, sel) + _dot(mid, sel) + _dot(lo, sel)


def _sel_dot(sel, x):
    hi, mid, lo = _split3(x)
    return _dot(sel, hi) + _dot(sel, mid) + _dot(sel, lo)


def _sigmoid(x):
    return 1.0 / (1.0 + jnp.exp(-x))


def _silu(x):
    return x * _sigmoid(x)


def _softplus(x):
    return jnp.maximum(x, 0.0) + jnp.log1p(jnp.exp(-jnp.abs(x)))


def _log_sigmoid(x):
    return jnp.minimum(x, 0.0) - jnp.log1p(jnp.exp(-jnp.abs(x)))


def _head_norm(x, bd):
    x2 = x * x
    hi = x2.astype(BF16)
    lo = (x2 - hi.astype(F32)).astype(BF16)
    ms = _dot(hi, bd) + _dot(lo, bd)
    return x * lax.rsqrt(ms + EPS)


def _lane_iota(shape):
    return lax.broadcasted_iota(jnp.int32, shape, len(shape) - 1)


def _row_iota(shape):
    return lax.broadcasted_iota(jnp.int32, shape, len(shape) - 2)


def _proj_kernel(x_ref, g_ref, w_ref, o_ref, h_ref, *, act, w_rows):
    @pl.when(pl.program_id(1) == 0)
    def _():
        x = x_ref[...]
        ms = jnp.mean(x * x, axis=-1, keepdims=True)
        h_ref[...] = (x * lax.rsqrt(ms + EPS) * g_ref[...]).astype(BF16)

    u = _dot_nt(h_ref[...], w_ref[...]) if w_rows else _dot(h_ref[...], w_ref[...])
    if act == 'sigmoid':
        u = _sigmoid(u)
    o_ref[...] = u.astype(o_ref.dtype)


def _proj(x, g_row, w_bf, *, act, out_dtype, name, w_rows):
    m, d = x.shape
    n = w_bf.shape[0] if w_rows else w_bf.shape[1]
    tm = _pick(m, (1024, 512, 256, 128))
    tn = _pick(n, (1024, 512, 256, 128))
    w_spec = pl.BlockSpec((tn, d), lambda i, j: (j, 0)) if w_rows else pl.BlockSpec((d, tn), lambda i, j: (0, j))
    return pl.pallas_call(
        functools.partial(_proj_kernel, act=act, w_rows=w_rows),
        grid=(m // tm, n // tn),
        in_specs=[pl.BlockSpec((tm, d), lambda i, j: (i, 0)),
                  pl.BlockSpec((1, d), lambda i, j: (0, 0)),
                  w_spec],
        out_specs=pl.BlockSpec((tm, tn), lambda i, j: (i, j)),
        out_shape=jax.ShapeDtypeStruct((m, n), out_dtype),
        scratch_shapes=[pltpu.VMEM((tm, d), BF16)],
        compiler_params=_cp(("parallel", "arbitrary")),
        name=name,
    )(x, g_row, w_bf)


PREP_TS = 512
VT_CHUNK = 256


def _prep_common(aq, ak, av, cq, ck, cv, mq, sm, gaq, gak, gcq, gck, gmq, fb, bd):
    bdv = bd[...]
    return dict(
        a_qs=(_head_norm(aq[...], bdv) * gaq[...]).astype(BF16),
        a_kn=_head_norm(ak[...], bdv) * gak[...],
        a_v=av[...],
        c_qs=(_head_norm(cq[...], bdv) * gcq[...]).astype(BF16),
        c_kn=_head_norm(ck[...], bdv) * gck[...],
        c_v=cv[...],
        m_qs=(_head_norm(mq[...], bd[:M_WIDTH, :M_WIDTH]) * gmq[...]).astype(BF16),
        lf=_log_sigmoid(sm[...] + fb[...]))


def _prep_prompt_kernel(aq, ak, av, cq, ck, cv, mq, sm, gaq, gak, gcq, gck, gmq, fb, bd,
                        o_aq, o_akb, o_avb, o_akt, o_avt, o_cq, o_ckb, o_ckt, o_cvt, o_cvtb, o_mq, o_lf, o_lft,
                        *, steps_per_seq):
    r = _prep_common(aq, ak, av, cq, ck, cv, mq, sm, gaq, gak, gcq, gck, gmq, fb, bd)
    o_aq[...] = r['a_qs']
    o_akb[...] = r['a_kn'].astype(BF16)
    o_avb[...] = r['a_v'].astype(BF16)

    @pl.when(pl.program_id(0) % steps_per_seq == steps_per_seq - 1)
    def _():
        o_akt[...] = r['a_kn'].T
        o_avt[...] = r['a_v'].T

    o_cq[...] = r['c_qs']
    o_ckb[...] = r['c_kn'].astype(BF16)
    o_ckt[...] = r['c_kn'].T
    cvt = r['c_v'].T
    o_cvt[...] = cvt
    cvtb = cvt.astype(BF16)
    for c in range(PREP_TS // VT_CHUNK):
        o_cvtb[c] = cvtb[:, c * VT_CHUNK:(c + 1) * VT_CHUNK]
    o_mq[...] = r['m_qs']
    o_lf[...] = r['lf']
    o_lft[...] = r['lf'].T[CF_LANE:CF_LANE + C_HEADS, :]


def _prep_sample_kernel(aq, ak, av, cq, ck, cv, mq, sm, gaq, gak, gcq, gck, gmq, fb, bd,
                        o_aq, o_akb, o_avb, o_ak4, o_av4, o_cq, o_ckb, o_cvb, o_ck4, o_cv4, o_mq, o_lf):
    r = _prep_common(aq, ak, av, cq, ck, cv, mq, sm, gaq, gak, gcq, gck, gmq, fb, bd)
    ts = o_aq.shape[0]
    o_aq[...] = r['a_qs']
    o_akb[...] = r['a_kn'].astype(BF16)
    o_avb[...] = r['a_v'].astype(BF16)
    o_ak4[...] = r['a_kn'].reshape(ts, A_HEADS, HEAD_DIM)
    o_av4[...] = r['a_v'].reshape(ts, A_HEADS, HEAD_DIM)
    o_cq[...] = r['c_qs']
    o_ckb[...] = r['c_kn'].astype(BF16)
    o_cvb[...] = r['c_v'].astype(BF16)
    o_ck4[...] = r['c_kn'].reshape(ts, C_HEADS, HEAD_DIM)
    o_cv4[...] = r['c_v'].reshape(ts, C_HEADS, HEAD_DIM)
    o_mq[...] = r['m_qs']
    o_lf[...] = r['lf']


def _prep_in_specs(ts):
    def ucol(name, width):
        idx = _U_OFF[name] // width
        assert idx * width == _U_OFF[name]
        return pl.BlockSpec((ts, width), lambda i: (i, idx))

    def row(width):
        return pl.BlockSpec((1, width), lambda i: (0, 0))

    return [ucol('a_q', 512), ucol('a_k', 512), ucol('a_v', 512),
            ucol('c_q', 512), ucol('c_k', 512), ucol('c_v', 512),
            ucol('m_q', 256), ucol('b_dt', LANES),
            row(512), row(512), row(512), row(512), row(256), row(LANES),
            pl.BlockSpec((512, 512), lambda i: (0, 0))]


def _prep_prompt(u, gains, nb, seq):
    ts = PREP_TS
    assert seq % ts == 0 and min(A_WIN, seq) == ts
    sps = seq // ts
    m = nb * seq
    sds = jax.ShapeDtypeStruct
    rows = lambda width: pl.BlockSpec((ts, width), lambda i: (i, 0))
    last = pl.BlockSpec((None, 512, ts), lambda i: (i // sps, 0, 0))
    feat = pl.BlockSpec((None, 512, ts), lambda i: (i // sps, 0, i % sps))
    nvc = ts // VT_CHUNK
    out_specs = [rows(512), rows(512), rows(512), last, last,
                 rows(512), rows(512), feat, feat,
                 pl.BlockSpec((None, nvc, 512, VT_CHUNK), lambda i: (i // sps, i % sps, 0, 0)),
                 rows(256), rows(LANES),
                 pl.BlockSpec((None, C_HEADS, ts), lambda i: (i // sps, 0, i % sps))]
    out_shape = [sds((m, 512), BF16), sds((m, 512), BF16), sds((m, 512), BF16),
                 sds((nb, 512, ts), F32), sds((nb, 512, ts), F32),
                 sds((m, 512), BF16), sds((m, 512), BF16), sds((nb, 512, seq), F32), sds((nb, 512, seq), F32),
                 sds((nb, seq // VT_CHUNK, 512, VT_CHUNK), BF16),
                 sds((m, 256), BF16), sds((m, LANES), F32), sds((nb, C_HEADS, seq), F32)]
    return pl.pallas_call(
        functools.partial(_prep_prompt_kernel, steps_per_seq=sps),
        grid=(m // ts,),
        in_specs=_prep_in_specs(ts),
        out_specs=out_specs,
        out_shape=out_shape,
        compiler_params=_cp(("arbitrary",)),
        name="prep_prompt",
    )(*([u] * 8), *gains)


def _prep_sample(u, gains):
    m = u.shape[0]
    ts = _pick(m, (512, 256, 128, 64))
    sds = jax.ShapeDtypeStruct
    rows = lambda width: pl.BlockSpec((ts, width), lambda i: (i, 0))
    heads = pl.BlockSpec((ts, A_HEADS, HEAD_DIM), lambda i: (i, 0, 0))
    return pl.pallas_call(
        _prep_sample_kernel,
        grid=(m // ts,),
        in_specs=_prep_in_specs(ts),
        out_specs=[rows(512), rows(512), rows(512), heads, heads,
                   rows(512), rows(512), rows(512), heads, heads, rows(256), rows(LANES)],
        out_shape=[sds((m, 512), BF16), sds((m, 512), BF16), sds((m, 512), BF16),
                   sds((m, A_HEADS, HEAD_DIM), F32), sds((m, A_HEADS, HEAD_DIM), F32),
                   sds((m, 512), BF16), sds((m, 512), BF16), sds((m, 512), BF16),
                   sds((m, C_HEADS, HEAD_DIM), F32), sds((m, C_HEADS, HEAD_DIM), F32),
                   sds((m, 256), BF16), sds((m, LANES), F32)],
        compiler_params=_cp(("parallel",)),
        name="prep_sample",
    )(*([u] * 8), *gains)


def _memkv_kernel(kv, g, bd, o_kt, o_vt):
    o_kt[...] = (_head_norm(kv[:, :M_WIDTH], bd[...]) * g[...]).T
    o_vt[...] = kv[:, M_WIDTH:].T


def _memkv(kv, g_row, bd256, nb, n_mem):
    spec = pl.BlockSpec((None, M_WIDTH, n_mem), lambda b: (b, 0, 0))
    return pl.pallas_call(
        _memkv_kernel,
        grid=(nb,),
        in_specs=[pl.BlockSpec((n_mem, 2 * M_WIDTH), lambda b: (b, 0)),
                  pl.BlockSpec((1, M_WIDTH), lambda b: (0, 0)),
                  pl.BlockSpec((M_WIDTH, M_WIDTH), lambda b: (0, 0))],
        out_specs=[spec, spec],
        out_shape=[jax.ShapeDtypeStruct((nb, M_WIDTH, n_mem), F32)] * 2,
        compiler_params=_cp(("parallel",)),
        name="memkv",
    )(kv, g_row, bd256)


def _cum_kernel(lf_ref, tri_ref, selq_ref, selk_ref, qc_ref, kc_ref, blk_ref, low_ref, cq_ref, ck_ref, carry_ref,
                *, nchunk, unroll):
    totals = _sel_dot(blk_ref[...], lf_ref[...])
    carry_ref[...] = _sel_dot(low_ref[...], totals)
    tri = tri_ref[...]

    def body(cc, carry):
        cs = [cc * unroll + k for k in range(unroll)]
        rs = [pl.multiple_of(c * LANES, LANES) for c in cs]
        xs = [lf_ref[pl.ds(r, LANES), :] for r in rs]
        zs = [(_sel_dot(tri, x) + carry_ref[pl.ds(c, 1), :]) * LOG2E for x, c in zip(xs, cs)]
        parts = [_split3(z) for z in zs]
        qs = [_dot(p[0], selq_ref[0]) + _dot(p[1], selq_ref[1]) + _dot(p[2], selq_ref[2]) + qc_ref[...]
              for p in parts]
        ks = [_dot(p[0], selk_ref[0]) + _dot(p[1], selk_ref[1]) + _dot(p[2], selk_ref[2]) + kc_ref[...]
              for p in parts]
        for r, qv, kv in zip(rs, qs, ks):
            cq_ref[pl.ds(r, LANES), :] = qv.astype(BF16)
            ck_ref[pl.ds(r, LANES), :] = kv.astype(BF16)
        return carry

    lax.fori_loop(0, nchunk // unroll, body, 0)


CUM_UNROLLS = (4, 3, 2, 1)


def _cum(lf, consts):
    nseq, length, _ = lf.shape
    assert length % LANES == 0
    nchunk = length // LANES
    unroll = _pick(nchunk, CUM_UNROLLS)
    ncp = -(-nchunk // SUBLANES) * SUBLANES
    tri, selq, selk, qc, kc = consts
    chunk_of = np.arange(length) // LANES
    blk = jnp.asarray((np.arange(ncp)[:, None] == chunk_of[None, :]).astype(np.float32), BF16)
    low = jnp.asarray(np.tril(np.ones((ncp, ncp), np.float32), -1), BF16)
    full = lambda shape: pl.BlockSpec(shape, lambda s: (0,) * len(shape))
    seq = pl.BlockSpec((None, length, LANES), lambda s: (s, 0, 0))
    return pl.pallas_call(
        functools.partial(_cum_kernel, nchunk=nchunk, unroll=unroll),
        grid=(nseq,),
        in_specs=[seq, full((LANES, LANES)), full((3, LANES, LANES)), full((3, LANES, LANES)),
                  full((1, LANES)), full((1, LANES)), full((ncp, length)), full((ncp, ncp))],
        out_specs=[seq, seq],
        out_shape=[jax.ShapeDtypeStruct(lf.shape, BF16)] * 2,
        scratch_shapes=[pltpu.VMEM((ncp, LANES), F32)],
        compiler_params=_cp(("parallel",)),
        name="cum",
    )(lf, tri, selq, selk, qc, kc, blk, low)


def _cum_consts():
    tri = np.tril(np.ones((LANES, LANES), np.float32))
    selq = np.zeros((3, LANES, LANES), np.float32)
    selk = np.zeros((3, LANES, LANES), np.float32)
    qc = np.zeros((1, LANES), np.float32)
    kc = np.zeros((1, LANES), np.float32)
    for h in range(C_HEADS):
        for p in range(3):
            selq[p, CF_LANE + h, AUG_PER_HEAD * h + p] = 1.0
            selk[p, CF_LANE + h, AUG_PER_HEAD * h + 3 + p] = -1.0
            qc[0, AUG_PER_HEAD * h + 3 + p] = 1.0
            kc[0, AUG_PER_HEAD * h + p] = 1.0
    return (jnp.asarray(tri, BF16), jnp.asarray(selq, BF16), jnp.asarray(selk, BF16),
            jnp.asarray(qc), jnp.asarray(kc))


BAND_Q = 4 * CHUNK
BAND_K = 3 * BAND_Q
TAB_PAD = 384


def _bias_kernel(tab_ref, idx_ref, neg_ref, o_ref):
    tn = idx_ref.shape[1]
    onehot = (lax.broadcasted_iota(jnp.int32, (TAB_PAD, tn), 0) == idx_ref[...]).astype(BF16)
    o_ref[...] = _dot_sel(tab_ref[...], onehot) * LOG2E + neg_ref[...]


def _band_bias(table):
    r = np.arange(BAND_Q)[:, None]
    t = np.arange(BAND_K)[None, :]
    idx = (np.clip(A_WIN + r - t, -A_REL_CLIP, A_REL_CLIP) + A_REL_CLIP).astype(np.int32).reshape(1, -1)
    cb = t // CHUNK - r // CHUNK
    neg = np.where((cb >= 0) & (cb <= A_LEFT_CHUNKS), 0.0, NEG).astype(np.float32).reshape(1, -1)
    n = idx.shape[1]
    tn = 8192
    tab = jnp.zeros((A_HEADS, TAB_PAD), F32).at[:, :table.shape[0]].set(table.T)
    out = pl.pallas_call(
        _bias_kernel,
        grid=(n // tn,),
        in_specs=[pl.BlockSpec((A_HEADS, TAB_PAD), lambda j: (0, 0)),
                  pl.BlockSpec((1, tn), lambda j: (0, j)),
                  pl.BlockSpec((1, tn), lambda j: (0, j))],
        out_specs=pl.BlockSpec((A_HEADS, tn), lambda j: (0, j)),
        out_shape=jax.ShapeDtypeStruct((A_HEADS, n), F32),
        compiler_params=_cp(("parallel",)),
        name="band_bias",
    )(tab, jnp.asarray(idx), jnp.asarray(neg))
    return out.reshape(A_HEADS, BAND_Q, BAND_K)


def _half_mask(par):
    lane = _lane_iota((1, LANES))
    return (lane < HEAD_DIM) if par == 0 else (lane >= HEAD_DIM)


def _softmax_pv_many(s_lists, v_lists, v_feature_major=None):
    n = len(s_lists)
    nblk = len(s_lists[0])
    if v_feature_major is None:
        v_feature_major = (False,) * nblk
    ms = []
    for sb in s_lists:
        m = sb[0].max(axis=1, keepdims=True)
        for s in sb[1:]:
            m = jnp.maximum(m, s.max(axis=1, keepdims=True))
        ms.append(m)
    ps = [[jnp.exp2(s - ms[i]) for s in s_lists[i]] for i in range(n)]
    ls = []
    for i in range(n):
        l = ps[i][0].sum(axis=1, keepdims=True)
        for p in ps[i][1:]:
            l = l + p.sum(axis=1, keepdims=True)
        ls.append(l)
    outs = []
    for i in range(n):
        acc = None
        for p, v, fm in zip(ps[i], v_lists[i], v_feature_major):
            pv = _dot_nt(p.astype(BF16), v) if fm else _dot(p.astype(BF16), v)
            acc = pv if acc is None else acc + pv
        outs.append(acc)
    return [o / l for o, l in zip(outs, ls)]


def _softmax_pv(s_blocks, v_blocks, v_feature_major=None):
    return _softmax_pv_many([s_blocks], [v_blocks], v_feature_major)[0]


def _band_prompt_kernel(q_ref, k0, k1, k2, v0, v1, v2, bias_ref, o_ref):
    g = pl.program_id(1)
    lane = _lane_iota((1, LANES))
    krefs = (k0, k1, k2)
    vrefs = (v0, v1, v2)
    s_lists, v_lists = [], []
    for hp in range(A_HEADS // 2):
        cols = slice(hp * LANES, (hp + 1) * LANES)
        qp = q_ref[:, cols]
        ks = [kr[:, cols] for kr in krefs]
        vs = [vr[:, cols] for vr in vrefs]
        for par in range(2):
            h = 2 * hp + par
            qm = jnp.where(_half_mask(par), qp, jnp.zeros_like(qp))
            sb = []
            for j in range(3):
                s = _dot_nt(qm, ks[j]) + bias_ref[h, :, j * BAND_Q:(j + 1) * BAND_Q]
                if j < 2:
                    s = jnp.where(g + j >= 2, s, NEG)
                sb.append(s)
            s_lists.append(sb)
            v_lists.append(vs)
    outs = _softmax_pv_many(s_lists, v_lists)
    for hp in range(A_HEADS // 2):
        o_ref[:, hp * LANES:(hp + 1) * LANES] = jnp.where(lane < HEAD_DIM, outs[2 * hp], outs[2 * hp + 1])


def _band_prompt(qs, kb, vb, bias, nb, seq):
    ng = seq // BAND_Q

    def kv(j):
        return pl.BlockSpec((BAND_Q, A_WIDTH), lambda b, g: (b * ng + jnp.maximum(g - 2 + j, 0), 0))

    cur = pl.BlockSpec((BAND_Q, A_WIDTH), lambda b, g: (b * ng + g, 0))
    return pl.pallas_call(
        _band_prompt_kernel,
        grid=(nb, ng),
        in_specs=[cur, kv(0), kv(1), kv(2), kv(0), kv(1), kv(2),
                  pl.BlockSpec((A_HEADS, BAND_Q, BAND_K), lambda b, g: (0, 0, 0))],
        out_specs=cur,
        out_shape=jax.ShapeDtypeStruct((nb * seq, A_WIDTH), F32),
        compiler_params=_cp(("parallel", "parallel")),
        name="band_prompt",
    )(qs, kb, kb, kb, vb, vb, vb, bias)


def _band_sample_kernel(q_ref, kc_ref, kn_ref, vc_ref, vn_ref, bias_ref, o_ref, *, t):
    lane = _lane_iota((1, LANES))
    lc = kc_ref.shape[1]
    for hp in range(A_HEADS // 2):
        cols = slice(hp * LANES, (hp + 1) * LANES)
        qp = q_ref[:, cols]
        pad = jnp.zeros((LANES - t, LANES), BF16)
        kc = kc_ref[cols, :].astype(BF16)
        vc = vc_ref[cols, :].astype(BF16)
        kn = jnp.concatenate([kn_ref[:, cols], pad], axis=0)
        vn = jnp.concatenate([vn_ref[:, cols], pad], axis=0)
        outs = []
        for par in range(2):
            h = 2 * hp + par
            qm = jnp.where(_half_mask(par), qp, jnp.zeros_like(qp))
            s0 = _dot(qm, kc) + bias_ref[h, :, :lc]
            s1 = _dot_nt(qm, kn) + bias_ref[h, :, lc:]
            outs.append(_softmax_pv([s0, s1], [vc, vn], (True, False)))
        o_ref[:, cols] = jnp.where(lane < HEAD_DIM, outs[0], outs[1])


def _band_sample(qs, kb, vb, cache_kt, cache_vt, layer, bias_s, nseq, t):
    lc = cache_kt.shape[3]
    new = pl.BlockSpec((t, A_WIDTH), lambda s: (s, 0))
    cache = pl.BlockSpec((None, None, A_WIDTH, lc), lambda s: (layer, s, 0, 0))
    return pl.pallas_call(
        functools.partial(_band_sample_kernel, t=t),
        grid=(nseq,),
        in_specs=[new, cache, new, cache, new,
                  pl.BlockSpec(bias_s.shape, lambda s: (0, 0, 0))],
        out_specs=new,
        out_shape=jax.ShapeDtypeStruct((nseq * t, A_WIDTH), F32),
        compiler_params=_cp(("parallel",)),
        name="band_sample",
    )(qs, cache_kt, kb, cache_vt, vb, bias_s)


FOX_TQ = 256
FOX_TK = 512


def _aug_mask(h):
    lane = _lane_iota((1, LANES))
    return (lane >= AUG_PER_HEAD * h) & (lane < AUG_PER_HEAD * (h + 1))


def _fox_prompt_kernel(q_ref, cq_ref, k_ref, ck_ref, vt_ref, o_ref, st_ref, m_ref, l_ref, acc_ref):
    hp = pl.program_id(1)
    i = pl.program_id(2)
    qp = q_ref[...]
    cq = cq_ref[...]
    qcs = []
    for par in range(2):
        qm = jnp.where(_half_mask(par), qp, jnp.zeros_like(qp))
        qa = jnp.where(_aug_mask(2 * hp + par), cq, jnp.zeros_like(cq))
        qcs.append(jnp.concatenate([qm, qa], axis=1))
    qcat = jnp.concatenate(qcs, axis=0)
    m_ref[...] = jnp.full(m_ref.shape, NEG, F32)
    l_ref[...] = jnp.zeros(l_ref.shape, F32)
    acc_ref[...] = jnp.zeros(acc_ref.shape, F32)
    ratio = FOX_TK // FOX_TQ

    def scores(j, slot):
        off = pl.multiple_of(j * FOX_TK, FOX_TK)
        kc = jnp.concatenate([k_ref[pl.ds(off, FOX_TK), :], ck_ref[pl.ds(off, FOX_TK), :]], axis=1)
        st_ref[slot] = _dot_nt(kc, qcat)

    def reduce(j, slot, masked):
        st = st_ref[slot]
        if masked:
            shape = (FOX_TK, 2 * FOX_TQ)
            qpos = (_lane_iota(shape) & (FOX_TQ - 1)) + (i % ratio) * FOX_TQ
            st = jnp.where(_row_iota(shape) <= qpos, st, NEG)
        m_old = m_ref[...]
        m_new = jnp.maximum(m_old, st.max(axis=0, keepdims=True))
        alpha = jnp.exp2(m_old - m_new)
        p = jnp.exp2(st - m_new)
        l_ref[...] = alpha * l_ref[...] + p.sum(axis=0, keepdims=True)
        pb = p.astype(BF16)
        for par in range(2):
            cols = slice(par * FOX_TQ, (par + 1) * FOX_TQ)
            pv = None
            for c in range(ratio):
                vt = vt_ref[j * ratio + c, par * HEAD_DIM:(par + 1) * HEAD_DIM, :]
                d = _dot(vt, pb[c * FOX_TQ:(c + 1) * FOX_TQ, cols])
                pv = d if pv is None else pv + d
            acc_ref[par] = alpha[:, cols] * acc_ref[par] + pv
        m_ref[...] = m_new

    nfull = i // ratio
    scores(0, 0)

    def body(jj, carry):
        j = 2 * jj
        scores(j + 1, 1)
        reduce(j, 0, False)
        scores(j + 2, 0)
        reduce(j + 1, 1, False)
        return carry

    lax.fori_loop(0, nfull // 2, body, 0)

    @pl.when(nfull % 2 == 0)
    def _():
        reduce(nfull, 0, True)

    @pl.when(nfull % 2 == 1)
    def _():
        scores(nfull, 1)
        reduce(nfull - 1, 0, False)
        reduce(nfull, 1, True)

    l = l_ref[...]
    ot = jnp.concatenate([acc_ref[0] / l[:, :FOX_TQ], acc_ref[1] / l[:, FOX_TQ:]], axis=0)
    o_ref[...] = ot.T


def _fox_prompt(qs, kb, vtb, cq, ck, nb, seq):
    assert VT_CHUNK == FOX_TQ
    nq = seq // FOX_TQ
    qspec = pl.BlockSpec((FOX_TQ, LANES), lambda b, hp, i: (b * nq + i, hp))
    cqspec = pl.BlockSpec((FOX_TQ, LANES), lambda b, hp, i: (b * nq + i, 0))
    kspec = pl.BlockSpec((seq, LANES), lambda b, hp, i: (b, hp))
    ckspec = pl.BlockSpec((seq, LANES), lambda b, hp, i: (b, 0))
    vspec = pl.BlockSpec((None, nq, LANES, FOX_TQ), lambda b, hp, i: (b, 0, hp, 0))
    return pl.pallas_call(
        _fox_prompt_kernel,
        grid=(nb, C_HEADS // 2, nq),
        in_specs=[qspec, cqspec, kspec, ckspec, vspec],
        out_specs=qspec,
        out_shape=jax.ShapeDtypeStruct((nb * seq, C_WIDTH), F32),
        scratch_shapes=[pltpu.VMEM((2, FOX_TK, 2 * FOX_TQ), F32),
                        pltpu.VMEM((1, 2 * FOX_TQ), F32), pltpu.VMEM((1, 2 * FOX_TQ), F32),
                        pltpu.VMEM((2, HEAD_DIM, FOX_TQ), F32)],
        compiler_params=_cp(("parallel", "parallel", "arbitrary")),
        name="fox_prompt",
    )(qs, cq, kb, ck, vtb)


def _fox_sample_kernel(q_ref, cq_ref, kc_ref, kn_ref, ck_ref, vc_ref, vn_ref, o_ref, *, t, past):
    lane = _lane_iota((1, LANES))
    cq = cq_ref[past:past + t, :]
    pad = jnp.zeros((LANES - t, LANES), BF16)
    ck_cache_t = ck_ref[:past, :].astype(F32).T.astype(BF16)
    ck_new = ck_ref[past:, :]
    vis = _lane_iota((t, LANES)) <= _row_iota((t, LANES))
    for hp in range(C_HEADS // 2):
        cols = slice(hp * LANES, (hp + 1) * LANES)
        qp = q_ref[:, cols]
        kc = jnp.concatenate([kc_ref[cols, :].astype(BF16), ck_cache_t], axis=0)
        kn = jnp.concatenate([jnp.concatenate([kn_ref[:, cols], pad], axis=0), ck_new], axis=1)
        vc = vc_ref[cols, :].astype(BF16)
        vn = jnp.concatenate([vn_ref[:, cols], pad], axis=0)
        outs = []
        for par in range(2):
            qm = jnp.where(_half_mask(par), qp, jnp.zeros_like(qp))
            qa = jnp.where(_aug_mask(2 * hp + par), cq, jnp.zeros_like(cq))
            qc = jnp.concatenate([qm, qa], axis=1)
            s0 = _dot(qc, kc)
            s1 = jnp.where(vis, _dot_nt(qc, kn), NEG)
            outs.append(_softmax_pv([s0, s1], [vc, vn], (True, False)))
        o_ref[:, cols] = jnp.where(lane < HEAD_DIM, outs[0], outs[1])


def _fox_sample(qs, kb, vb, cq, ck, cache_kt, cache_vt, layer, nseq, t):
    past = cache_kt.shape[3]
    new = pl.BlockSpec((t, C_WIDTH), lambda s: (s, 0))
    cache = pl.BlockSpec((None, None, C_WIDTH, past), lambda s: (layer, s, 0, 0))
    aug = pl.BlockSpec((None, past + LANES, LANES), lambda s: (s, 0, 0))
    return pl.pallas_call(
        functools.partial(_fox_sample_kernel, t=t, past=past),
        grid=(nseq,),
        in_specs=[new, aug, cache, new, aug, cache, new],
        out_specs=new,
        out_shape=jax.ShapeDtypeStruct((nseq * t, C_WIDTH), F32),
        compiler_params=_cp(("parallel",)),
        name="fox_sample",
    )(qs, cq, cache_kt, kb, ck, cache_vt, vb)


def _cross_kernel(q_ref, k_ref, v_ref, o_ref):
    lane = _lane_iota((1, LANES))
    for hp in range(M_HEADS // 2):
        cols = slice(hp * LANES, (hp + 1) * LANES)
        qp = q_ref[:, cols]
        k = k_ref[cols, :].astype(BF16)
        v = v_ref[cols, :].astype(BF16)
        outs = []
        for par in range(2):
            qm = jnp.where(_half_mask(par), qp, jnp.zeros_like(qp))
            outs.append(_softmax_pv([_dot(qm, k)], [v], (True,)))
        o_ref[:, cols] = jnp.where(lane < HEAD_DIM, outs[0], outs[1])


def _cross(qs, mkt, mvt, layer, rows_per_seq, name):
    nrows = qs.shape[0]
    n_mem = mkt.shape[3]
    tq = _pick(rows_per_seq, (512, 256, 128, 64))
    per = rows_per_seq // tq
    mem = pl.BlockSpec((None, None, M_WIDTH, n_mem), lambda i: (layer, i // per, 0, 0))
    rows = pl.BlockSpec((tq, M_WIDTH), lambda i: (i, 0))
    return pl.pallas_call(
        _cross_kernel,
        grid=(nrows // tq,),
        in_specs=[rows, mem, mem],
        out_specs=rows,
        out_shape=jax.ShapeDtypeStruct((nrows, M_WIDTH), F32),
        compiler_params=_cp(("parallel",)),
        name=name,
    )(qs, mkt, mvt)


SSD_Q = 128
CONV_PAD = SUBLANES
SSD_GROUP = 4


def _ssd_kernel(xbc_ref, z_ref, sm_ref, cs0_ref, hs0_ref, cw_ref, cb_ref, dtb_ref, alog_ref, dsk_ref,
                gn_ref, exp_ref, tri_ref, qm_ref,
                o_ref, hs_out_ref, cs_out_ref, hs_ref, xext_ref, *, nv, nchunk):
    c = pl.program_id(1)
    q = SSD_Q
    grp_ids = range(SSD_GROUP)
    lane = _lane_iota((1, LANES))
    head_lane = (lane >= DT_LANE) & (lane < DT_LANE + B_HEADS)
    a_row = jnp.where(head_lane, -jnp.exp(alog_ref[...]), 0.0)
    causal = _lane_iota((q, q)) <= _row_iota((q, q))
    heads_per_group = B_HEADS // B_GROUPS
    expand = exp_ref[...]
    tri = tri_ref[...]
    qm = qm_ref[...]

    @pl.when(c == 0)
    def _():
        for g in grp_ids:
            hs_ref[g] = hs0_ref[g].T * qm
            xext_ref[g, 0:CONV_PAD, :] = cs0_ref[g]

    for g in grp_ids:
        xext_ref[g, CONV_PAD:CONV_PAD + nv, :] = xbc_ref[g]
        if nv < q:
            xext_ref[g, CONV_PAD + nv:CONV_PAD + q, :] = jnp.zeros((q - nv, B_XBC), F32)
    xa = []
    for g in grp_ids:
        y = cb_ref[...]
        for tap in range(B_CONV):
            y = y + xext_ref[g, pl.ds(CONV_PAD - (B_CONV - 1) + tap, q), :] * cw_ref[tap:tap + 1, :]
        xa.append(_silu(y))
    for g in grp_ids:
        tail = xext_ref[g, nv:nv + CONV_PAD, :]
        xext_ref[g, 0:CONV_PAD, :] = tail
        cs_out_ref[g] = tail
    xs = [x[:, :B_WIDTH] for x in xa]
    bm = [x[:, B_WIDTH:B_WIDTH + LANES].astype(BF16) for x in xa]
    cm = [x[:, B_WIDTH + LANES:] for x in xa]

    dt = []
    for g in grp_ids:
        d = _softplus(sm_ref[g] + dtb_ref[...])
        if nv < q:
            d = jnp.concatenate([d, jnp.zeros((q - nv, LANES), F32)], axis=0)
        dt.append(jnp.where(head_lane, d, 0.0))
    acum = [_sel_dot(tri, d * a_row) for d in dt]
    dt_e = [_dot_sel(d, expand) for d in dt]
    acum_e = [_dot_sel(a, expand) for a in acum]
    alast_e = [a[q - 1:q, :] for a in acum_e]
    xdt = [x * d for x, d in zip(xs, dt_e)]

    hs = [hs_ref[g] for g in grp_ids]
    y_off = [_dot(cm[g].astype(BF16), hs[g].astype(BF16)) * jnp.exp(acum_e[g]) for g in grp_ids]
    upd = [_dot_tn(bm[g], (xdt[g] * jnp.exp(alast_e[g] - acum_e[g])).astype(BF16)) for g in grp_ids]
    hs_new = [(jnp.exp(alast_e[g]) * hs[g] + upd[g]) * qm for g in grp_ids]
    for g in grp_ids:
        hs_ref[g] = hs_new[g]

    @pl.when(c == nchunk - 1)
    def _():
        for g in grp_ids:
            hs_out_ref[g] = hs_new[g].T

    acum_t = [a.T for a in acum]
    xdt_b = [x.astype(BF16) for x in xdt]
    cbm = [[None] * B_GROUPS for _ in grp_ids]
    for grp in range(B_GROUPS):
        for g in grp_ids:
            cg = jnp.where(_half_mask(grp), cm[g], 0.0).astype(BF16)
            cbm[g][grp] = _dot_nt(cg, bm[g])
    y_pairs = [[] for _ in grp_ids]
    for hp in range(B_HEADS // 2):
        pair = [[] for _ in grp_ids]
        for par in range(2):
            h = 2 * hp + par
            for g in grp_ids:
                seg = acum[g][:, DT_LANE + h:DT_LANE + h + 1] - acum_t[g][DT_LANE + h:DT_LANE + h + 1, :]
                lmat = jnp.exp(jnp.where(causal, seg, -jnp.inf))
                m = (cbm[g][h // heads_per_group] * lmat).astype(BF16)
                pair[g].append(_dot(m, xdt_b[g][:, hp * LANES:(hp + 1) * LANES]))
        for g in grp_ids:
            y_pairs[g].append(jnp.where(lane < HEAD_DIM, pair[g][0], pair[g][1]))

    for g in grp_ids:
        yt = y_off[g] + jnp.concatenate(y_pairs[g], axis=1) + dsk_ref[...] * xs[g]
        if nv < q:
            yt = yt[:nv]
        yz = yt * _silu(z_ref[g])
        ms = jnp.mean(yz * yz, axis=-1, keepdims=True)
        o_ref[g] = yz * lax.rsqrt(ms + EPS) * gn_ref[...]


def _ssd(u, cs0, hs0, consts, nseq, rows_per_seq):
    nv = min(SSD_Q, rows_per_seq)
    nchunk = rows_per_seq // nv
    gsz = SSD_GROUP
    assert nseq % gsz == 0
    cw, cb, dtb, alog, dsk, gn, expand, tri, qmask = consts
    u3 = u.reshape(nseq, rows_per_seq, U_WIDTH)

    def ucol(name, width):
        idx = _U_OFF[name] // width
        assert idx * width == _U_OFF[name]
        return pl.BlockSpec((gsz, nv, width), lambda s, c: (s, c, idx))

    full = lambda a: pl.BlockSpec(a.shape, lambda s, c: (0,) * a.ndim)
    cs_spec = pl.BlockSpec((gsz, CONV_PAD, B_XBC), lambda s, c: (s, 0, 0))
    hs_spec = pl.BlockSpec((gsz, B_WIDTH, LANES), lambda s, c: (s, 0, 0))
    o, hs, cs = pl.pallas_call(
        functools.partial(_ssd_kernel, nv=nv, nchunk=nchunk),
        grid=(nseq // gsz, nchunk),
        in_specs=[ucol('b_xbc', B_XBC), ucol('b_z', B_WIDTH), ucol('b_dt', LANES), cs_spec, hs_spec,
                  full(cw), full(cb), full(dtb), full(alog), full(dsk), full(gn), full(expand), full(tri),
                  full(qmask)],
        out_specs=[pl.BlockSpec((gsz, nv, B_WIDTH), lambda s, c: (s, c, 0)), hs_spec, cs_spec],
        out_shape=[jax.ShapeDtypeStruct((nseq, rows_per_seq, B_WIDTH), F32),
                   jax.ShapeDtypeStruct((nseq, B_WIDTH, LANES), F32),
                   jax.ShapeDtypeStruct((nseq, CONV_PAD, B_XBC), F32)],
        scratch_shapes=[pltpu.VMEM((gsz, LANES, B_WIDTH), F32),
                        pltpu.VMEM((gsz, SSD_Q + CONV_PAD, B_XBC), F32)],
        compiler_params=_cp(("parallel", "arbitrary")),
        name="ssd",
    )(u3, u3, u3, cs0, hs0, cw, cb, dtb, alog, dsk, gn, expand, tri, qmask)
    return o.reshape(nseq * rows_per_seq, B_WIDTH), hs, cs


def _ssd_static_consts():
    expand = np.zeros((LANES, B_WIDTH), np.float32)
    for h in range(B_HEADS):
        expand[DT_LANE + h, h * HEAD_DIM:(h + 1) * HEAD_DIM] = 1.0
    tri = np.tril(np.ones((SSD_Q, SSD_Q), np.float32))
    qmask = np.zeros((LANES, B_WIDTH), np.float32)
    half = B_WIDTH // B_GROUPS
    for g in range(B_GROUPS):
        qmask[g * B_STATE:(g + 1) * B_STATE, g * half:(g + 1) * half] = 1.0
    return jnp.asarray(expand, BF16), jnp.asarray(tri, BF16), jnp.asarray(qmask)


def _stack_state(h):
    r = h.reshape(h.shape[0], B_WIDTH, B_STATE)
    return jnp.concatenate([r, r], axis=2)


def _unstack_state(hst):
    n = hst.shape[0]
    first = (np.arange(B_WIDTH) < B_WIDTH // B_GROUPS)[None, :, None]
    return jnp.where(first, hst[:, :, :B_STATE], hst[:, :, B_STATE:]).reshape(n, B_HEADS, HEAD_DIM, B_STATE)


def _merge_kernel(x, oa, az, ob, oc, cz, om, mz, gt, wpa, wpb, wpc, wpm, wout, o):
    d = x.shape[1]
    pa = _dot((oa[...] * _silu(az[...])).astype(BF16), wpa[...])
    pb = _dot(ob[...].astype(BF16), wpb[...])
    pc = _dot((oc[...] * _silu(cz[...])).astype(BF16), wpc[...])
    pm = _dot((om[...] * _silu(mz[...])).astype(BF16), wpm[...])
    mix = (gt[:, 0:d].astype(F32) * pa + gt[:, d:2 * d].astype(F32) * pb
           + gt[:, 2 * d:3 * d].astype(F32) * pc + gt[:, 3 * d:4 * d].astype(F32) * pm)
    o[...] = x[...] + _dot(mix.astype(BF16), wout[...])


def _merge(x, u, gates, oa, ob, oc, om, wpa, wpb, wpc, wpm, wout):
    m, d = x.shape
    tm = _pick(m, (512, 256, 128))

    def ucol(name, width):
        idx = _U_OFF[name] // width
        assert idx * width == _U_OFF[name]
        return pl.BlockSpec((tm, width), lambda i: (i, idx))

    def rows(width):
        return pl.BlockSpec((tm, width), lambda i: (i, 0))

    full = lambda a: pl.BlockSpec(a.shape, lambda i: (0, 0))
    return pl.pallas_call(
        _merge_kernel,
        grid=(m // tm,),
        in_specs=[rows(d), rows(A_WIDTH), ucol('a_z', A_WIDTH), rows(B_WIDTH), rows(C_WIDTH),
                  ucol('c_z', C_WIDTH), rows(M_WIDTH), ucol('m_z', M_WIDTH), rows(N_BRANCH * d),
                  full(wpa), full(wpb), full(wpc), full(wpm), full(wout)],
        out_specs=rows(d),
        out_shape=jax.ShapeDtypeStruct((m, d), F32),
        compiler_params=_cp(("parallel",)),
        name="merge",
    )(x, oa, u, ob, oc, u, om, u, gates, wpa, wpb, wpc, wpm, wout)


def _row(v):
    return v.reshape(1, -1).astype(F32)


def _lane_row(v, lane0):
    return jnp.zeros((1, LANES), F32).at[0, lane0:lane0 + v.shape[0]].set(v.astype(F32))


def _block_diag_mean(width):
    seg = np.arange(width) // HEAD_DIM
    return jnp.asarray((seg[:, None] == seg[None, :]).astype(np.float32) / HEAD_DIM, BF16)


def kernel(x_prompt, x_sample, mem_prompt, cache_a_k, cache_a_v, cache_c_k, cache_c_v, cache_c_logf, state_b_ssm, state_b_conv, cache_mem_k, cache_mem_v, g_norm, w_in, a_qnorm, a_knorm, a_rel, b_conv_w, b_conv_b, b_dt_bias, b_a_log, b_d, b_norm, c_qnorm, c_knorm, c_fbias, m_norm, w_mkv, m_qnorm, m_knorm, w_pa, w_pb, w_pc, w_pm, w_out):
    nb, seq, d = x_prompt.shape
    ns, tdec, _ = x_sample.shape
    depth = g_norm.shape[0]
    n_mem = mem_prompt.shape[1]
    past = cache_c_k.shape[2]
    la_cache = cache_a_k.shape[2]
    mp = nb * seq
    md = ns * tdec
    assert seq % FOX_TK == 0 and seq % BAND_Q == 0 and seq % SSD_Q == 0
    assert tdec == CHUNK and la_cache == A_WIN and past % LANES == 0

    xp = x_prompt.reshape(mp, d)
    xs = x_sample.reshape(md, d)
    mem = mem_prompt.reshape(nb * n_mem, d)

    def feature_major(c):
        dd, n, t, h, e = c.shape
        return jnp.transpose(c, (0, 1, 3, 4, 2)).reshape(dd, n, h * e, t)

    cak, cav = feature_major(cache_a_k), feature_major(cache_a_v)
    cck, ccv = feature_major(cache_c_k), feature_major(cache_c_v)
    cmk, cmv = feature_major(cache_mem_k), feature_major(cache_mem_v)

    bd512 = _block_diag_mean(512)
    bd256 = _block_diag_mean(M_WIDTH)
    cum_consts = _cum_consts()
    expand, tri_q, qmask = _ssd_static_consts()
    scale = HEAD_DIM ** -0.5 * LOG2E

    outs = {k: [] for k in ('pa_k', 'pa_v', 'pc_k', 'pc_v', 'pc_f', 'pb_s', 'pb_c', 'pm_k', 'pm_v',
                            'sa_k', 'sa_v', 'sc_k', 'sc_v', 'sc_f', 'sb_s', 'sb_c')}

    for l in range(depth):
        wt = jnp.transpose(w_in[l])
        pieces = [wt[_SRC[n][0]:_SRC[n][0] + _SRC[n][1]] for n in _U_ORDER]
        used = sum(p.shape[0] for p in pieces)
        w_u = jnp.concatenate(pieces + [jnp.zeros((U_WIDTH - used, d), F32)], axis=0).astype(BF16)
        g0, gw = _SRC['gate']
        w_g = wt[g0:g0 + gw].astype(BF16)
        g_row = _row(g_norm[l])
        gains = (_row(jnp.tile(a_qnorm[l], A_HEADS)) * scale, _row(jnp.tile(a_knorm[l], A_HEADS)),
                 _row(jnp.tile(c_qnorm[l], C_HEADS)) * scale, _row(jnp.tile(c_knorm[l], C_HEADS)),
                 _row(jnp.tile(m_qnorm[l], M_HEADS)) * scale, _lane_row(c_fbias[l], CF_LANE), bd512)

        u_p = _proj(xp, g_row, w_u, act=None, out_dtype=F32, name="proj_u", w_rows=True)
        gates_p = _proj(xp, g_row, w_g, act='sigmoid', out_dtype=BF16, name="proj_gate", w_rows=True)
        (a_qs_p, a_kb_p, a_vb_p, pa_kt, pa_vt, c_qs_p, c_kb_p, pc_kt, pc_vt, c_vtb_p, m_qs_p, lf_p,
         pc_ft) = _prep_prompt(u_p, gains, nb, seq)
        u_s = _proj(xs, g_row, w_u, act=None, out_dtype=F32, name="proj_u", w_rows=True)
        gates_s = _proj(xs, g_row, w_g, act='sigmoid', out_dtype=BF16, name="proj_gate", w_rows=True)
        (a_qs_s, a_kb_s, a_vb_s, sa_k4, sa_v4, c_qs_s, c_kb_s, c_vb_s, sc_k4, sc_v4, m_qs_s,
         lf_s) = _prep_sample(u_s, gains)

        kv = _proj(mem, _row(m_norm[l]), w_mkv[l].astype(BF16), act=None, out_dtype=F32, name="proj_mem",
                   w_rows=False)
        mkt, mvt = _memkv(kv, _row(jnp.tile(m_knorm[l], M_HEADS)), bd256, nb, n_mem)

        bias = _band_bias(a_rel[l])
        o_a_p = _band_prompt(a_qs_p, a_kb_p, a_vb_p, bias, nb, seq)
        bias_s = jnp.concatenate(
            [bias[:, :tdec, :la_cache + tdec],
             jnp.full((A_HEADS, tdec, LANES - tdec), NEG, F32)], axis=2)
        o_a_s = _band_sample(a_qs_s, a_kb_s, a_vb_s, cak, cav, l, bias_s, ns, tdec)

        cq_p, ck_p = _cum(lf_p.reshape(nb, seq, LANES), cum_consts)
        o_c_p = _fox_prompt(c_qs_p, c_kb_p, c_vtb_p, cq_p.reshape(mp, LANES), ck_p.reshape(mp, LANES), nb, seq)
        lf_cache = jnp.pad(cache_c_logf[l].astype(F32), ((0, 0), (0, 0), (CF_LANE, LANES - CF_LANE - C_HEADS)))
        lf_cat = jnp.concatenate([lf_cache, lf_s.reshape(ns, tdec, LANES),
                                  jnp.zeros((ns, LANES - tdec, LANES), F32)], axis=1)
        cq_s, ck_s = _cum(lf_cat, cum_consts)
        o_c_s = _fox_sample(c_qs_s, c_kb_s, c_vb_s, cq_s, ck_s, cck, ccv, l, ns, tdec)

        ssd_consts = (jnp.pad(b_conv_w[l], ((0, SUBLANES - B_CONV), (0, 0))), _row(b_conv_b[l]),
                      _lane_row(b_dt_bias[l], DT_LANE), _lane_row(b_a_log[l], DT_LANE),
                      _row(jnp.repeat(b_d[l], HEAD_DIM)), _row(b_norm[l]), expand, tri_q, qmask)
        o_b_p, hs_p, cs_p = _ssd(u_p, jnp.zeros((nb, CONV_PAD, B_XBC), F32),
                                 jnp.zeros((nb, B_WIDTH, LANES), F32), ssd_consts, nb, seq)
        cs0 = jnp.pad(state_b_conv[l].astype(F32), ((0, 0), (CONV_PAD - (B_CONV - 1), 0), (0, 0)))
        o_b_s, hs_s, cs_s = _ssd(u_s, cs0, _stack_state(state_b_ssm[l].astype(F32)), ssd_consts, ns, tdec)

        o_m_p = _cross(m_qs_p, mkt[None], mvt[None], 0, seq, "cross_prompt")
        o_m_s = _cross(m_qs_s, cmk, cmv, l, tdec, "cross_sample")

        w_merge = (w_pa[l].astype(BF16), w_pb[l].astype(BF16), w_pc[l].astype(BF16), w_pm[l].astype(BF16),
                   w_out[l].astype(BF16))
        xp = _merge(xp, u_p, gates_p, o_a_p, o_b_p, o_c_p, o_m_p, *w_merge)
        xs = _merge(xs, u_s, gates_s, o_a_s, o_b_s, o_c_s, o_m_s, *w_merge)

        outs['pa_k'].append(pa_kt)
        outs['pa_v'].append(pa_vt)
        outs['pc_k'].append(pc_kt)
        outs['pc_v'].append(pc_vt)
        outs['pc_f'].append(pc_ft)
        outs['pb_s'].append(_unstack_state(hs_p))
        outs['pb_c'].append(cs_p[:, CONV_PAD - (B_CONV - 1):])
        outs['pm_k'].append(mkt)
        outs['pm_v'].append(mvt)
        outs['sa_k'].append(sa_k4)
        outs['sa_v'].append(sa_v4)
        outs['sc_k'].append(sc_k4)
        outs['sc_v'].append(sc_v4)
        outs['sc_f'].append(lf_s[:, CF_LANE:CF_LANE + C_HEADS].reshape(ns, tdec, C_HEADS))
        outs['sb_s'].append(_unstack_state(hs_s))
        outs['sb_c'].append(cs_s[:, CONV_PAD - (B_CONV - 1):])

    st = jnp.stack

    def token_major(name, heads):
        a = st(outs[name])
        dd, n, _, t = a.shape
        return jnp.transpose(a.reshape(dd, n, heads, HEAD_DIM, t), (0, 1, 4, 2, 3))

    def sample_heads(name):
        return st(outs[name]).reshape(depth, ns, tdec, A_HEADS, HEAD_DIM)

    return (xp.reshape(nb, seq, d), xs.reshape(ns, tdec, d),
            token_major('pa_k', A_HEADS), token_major('pa_v', A_HEADS),
            token_major('pc_k', C_HEADS), token_major('pc_v', C_HEADS),
            jnp.transpose(st(outs['pc_f']), (0, 1, 3, 2)),
            st(outs['pb_s']), st(outs['pb_c']), token_major('pm_k', M_HEADS), token_major('pm_v', M_HEADS),
            sample_heads('sa_k'), sample_heads('sa_v'), sample_heads('sc_k'), sample_heads('sc_v'),
            st(outs['sc_f']), st(outs['sb_s']), st(outs['sb_c']))
```

```python
import functools

import numpy as np
import jax
import jax.numpy as jnp
from jax import lax
from jax.experimental import pallas as pl
from jax.experimental.pallas import tpu as pltpu

F32 = jnp.float32
BF16 = jnp.bfloat16

EPS = 1e-6
NEG = -1e30
LOG2E = 1.4426950408889634

HEAD_DIM = 64
CHUNK = 64
A_HEADS = 8
A_WIDTH = 512
A_LEFT_CHUNKS = 8
A_WIN = A_LEFT_CHUNKS * CHUNK
A_REL_CLIP = 128
B_HEADS = 8
B_WIDTH = 512
B_GROUPS = 2
B_STATE = 64
B_CONV = 4
B_XBC = B_WIDTH + 2 * B_GROUPS * B_STATE
C_HEADS = 8
C_WIDTH = 512
M_HEADS = 4
M_WIDTH = 256
N_BRANCH = 4

LANES = 128
SUBLANES = 8
MXU_WIDTH = 256
VMEM_LIMIT = 48 * 1024 * 1024

_SPLITS = (('a_q', A_WIDTH), ('a_k', A_WIDTH), ('a_v', A_WIDTH), ('a_z', A_WIDTH),
           ('b_z', B_WIDTH), ('b_xbc', B_XBC), ('b_dt', B_HEADS),
           ('c_q', C_WIDTH), ('c_k', C_WIDTH), ('c_v', C_WIDTH), ('c_f', C_HEADS), ('c_z', C_WIDTH),
           ('m_q', M_WIDTH), ('m_z', M_WIDTH), ('gate', N_BRANCH * 1024))
_SRC = {}
_off = 0
for _n, _w in _SPLITS:
    _SRC[_n] = (_off, _w)
    _off += _w

_U_ORDER = ('a_q', 'a_k', 'a_v', 'a_z', 'c_q', 'c_k', 'c_v', 'c_z', 'b_z', 'b_xbc', 'm_q', 'm_z', 'b_dt', 'c_f')
_U_OFF = {}
_off = 0
for _n in _U_ORDER:
    _U_OFF[_n] = _off
    _off += _SRC[_n][1]
U_WIDTH = 6144
DT_LANE = 0
CF_LANE = B_HEADS
AUG_PER_HEAD = 6


def _cp(sem):
    return pltpu.CompilerParams(dimension_semantics=sem, vmem_limit_bytes=VMEM_LIMIT)


def _pick(n, cands):
    for c in cands:
        if n % c == 0:
            return c
    raise ValueError(f"no tile for {n} in {cands}")


def _dot(a, b):
    return jnp.dot(a, b, preferred_element_type=F32)


def _dot_nt(a, b):
    return lax.dot_general(a, b, (((1,), (1,)), ((), ())), preferred_element_type=F32)


def _dot_tn(a, b):
    return lax.dot_general(a, b, (((0,), (0,)), ((), ())), preferred_element_type=F32)


def _split3(x):
    hi = x.astype(BF16)
    r = x - hi.astype(F32)
    mid = r.astype(BF16)
    lo = (r - mid.astype(F32)).astype(BF16)
    return hi, mid, lo


def _dot_sel(x, sel):
    hi, mid, lo = _split3(x)
    return _dot(hi, sel) + _dot(mid, sel) + _dot(lo, sel)


def _sel_dot(sel, x):
    hi, mid, lo = _split3(x)
    return _dot(sel, hi) + _dot(sel, mid) + _dot(sel, lo)


def _sigmoid(x):
    return 1.0 / (1.0 + jnp.exp(-x))


def _silu(x):
    return x * _sigmoid(x)


def _softplus(x):
    return jnp.maximum(x, 0.0) + jnp.log1p(jnp.exp(-jnp.abs(x)))


def _log_sigmoid(x):
    return jnp.minimum(x, 0.0) - jnp.log1p(jnp.exp(-jnp.abs(x)))


def _head_norm(x, bd):
    x2 = x * x
    hi = x2.astype(BF16)
    lo = (x2 - hi.astype(F32)).astype(BF16)
    w = bd.shape[0]
    ms = [_dot(hi[:, c:c + w], bd) + _dot(lo[:, c:c + w], bd) for c in range(0, x.shape[1], w)]
    ms = ms[0] if len(ms) == 1 else jnp.concatenate(ms, axis=1)
    return x * lax.rsqrt(ms + EPS)


def _lane_iota(shape):
    return lax.broadcasted_iota(jnp.int32, shape, len(shape) - 1)


def _row_iota(shape):
    return lax.broadcasted_iota(jnp.int32, shape, len(shape) - 2)


def _proj_kernel(x_ref, g_ref, w_ref, o_ref, h_ref, *, act, w_rows):
    @pl.when(pl.program_id(1) == 0)
    def _():
        x = x_ref[...]
        ms = jnp.mean(x * x, axis=-1, keepdims=True)
        h_ref[...] = (x * lax.rsqrt(ms + EPS) * g_ref[...]).astype(BF16)

    u = _dot_nt(h_ref[...], w_ref[...]) if w_rows else _dot(h_ref[...], w_ref[...])
    if act == 'sigmoid':
        u = _sigmoid(u)
    o_ref[...] = u.astype(o_ref.dtype)


def _proj(x, g_row, w_bf, *, act, out_dtype, name, w_rows):
    m, d = x.shape
    n = w_bf.shape[0] if w_rows else w_bf.shape[1]
    tm = _pick(m, (2048, 1024, 512, 256, 128))
    tn = _pick(n, (1024, 512, 256, 128))
    w_spec = pl.BlockSpec((tn, d), lambda i, j: (j, 0)) if w_rows else pl.BlockSpec((d, tn), lambda i, j: (0, j))
    return pl.pallas_call(
        functools.partial(_proj_kernel, act=act, w_rows=w_rows),
        grid=(m // tm, n // tn),
        in_specs=[pl.BlockSpec((tm, d), lambda i, j: (i, 0)),
                  pl.BlockSpec((1, d), lambda i, j: (0, 0)),
                  w_spec],
        out_specs=pl.BlockSpec((tm, tn), lambda i, j: (i, j)),
        out_shape=jax.ShapeDtypeStruct((m, n), out_dtype),
        scratch_shapes=[pltpu.VMEM((tm, d), BF16)],
        compiler_params=_cp(("parallel", "arbitrary")),
        name=name,
    )(x, g_row, w_bf)


PREP_TS = 512
VT_CHUNK = 256


def _prep_common(aq, ak, av, cq, ck, cv, mq, sm, gaq, gak, gcq, gck, gmq, fb, bd):
    bdv = bd[...]
    return dict(
        a_qs=(_head_norm(aq[...], bdv) * gaq[...]).astype(BF16),
        a_kn=_head_norm(ak[...], bdv) * gak[...],
        a_v=av[...],
        c_qs=(_head_norm(cq[...], bdv) * gcq[...]).astype(BF16),
        c_kn=_head_norm(ck[...], bdv) * gck[...],
        c_v=cv[...],
        m_qs=(_head_norm(mq[...], bdv) * gmq[...]).astype(BF16),
        lf=_log_sigmoid(sm[...] + fb[...]))


def _prep_prompt_kernel(aq, ak, av, cq, ck, cv, mq, sm, gaq, gak, gcq, gck, gmq, fb, bd,
                        o_aq, o_akb, o_avtb, o_akt, o_avt, o_cq, o_ckb, o_ckt, o_cvt, o_cvtb, o_mq, o_lf, o_lft,
                        *, steps_per_seq):
    r = _prep_common(aq, ak, av, cq, ck, cv, mq, sm, gaq, gak, gcq, gck, gmq, fb, bd)
    o_aq[...] = r['a_qs']
    o_akb[...] = r['a_kn'].astype(BF16)
    avt = r['a_v'].T
    avtb = avt.astype(BF16)
    for c in range(PREP_TS // VT_CHUNK):
        o_avtb[c] = avtb[:, c * VT_CHUNK:(c + 1) * VT_CHUNK]

    @pl.when(pl.program_id(0) % steps_per_seq == steps_per_seq - 1)
    def _():
        o_akt[...] = r['a_kn'].T
        o_avt[...] = avt

    o_cq[...] = r['c_qs']
    o_ckb[...] = r['c_kn'].astype(BF16)
    o_ckt[...] = r['c_kn'].T
    cvt = r['c_v'].T
    o_cvt[...] = cvt
    cvtb = cvt.astype(BF16)
    for c in range(PREP_TS // VT_CHUNK):
        o_cvtb[c] = cvtb[:, c * VT_CHUNK:(c + 1) * VT_CHUNK]
    o_mq[...] = r['m_qs']
    o_lf[...] = r['lf']
    o_lft[...] = r['lf'].T[CF_LANE:CF_LANE + C_HEADS, :]


def _prep_sample_kernel(aq, ak, av, cq, ck, cv, mq, sm, gaq, gak, gcq, gck, gmq, fb, bd,
                        o_aq, o_akb, o_avb, o_ak4, o_av4, o_cq, o_ckb, o_cvb, o_ck4, o_cv4, o_mq, o_lf):
    r = _prep_common(aq, ak, av, cq, ck, cv, mq, sm, gaq, gak, gcq, gck, gmq, fb, bd)
    ts = o_aq.shape[0]
    o_aq[...] = r['a_qs']
    o_akb[...] = r['a_kn'].astype(BF16)
    o_avb[...] = r['a_v'].astype(BF16)
    o_ak4[...] = r['a_kn'].reshape(ts, A_HEADS, HEAD_DIM)
    o_av4[...] = r['a_v'].reshape(ts, A_HEADS, HEAD_DIM)
    o_cq[...] = r['c_qs']
    o_ckb[...] = r['c_kn'].astype(BF16)
    o_cvb[...] = r['c_v'].astype(BF16)
    o_ck4[...] = r['c_kn'].reshape(ts, C_HEADS, HEAD_DIM)
    o_cv4[...] = r['c_v'].reshape(ts, C_HEADS, HEAD_DIM)
    o_mq[...] = r['m_qs']
    o_lf[...] = r['lf']


def _prep_in_specs(ts):
    def ucol(name, width):
        idx = _U_OFF[name] // width
        assert idx * width == _U_OFF[name]
        return pl.BlockSpec((ts, width), lambda i: (i, idx))

    def row(width):
        return pl.BlockSpec((1, width), lambda i: (0, 0))

    return [ucol('a_q', 512), ucol('a_k', 512), ucol('a_v', 512),
            ucol('c_q', 512), ucol('c_k', 512), ucol('c_v', 512),
            ucol('m_q', 256), ucol('b_dt', LANES),
            row(512), row(512), row(512), row(512), row(256), row(LANES),
            pl.BlockSpec((MXU_WIDTH, MXU_WIDTH), lambda i: (0, 0))]


def _prep_prompt(u, gains, nb, seq):
    ts = PREP_TS
    assert seq % ts == 0 and min(A_WIN, seq) == ts
    sps = seq // ts
    m = nb * seq
    sds = jax.ShapeDtypeStruct
    rows = lambda width: pl.BlockSpec((ts, width), lambda i: (i, 0))
    last = pl.BlockSpec((None, 512, ts), lambda i: (i // sps, 0, 0))
    feat = pl.BlockSpec((None, 512, ts), lambda i: (i // sps, 0, i % sps))
    nvc = ts // VT_CHUNK
    chunks = pl.BlockSpec((None, nvc, 512, VT_CHUNK), lambda i: (i // sps, i % sps, 0, 0))
    out_specs = [rows(512), rows(512), chunks, last, last,
                 rows(512), rows(512), feat, feat, chunks,
                 rows(256), rows(LANES),
                 pl.BlockSpec((None, C_HEADS, ts), lambda i: (i // sps, 0, i % sps))]
    out_shape = [sds((m, 512), BF16), sds((m, 512), BF16), sds((nb, seq // VT_CHUNK, 512, VT_CHUNK), BF16),
                 sds((nb, 512, ts), F32), sds((nb, 512, ts), F32),
                 sds((m, 512), BF16), sds((m, 512), BF16), sds((nb, 512, seq), F32), sds((nb, 512, seq), F32),
                 sds((nb, seq // VT_CHUNK, 512, VT_CHUNK), BF16),
                 sds((m, 256), BF16), sds((m, LANES), F32), sds((nb, C_HEADS, seq), F32)]
    return pl.pallas_call(
        functools.partial(_prep_prompt_kernel, steps_per_seq=sps),
        grid=(m // ts,),
        in_specs=_prep_in_specs(ts),
        out_specs=out_specs,
        out_shape=out_shape,
        compiler_params=_cp(("arbitrary",)),
        name="prep_prompt",
    )(*([u] * 8), *gains)


def _prep_sample(u, gains):
    m = u.shape[0]
    ts = _pick(m, (512, 256, 128, 64))
    sds = jax.ShapeDtypeStruct
    rows = lambda width: pl.BlockSpec((ts, width), lambda i: (i, 0))
    heads = pl.BlockSpec((ts, A_HEADS, HEAD_DIM), lambda i: (i, 0, 0))
    return pl.pallas_call(
        _prep_sample_kernel,
        grid=(m // ts,),
        in_specs=_prep_in_specs(ts),
        out_specs=[rows(512), rows(512), rows(512), heads, heads,
                   rows(512), rows(512), rows(512), heads, heads, rows(256), rows(LANES)],
        out_shape=[sds((m, 512), BF16), sds((m, 512), BF16), sds((m, 512), BF16),
                   sds((m, A_HEADS, HEAD_DIM), F32), sds((m, A_HEADS, HEAD_DIM), F32),
                   sds((m, 512), BF16), sds((m, 512), BF16), sds((m, 512), BF16),
                   sds((m, C_HEADS, HEAD_DIM), F32), sds((m, C_HEADS, HEAD_DIM), F32),
                   sds((m, 256), BF16), sds((m, LANES), F32)],
        compiler_params=_cp(("parallel",)),
        name="prep_sample",
    )(*([u] * 8), *gains)


def _memkv_kernel(kv, g, bd, o_kt, o_vt):
    o_kt[...] = (_head_norm(kv[:, :M_WIDTH], bd[...]) * g[...]).T
    o_vt[...] = kv[:, M_WIDTH:].T


def _memkv(kv, g_row, bd256, nb, n_mem):
    spec = pl.BlockSpec((None, M_WIDTH, n_mem), lambda b: (b, 0, 0))
    return pl.pallas_call(
        _memkv_kernel,
        grid=(nb,),
        in_specs=[pl.BlockSpec((n_mem, 2 * M_WIDTH), lambda b: (b, 0)),
                  pl.BlockSpec((1, M_WIDTH), lambda b: (0, 0)),
                  pl.BlockSpec((M_WIDTH, M_WIDTH), lambda b: (0, 0))],
        out_specs=[spec, spec],
        out_shape=[jax.ShapeDtypeStruct((nb, M_WIDTH, n_mem), F32)] * 2,
        compiler_params=_cp(("parallel",)),
        name="memkv",
    )(kv, g_row, bd256)


def _cum_kernel(lf_ref, tri_ref, selq_ref, selk_ref, qc_ref, kc_ref, blk_ref, low_ref, cq_ref, ck_ref, carry_ref,
                *, nchunk, unroll):
    totals = _sel_dot(blk_ref[...], lf_ref[...])
    carry_ref[...] = _sel_dot(low_ref[...], totals)
    tri = tri_ref[...]

    def body(cc, carry):
        cs = [cc * unroll + k for k in range(unroll)]
        rs = [pl.multiple_of(c * LANES, LANES) for c in cs]
        xs = [lf_ref[pl.ds(r, LANES), :] for r in rs]
        zs = [(_sel_dot(tri, x) + carry_ref[pl.ds(c, 1), :]) * LOG2E for x, c in zip(xs, cs)]
        parts = [_split3(z) for z in zs]
        qs = [_dot(p[0], selq_ref[0]) + _dot(p[1], selq_ref[1]) + _dot(p[2], selq_ref[2]) + qc_ref[...]
              for p in parts]
        ks = [_dot(p[0], selk_ref[0]) + _dot(p[1], selk_ref[1]) + _dot(p[2], selk_ref[2]) + kc_ref[...]
              for p in parts]
        for r, qv, kv in zip(rs, qs, ks):
            cq_ref[pl.ds(r, LANES), :] = qv.astype(BF16)
            ck_ref[pl.ds(r, LANES), :] = kv.astype(BF16)
        return carry

    lax.fori_loop(0, nchunk // unroll, body, 0)


CUM_UNROLLS = (8, 4, 3, 2, 1)


def _cum(lf, consts):
    nseq, length, _ = lf.shape
    assert length % LANES == 0
    nchunk = length // LANES
    unroll = _pick(nchunk, CUM_UNROLLS)
    ncp = -(-nchunk // SUBLANES) * SUBLANES
    tri, selq, selk, qc, kc = consts
    chunk_of = np.arange(length) // LANES
    blk = jnp.asarray((np.arange(ncp)[:, None] == chunk_of[None, :]).astype(np.float32), BF16)
    low = jnp.asarray(np.tril(np.ones((ncp, ncp), np.float32), -1), BF16)
    full = lambda shape: pl.BlockSpec(shape, lambda s: (0,) * len(shape))
    seq = pl.BlockSpec((None, length, LANES), lambda s: (s, 0, 0))
    return pl.pallas_call(
        functools.partial(_cum_kernel, nchunk=nchunk, unroll=unroll),
        grid=(nseq,),
        in_specs=[seq, full((LANES, LANES)), full((3, LANES, LANES)), full((3, LANES, LANES)),
                  full((1, LANES)), full((1, LANES)), full((ncp, length)), full((ncp, ncp))],
        out_specs=[seq, seq],
        out_shape=[jax.ShapeDtypeStruct(lf.shape, BF16)] * 2,
        scratch_shapes=[pltpu.VMEM((ncp, LANES), F32)],
        compiler_params=_cp(("parallel",)),
        name="cum",
    )(lf, tri, selq, selk, qc, kc, blk, low)


def _cum_consts():
    tri = np.tril(np.ones((LANES, LANES), np.float32))
    selq = np.zeros((3, LANES, LANES), np.float32)
    selk = np.zeros((3, LANES, LANES), np.float32)
    qc = np.zeros((1, LANES), np.float32)
    kc = np.zeros((1, LANES), np.float32)
    for h in range(C_HEADS):
        for p in range(3):
            selq[p, CF_LANE + h, AUG_PER_HEAD * h + p] = 1.0
            selk[p, CF_LANE + h, AUG_PER_HEAD * h + 3 + p] = -1.0
            qc[0, AUG_PER_HEAD * h + 3 + p] = 1.0
            kc[0, AUG_PER_HEAD * h + p] = 1.0
    return (jnp.asarray(tri, BF16), jnp.asarray(selq, BF16), jnp.asarray(selk, BF16),
            jnp.asarray(qc), jnp.asarray(kc))


BAND_Q = 4 * CHUNK
BAND_K = 3 * BAND_Q
TAB_PAD = 384
DIAG_PAD = 1024


def _bias_kernel(tab_ref, idx_ref, neg_ref, o_ref):
    nrows, ncols = neg_ref.shape
    onehot = (lax.broadcasted_iota(jnp.int32, (TAB_PAD, DIAG_PAD), 0) == idx_ref[...]).astype(BF16)
    diag = _dot_sel(tab_ref[...], onehot) * LOG2E
    for h in range(A_HEADS):
        rows = jnp.broadcast_to(diag[h:h + 1, :], (nrows, DIAG_PAD))
        o_ref[h] = pltpu.roll(rows, 0, 1, stride=1, stride_axis=0)[:, :ncols] + neg_ref[...]


def _band_bias(table, nq, nk, key_major):
    assert nq + nk <= DIAG_PAD
    r = np.arange(nq)[None, :] if key_major else np.arange(nq)[:, None]
    t = np.arange(nk)[:, None] if key_major else np.arange(nk)[None, :]
    cb = t // CHUNK - r // CHUNK
    neg = np.where((cb >= 0) & (cb <= A_LEFT_CHUNKS), 0.0, NEG).astype(np.float32)
    p = np.arange(DIAG_PAD)
    ncols = nq if key_major else nk
    col_minus_row = np.where(p < ncols, p, p - DIAG_PAD)
    q_minus_k = col_minus_row if key_major else -col_minus_row
    idx = (np.clip(A_WIN + q_minus_k, -A_REL_CLIP, A_REL_CLIP) + A_REL_CLIP).astype(np.int32).reshape(1, -1)
    tab = jnp.zeros((A_HEADS, TAB_PAD), F32).at[:, :table.shape[0]].set(table.T)
    full = lambda shape: pl.BlockSpec(shape, lambda: (0,) * len(shape))
    return pl.pallas_call(
        _bias_kernel,
        in_specs=[full((A_HEADS, TAB_PAD)), full((1, DIAG_PAD)), full(neg.shape)],
        out_specs=full((A_HEADS,) + neg.shape),
        out_shape=jax.ShapeDtypeStruct((A_HEADS,) + neg.shape, F32),
        compiler_params=pltpu.CompilerParams(vmem_limit_bytes=VMEM_LIMIT),
        name="band_bias",
    )(tab, jnp.asarray(idx), jnp.asarray(neg))


def _half_mask(par):
    lane = _lane_iota((1, LANES))
    return (lane < HEAD_DIM) if par == 0 else (lane >= HEAD_DIM)


def _softmax_pv_many(s_lists, v_lists, v_feature_major=None):
    n = len(s_lists)
    nblk = len(s_lists[0])
    if v_feature_major is None:
        v_feature_major = (False,) * nblk
    ms = []
    for sb in s_lists:
        m = sb[0].max(axis=1, keepdims=True)
        for s in sb[1:]:
            m = jnp.maximum(m, s.max(axis=1, keepdims=True))
        ms.append(m)
    ps = [[jnp.exp2(s - ms[i]) for s in s_lists[i]] for i in range(n)]
    ls = []
    for i in range(n):
        l = ps[i][0].sum(axis=1, keepdims=True)
        for p in ps[i][1:]:
            l = l + p.sum(axis=1, keepdims=True)
        ls.append(l)
    outs = []
    for i in range(n):
        acc = None
        for p, v, fm in zip(ps[i], v_lists[i], v_feature_major):
            pv = _dot_nt(p.astype(BF16), v) if fm else _dot(p.astype(BF16), v)
            acc = pv if acc is None else acc + pv
        outs.append(acc)
    return [o / l for o, l in zip(outs, ls)]


ONES_ROWS = 16


def _band_prompt_kernel(q_ref, k0, k1, k2, v0, v1, v2, bias_ref, z_ref, o_ref):
    g = pl.program_id(1)
    krefs = (k0, k1, k2)
    vrefs = (v0, v1, v2)
    s_lists = []
    for hp in range(A_HEADS // 2):
        cols = slice(hp * LANES, (hp + 1) * LANES)
        qp = q_ref[:, cols]
        ks = [kr[:, cols] for kr in krefs]
        for par in range(2):
            h = 2 * hp + par
            qm = jnp.where(_half_mask(par), qp, jnp.zeros_like(qp))
            sb = []
            for j in range(3):
                s = _dot_nt(ks[j], qm) + bias_ref[h, j * BAND_Q:(j + 1) * BAND_Q, :]
                if j < 2:
                    s = jnp.where(g + j >= 2, s, NEG)
                sb.append(s)
            s_lists.append(sb)
    ms = []
    for sb in s_lists:
        m = sb[0].max(axis=0, keepdims=True)
        for s in sb[1:]:
            m = jnp.maximum(m, s.max(axis=0, keepdims=True))
        ms.append(m)
    ps = [[jnp.exp2(s - m).astype(BF16) for s in sb] for sb, m in zip(s_lists, ms)]
    ones = jnp.ones((ONES_ROWS, BAND_Q), BF16)
    outs = []
    for h in range(A_HEADS):
        acc = None
        for j in range(3):
            vt = jnp.concatenate([vrefs[j][h * HEAD_DIM:(h + 1) * HEAD_DIM, :], ones], axis=0)
            d = _dot(vt, ps[h][j])
            acc = d if acc is None else acc + d
        outs.append(acc[:HEAD_DIM] / acc[HEAD_DIM:HEAD_DIM + 1])
    o_ref[...] = (jnp.concatenate(outs, axis=0).T * _silu(z_ref[...])).astype(BF16)


def _band_prompt(qs, kb, vtb, bias_t, u, nb, seq):
    assert VT_CHUNK == BAND_Q
    ng = seq // BAND_Q

    def kblk(j):
        return pl.BlockSpec((BAND_Q, A_WIDTH), lambda b, g: (b * ng + jnp.maximum(g - 2 + j, 0), 0))

    def vblk(j):
        return pl.BlockSpec((None, None, A_WIDTH, BAND_Q), lambda b, g: (b, jnp.maximum(g - 2 + j, 0), 0, 0))

    cur = pl.BlockSpec((BAND_Q, A_WIDTH), lambda b, g: (b * ng + g, 0))
    zidx = _U_OFF['a_z'] // A_WIDTH
    return pl.pallas_call(
        _band_prompt_kernel,
        grid=(nb, ng),
        in_specs=[cur, kblk(0), kblk(1), kblk(2), vblk(0), vblk(1), vblk(2),
                  pl.BlockSpec((A_HEADS, BAND_K, BAND_Q), lambda b, g: (0, 0, 0)),
                  pl.BlockSpec((BAND_Q, A_WIDTH), lambda b, g: (b * ng + g, zidx))],
        out_specs=cur,
        out_shape=jax.ShapeDtypeStruct((nb * seq, A_WIDTH), BF16),
        compiler_params=_cp(("parallel", "parallel")),
        name="band_prompt",
    )(qs, kb, kb, kb, vtb, vtb, vtb, bias_t, u)


def _band_sample_kernel(q_ref, kc_ref, kn_ref, vc_ref, vn_ref, bias_ref, z_ref, o_ref, *, t):
    lane = _lane_iota((1, LANES))
    lc = kc_ref.shape[1]
    pad = jnp.zeros((LANES - t, LANES), BF16)
    s_lists, v_lists = [], []
    for hp in range(A_HEADS // 2):
        cols = slice(hp * LANES, (hp + 1) * LANES)
        qp = q_ref[:, cols]
        kc = kc_ref[cols, :].astype(BF16)
        vc = vc_ref[cols, :].astype(BF16)
        kn = jnp.concatenate([kn_ref[:, cols], pad], axis=0)
        vn = jnp.concatenate([vn_ref[:, cols], pad], axis=0)
        for par in range(2):
            h = 2 * hp + par
            qm = jnp.where(_half_mask(par), qp, jnp.zeros_like(qp))
            s_lists.append([_dot(qm, kc) + bias_ref[h, :, :lc], _dot_nt(qm, kn) + bias_ref[h, :, lc:]])
            v_lists.append([vc, vn])
    outs = _softmax_pv_many(s_lists, v_lists, (True, False))
    for hp in range(A_HEADS // 2):
        cols = slice(hp * LANES, (hp + 1) * LANES)
        o = jnp.where(lane < HEAD_DIM, outs[2 * hp], outs[2 * hp + 1])
        o_ref[:, cols] = (o * _silu(z_ref[:, cols])).astype(BF16)


def _band_sample(qs, kb, vb, cache_kt, cache_vt, layer, bias_s, u, nseq, t):
    lc = cache_kt.shape[3]
    new = pl.BlockSpec((t, A_WIDTH), lambda s: (s, 0))
    cache = pl.BlockSpec((None, None, A_WIDTH, lc), lambda s: (layer, s, 0, 0))
    return pl.pallas_call(
        functools.partial(_band_sample_kernel, t=t),
        grid=(nseq,),
        in_specs=[new, cache, new, cache, new,
                  pl.BlockSpec(bias_s.shape, lambda s: (0, 0, 0)),
                  pl.BlockSpec((t, A_WIDTH), lambda s: (s, _U_OFF['a_z'] // A_WIDTH))],
        out_specs=new,
        out_shape=jax.ShapeDtypeStruct((nseq * t, A_WIDTH), BF16),
        compiler_params=_cp(("parallel",)),
        name="band_sample",
    )(qs, cache_kt, kb, cache_vt, vb, bias_s, u)


FOX_TQ = 256
FOX_TK = 512


def _aug_mask(h):
    lane = _lane_iota((1, LANES))
    return (lane >= AUG_PER_HEAD * h) & (lane < AUG_PER_HEAD * (h + 1))


def _fox_prompt_kernel(q_ref, cq_ref, k_ref, ck_ref, vt_ref, z_ref, o_ref, st_ref, m_ref, acc_ref):
    hp = pl.program_id(1)
    i = pl.program_id(2)
    qp = q_ref[...]
    cq = cq_ref[...]
    qcs = []
    for par in range(2):
        qm = jnp.where(_half_mask(par), qp, jnp.zeros_like(qp))
        qa = jnp.where(_aug_mask(2 * hp + par), cq, jnp.zeros_like(cq))
        qcs.append(jnp.concatenate([qm, qa], axis=1))
    qcat = jnp.concatenate(qcs, axis=0)
    m_ref[...] = jnp.full(m_ref.shape, NEG, F32)
    acc_ref[...] = jnp.zeros(acc_ref.shape, F32)
    ratio = FOX_TK // FOX_TQ
    ones = jnp.ones((ONES_ROWS, FOX_TQ), BF16)

    def scores(j, slot):
        off = pl.multiple_of(j * FOX_TK, FOX_TK)
        kc = jnp.concatenate([k_ref[pl.ds(off, FOX_TK), :], ck_ref[pl.ds(off, FOX_TK), :]], axis=1)
        st_ref[slot] = _dot_nt(kc, qcat)

    def reduce(j, slot, masked):
        st = st_ref[slot]
        if masked:
            shape = (FOX_TK, 2 * FOX_TQ)
            qpos = (_lane_iota(shape) & (FOX_TQ - 1)) + (i % ratio) * FOX_TQ
            st = jnp.where(_row_iota(shape) <= qpos, st, NEG)
        m_old = m_ref[...]
        m_new = jnp.maximum(m_old, st.max(axis=0, keepdims=True))
        alpha = jnp.exp2(m_old - m_new)
        pb = jnp.exp2(st - m_new).astype(BF16)
        for par in range(2):
            cols = slice(par * FOX_TQ, (par + 1) * FOX_TQ)
            pv = None
            for c in range(ratio):
                vt = jnp.concatenate([vt_ref[j * ratio + c, par * HEAD_DIM:(par + 1) * HEAD_DIM, :], ones], axis=0)
                d = _dot(vt, pb[c * FOX_TQ:(c + 1) * FOX_TQ, cols])
                pv = d if pv is None else pv + d
            acc_ref[par] = alpha[:, cols] * acc_ref[par] + pv
        m_ref[...] = m_new

    nfull = i // ratio
    scores(0, 0)

    def body(jj, carry):
        j = 2 * jj
        scores(j + 1, 1)
        reduce(j, 0, False)
        scores(j + 2, 0)
        reduce(j + 1, 1, False)
        return carry

    lax.fori_loop(0, nfull // 2, body, 0)

    @pl.when(nfull % 2 == 0)
    def _():
        reduce(nfull, 0, True)

    @pl.when(nfull % 2 == 1)
    def _():
        scores(nfull, 1)
        reduce(nfull - 1, 0, False)
        reduce(nfull, 1, True)

    ot = jnp.concatenate([acc_ref[par, :HEAD_DIM] / acc_ref[par, HEAD_DIM:HEAD_DIM + 1] for par in range(2)],
                         axis=0)
    o_ref[...] = (ot.T * _silu(z_ref[...])).astype(BF16)


def _fox_prompt(qs, kb, vtb, cq, ck, u, nb, seq):
    assert VT_CHUNK == FOX_TQ
    nq = seq // FOX_TQ
    qspec = pl.BlockSpec((FOX_TQ, LANES), lambda b, hp, i: (b * nq + i, hp))
    cqspec = pl.BlockSpec((FOX_TQ, LANES), lambda b, hp, i: (b * nq + i, 0))
    kspec = pl.BlockSpec((seq, LANES), lambda b, hp, i: (b, hp))
    ckspec = pl.BlockSpec((seq, LANES), lambda b, hp, i: (b, 0))
    vspec = pl.BlockSpec((None, nq, LANES, FOX_TQ), lambda b, hp, i: (b, 0, hp, 0))
    z0 = _U_OFF['c_z'] // LANES
    zspec = pl.BlockSpec((FOX_TQ, LANES), lambda b, hp, i: (b * nq + i, z0 + hp))
    return pl.pallas_call(
        _fox_prompt_kernel,
        grid=(nb, C_HEADS // 2, nq),
        in_specs=[qspec, cqspec, kspec, ckspec, vspec, zspec],
        out_specs=qspec,
        out_shape=jax.ShapeDtypeStruct((nb * seq, C_WIDTH), BF16),
        scratch_shapes=[pltpu.VMEM((2, FOX_TK, 2 * FOX_TQ), F32),
                        pltpu.VMEM((1, 2 * FOX_TQ), F32),
                        pltpu.VMEM((2, HEAD_DIM + ONES_ROWS, FOX_TQ), F32)],
        compiler_params=_cp(("parallel", "parallel", "arbitrary")),
        name="fox_prompt",
    )(qs, cq, kb, ck, vtb, u)


def _fox_sample_kernel(q_ref, cq_ref, kc_ref, kn_ref, ck_ref, vc_ref, vn_ref, z_ref, o_ref, *, t, past):
    lane = _lane_iota((1, LANES))
    cq = cq_ref[past:past + t, :]
    pad = jnp.zeros((LANES - t, LANES), BF16)
    ck_cache_t = ck_ref[:past, :].astype(F32).T.astype(BF16)
    ck_new = ck_ref[past:, :]
    vis = _lane_iota((t, LANES)) <= _row_iota((t, LANES))
    s_lists, v_lists = [], []
    for hp in range(C_HEADS // 2):
        cols = slice(hp * LANES, (hp + 1) * LANES)
        qp = q_ref[:, cols]
        kc = jnp.concatenate([kc_ref[cols, :].astype(BF16), ck_cache_t], axis=0)
        kn = jnp.concatenate([jnp.concatenate([kn_ref[:, cols], pad], axis=0), ck_new], axis=1)
        vc = vc_ref[cols, :].astype(BF16)
        vn = jnp.concatenate([vn_ref[:, cols], pad], axis=0)
        for par in range(2):
            qm = jnp.where(_half_mask(par), qp, jnp.zeros_like(qp))
            qa = jnp.where(_aug_mask(2 * hp + par), cq, jnp.zeros_like(cq))
            qc = jnp.concatenate([qm, qa], axis=1)
            s_lists.append([_dot(qc, kc), jnp.where(vis, _dot_nt(qc, kn), NEG)])
            v_lists.append([vc, vn])
    outs = _softmax_pv_many(s_lists, v_lists, (True, False))
    for hp in range(C_HEADS // 2):
        cols = slice(hp * LANES, (hp + 1) * LANES)
        o = jnp.where(lane < HEAD_DIM, outs[2 * hp], outs[2 * hp + 1])
        o_ref[:, cols] = (o * _silu(z_ref[:, cols])).astype(BF16)


def _fox_sample(qs, kb, vb, cq, ck, cache_kt, cache_vt, layer, u, nseq, t):
    past = cache_kt.shape[3]
    new = pl.BlockSpec((t, C_WIDTH), lambda s: (s, 0))
    cache = pl.BlockSpec((None, None, C_WIDTH, past), lambda s: (layer, s, 0, 0))
    aug = pl.BlockSpec((None, past + LANES, LANES), lambda s: (s, 0, 0))
    return pl.pallas_call(
        functools.partial(_fox_sample_kernel, t=t, past=past),
        grid=(nseq,),
        in_specs=[new, aug, cache, new, aug, cache, new,
                  pl.BlockSpec((t, C_WIDTH), lambda s: (s, _U_OFF['c_z'] // C_WIDTH))],
        out_specs=new,
        out_shape=jax.ShapeDtypeStruct((nseq * t, C_WIDTH), BF16),
        compiler_params=_cp(("parallel",)),
        name="fox_sample",
    )(qs, cq, cache_kt, kb, ck, cache_vt, vb, u)


def _cross_kernel(q_ref, k_ref, v_ref, z_ref, o_ref):
    lane = _lane_iota((1, LANES))
    s_lists, v_lists = [], []
    for hp in range(M_HEADS // 2):
        cols = slice(hp * LANES, (hp + 1) * LANES)
        qp = q_ref[:, cols]
        k = k_ref[cols, :].astype(BF16)
        v = v_ref[cols, :].astype(BF16)
        for par in range(2):
            qm = jnp.where(_half_mask(par), qp, jnp.zeros_like(qp))
            s_lists.append([_dot(qm, k)])
            v_lists.append([v])
    outs = _softmax_pv_many(s_lists, v_lists, (True,))
    for hp in range(M_HEADS // 2):
        cols = slice(hp * LANES, (hp + 1) * LANES)
        o = jnp.where(lane < HEAD_DIM, outs[2 * hp], outs[2 * hp + 1])
        o_ref[:, cols] = (o * _silu(z_ref[:, cols])).astype(BF16)


def _cross(qs, mkt, mvt, layer, u, rows_per_seq, name):
    nrows = qs.shape[0]
    n_mem = mkt.shape[3]
    tq = _pick(rows_per_seq, (512, 256, 128, 64))
    per = rows_per_seq // tq
    mem = pl.BlockSpec((None, None, M_WIDTH, n_mem), lambda i: (layer, i // per, 0, 0))
    rows = pl.BlockSpec((tq, M_WIDTH), lambda i: (i, 0))
    return pl.pallas_call(
        _cross_kernel,
        grid=(nrows // tq,),
        in_specs=[rows, mem, mem, pl.BlockSpec((tq, M_WIDTH), lambda i: (i, _U_OFF['m_z'] // M_WIDTH))],
        out_specs=rows,
        out_shape=jax.ShapeDtypeStruct((nrows, M_WIDTH), BF16),
        compiler_params=_cp(("parallel",)),
        name=name,
    )(qs, mkt, mvt, u)


SSD_Q = 128
CONV_PAD = SUBLANES
SSD_GROUP = 4


def _ssd_kernel(xbc_ref, z_ref, sm_ref, cs0_ref, hs0_ref, cw_ref, cb_ref, dtb_ref, alog_ref, dsk_ref,
                gn_ref, exp_ref, tri_ref, qm_ref,
                o_ref, hs_out_ref, cs_out_ref, hs_ref, xext_ref, *, nv, nchunk):
    c = pl.program_id(1)
    q = SSD_Q
    grp_ids = range(SSD_GROUP)
    lane = _lane_iota((1, LANES))
    head_lane = (lane >= DT_LANE) & (lane < DT_LANE + B_HEADS)
    a_row = jnp.where(head_lane, -jnp.exp(alog_ref[...]), 0.0)
    causal = _lane_iota((q, q)) <= _row_iota((q, q))
    heads_per_group = B_HEADS // B_GROUPS
    expand = exp_ref[...]
    tri = tri_ref[...]
    qm = qm_ref[...]

    @pl.when(c == 0)
    def _():
        for g in grp_ids:
            hs_ref[g] = hs0_ref[g].T * qm
            xext_ref[g, 0:CONV_PAD, :] = cs0_ref[g]

    for g in grp_ids:
        xext_ref[g, CONV_PAD:CONV_PAD + nv, :] = xbc_ref[g]
        if nv < q:
            xext_ref[g, CONV_PAD + nv:CONV_PAD + q, :] = jnp.zeros((q - nv, B_XBC), F32)
    xa = []
    for g in grp_ids:
        y = cb_ref[...]
        for tap in range(B_CONV):
            y = y + xext_ref[g, pl.ds(CONV_PAD - (B_CONV - 1) + tap, q), :] * cw_ref[tap:tap + 1, :]
        xa.append(_silu(y))
    for g in grp_ids:
        tail = xext_ref[g, nv:nv + CONV_PAD, :]
        xext_ref[g, 0:CONV_PAD, :] = tail
        cs_out_ref[g] = tail
    xs = [x[:, :B_WIDTH] for x in xa]
    bm = [x[:, B_WIDTH:B_WIDTH + LANES].astype(BF16) for x in xa]
    cm = [x[:, B_WIDTH + LANES:] for x in xa]

    dt = []
    for g in grp_ids:
        d = _softplus(sm_ref[g] + dtb_ref[...])
        if nv < q:
            d = jnp.concatenate([d, jnp.zeros((q - nv, LANES), F32)], axis=0)
        dt.append(jnp.where(head_lane, d, 0.0))
    acum = [_sel_dot(tri, d * a_row) for d in dt]
    dt_e = [_dot_sel(d, expand) for d in dt]
    acum_e = [_dot_sel(a, expand) for a in acum]
    alast_e = [a[q - 1:q, :] for a in acum_e]
    xdt = [x * d for x, d in zip(xs, dt_e)]

    hs = [hs_ref[g] for g in grp_ids]
    y_off = [_dot(cm[g].astype(BF16), hs[g].astype(BF16)) * jnp.exp(acum_e[g]) for g in grp_ids]
    upd = [_dot_tn(bm[g], (xdt[g] * jnp.exp(alast_e[g] - acum_e[g])).astype(BF16)) for g in grp_ids]
    hs_new = [(jnp.exp(alast_e[g]) * hs[g] + upd[g]) * qm for g in grp_ids]
    for g in grp_ids:
        hs_ref[g] = hs_new[g]

    @pl.when(c == nchunk - 1)
    def _():
        for g in grp_ids:
            hs_out_ref[g] = hs_new[g].T

    acum_t = [a.T for a in acum]
    xdt_b = [x.astype(BF16) for x in xdt]
    cbm = [[None] * B_GROUPS for _ in grp_ids]
    for grp in range(B_GROUPS):
        for g in grp_ids:
            cg = jnp.where(_half_mask(grp), cm[g], 0.0).astype(BF16)
            cbm[g][grp] = _dot_nt(cg, bm[g])
    y_pairs = [[] for _ in grp_ids]
    for hp in range(B_HEADS // 2):
        pair = [[] for _ in grp_ids]
        for par in range(2):
            h = 2 * hp + par
            for g in grp_ids:
                seg = acum[g][:, DT_LANE + h:DT_LANE + h + 1] - acum_t[g][DT_LANE + h:DT_LANE + h + 1, :]
                lmat = jnp.exp(jnp.where(causal, seg, -jnp.inf))
                m = (cbm[g][h // heads_per_group] * lmat).astype(BF16)
                pair[g].append(_dot(m, xdt_b[g][:, hp * LANES:(hp + 1) * LANES]))
        for g in grp_ids:
            y_pairs[g].append(jnp.where(lane < HEAD_DIM, pair[g][0], pair[g][1]))

    for g in grp_ids:
        yt = y_off[g] + jnp.concatenate(y_pairs[g], axis=1) + dsk_ref[...] * xs[g]
        if nv < q:
            yt = yt[:nv]
        yz = yt * _silu(z_ref[g])
        ms = jnp.mean(yz * yz, axis=-1, keepdims=True)
        o_ref[g] = (yz * lax.rsqrt(ms + EPS) * gn_ref[...]).astype(BF16)


def _ssd(u, cs0, hs0, consts, nseq, rows_per_seq):
    nv = min(SSD_Q, rows_per_seq)
    nchunk = rows_per_seq // nv
    gsz = SSD_GROUP
    assert nseq % gsz == 0
    cw, cb, dtb, alog, dsk, gn, expand, tri, qmask = consts
    u3 = u.reshape(nseq, rows_per_seq, U_WIDTH)

    def ucol(name, width):
        idx = _U_OFF[name] // width
        assert idx * width == _U_OFF[name]
        return pl.BlockSpec((gsz, nv, width), lambda s, c: (s, c, idx))

    full = lambda a: pl.BlockSpec(a.shape, lambda s, c: (0,) * a.ndim)
    cs_spec = pl.BlockSpec((gsz, CONV_PAD, B_XBC), lambda s, c: (s, 0, 0))
    hs_spec = pl.BlockSpec((gsz, B_WIDTH, LANES), lambda s, c: (s, 0, 0))
    o, hs, cs = pl.pallas_call(
        functools.partial(_ssd_kernel, nv=nv, nchunk=nchunk),
        grid=(nseq // gsz, nchunk),
        in_specs=[ucol('b_xbc', B_XBC), ucol('b_z', B_WIDTH), ucol('b_dt', LANES), cs_spec, hs_spec,
                  full(cw), full(cb), full(dtb), full(alog), full(dsk), full(gn), full(expand), full(tri),
                  full(qmask)],
        out_specs=[pl.BlockSpec((gsz, nv, B_WIDTH), lambda s, c: (s, c, 0)), hs_spec, cs_spec],
        out_shape=[jax.ShapeDtypeStruct((nseq, rows_per_seq, B_WIDTH), BF16),
                   jax.ShapeDtypeStruct((nseq, B_WIDTH, LANES), F32),
                   jax.ShapeDtypeStruct((nseq, CONV_PAD, B_XBC), F32)],
        scratch_shapes=[pltpu.VMEM((gsz, LANES, B_WIDTH), F32),
                        pltpu.VMEM((gsz, SSD_Q + CONV_PAD, B_XBC), F32)],
        compiler_params=_cp(("parallel", "arbitrary")),
        name="ssd",
    )(u3, u3, u3, cs0, hs0, cw, cb, dtb, alog, dsk, gn, expand, tri, qmask)
    return o.reshape(nseq * rows_per_seq, B_WIDTH), hs, cs


def _ssd_static_consts():
    expand = np.zeros((LANES, B_WIDTH), np.float32)
    for h in range(B_HEADS):
        expand[DT_LANE + h, h * HEAD_DIM:(h + 1) * HEAD_DIM] = 1.0
    tri = np.tril(np.ones((SSD_Q, SSD_Q), np.float32))
    qmask = np.zeros((LANES, B_WIDTH), np.float32)
    half = B_WIDTH // B_GROUPS
    for g in range(B_GROUPS):
        qmask[g * B_STATE:(g + 1) * B_STATE, g * half:(g + 1) * half] = 1.0
    return jnp.asarray(expand, BF16), jnp.asarray(tri, BF16), jnp.asarray(qmask)


def _stack_state(h):
    r = h.reshape(h.shape[0], B_WIDTH, B_STATE)
    return jnp.concatenate([r, r], axis=2)


def _unstack_state(hst):
    n = hst.shape[0]
    first = (np.arange(B_WIDTH) < B_WIDTH // B_GROUPS)[None, :, None]
    return jnp.where(first, hst[:, :, :B_STATE], hst[:, :, B_STATE:]).reshape(n, B_HEADS, HEAD_DIM, B_STATE)


def _merge_kernel(x, ya, yb, yc, ym, gt, wpa, wpb, wpc, wpm, wout, o):
    d = x.shape[1]
    pa = _dot(ya[...], wpa[...])
    pb = _dot(yb[...], wpb[...])
    pc = _dot(yc[...], wpc[...])
    pm = _dot(ym[...], wpm[...])
    mix = (gt[:, 0:d].astype(F32) * pa + gt[:, d:2 * d].astype(F32) * pb
           + gt[:, 2 * d:3 * d].astype(F32) * pc + gt[:, 3 * d:4 * d].astype(F32) * pm)
    o[...] = x[...] + _dot(mix.astype(BF16), wout[...])


def _merge(x, gates, ya, yb, yc, ym, wpa, wpb, wpc, wpm, wout):
    m, d = x.shape
    tm = _pick(m, (1024, 512, 256, 128))

    def rows(width):
        return pl.BlockSpec((tm, width), lambda i: (i, 0))

    full = lambda a: pl.BlockSpec(a.shape, lambda i: (0, 0))
    return pl.pallas_call(
        _merge_kernel,
        grid=(m // tm,),
        in_specs=[rows(d), rows(A_WIDTH), rows(B_WIDTH), rows(C_WIDTH), rows(M_WIDTH), rows(N_BRANCH * d),
                  full(wpa), full(wpb), full(wpc), full(wpm), full(wout)],
        out_specs=rows(d),
        out_shape=jax.ShapeDtypeStruct((m, d), F32),
        compiler_params=_cp(("parallel",)),
        name="merge",
    )(x, ya, yb, yc, ym, gates, wpa, wpb, wpc, wpm, wout)


def _row(v):
    return v.reshape(1, -1).astype(F32)


def _lane_row(v, lane0):
    return jnp.zeros((1, LANES), F32).at[0, lane0:lane0 + v.shape[0]].set(v.astype(F32))


def _block_diag_mean(width):
    seg = np.arange(width) // HEAD_DIM
    return jnp.asarray((seg[:, None] == seg[None, :]).astype(np.float32) / HEAD_DIM, BF16)


def kernel(x_prompt, x_sample, mem_prompt, cache_a_k, cache_a_v, cache_c_k, cache_c_v, cache_c_logf, state_b_ssm, state_b_conv, cache_mem_k, cache_mem_v, g_norm, w_in, a_qnorm, a_knorm, a_rel, b_conv_w, b_conv_b, b_dt_bias, b_a_log, b_d, b_norm, c_qnorm, c_knorm, c_fbias, m_norm, w_mkv, m_qnorm, m_knorm, w_pa, w_pb, w_pc, w_pm, w_out):
    nb, seq, d = x_prompt.shape
    ns, tdec, _ = x_sample.shape
    depth = g_norm.shape[0]
    n_mem = mem_prompt.shape[1]
    past = cache_c_k.shape[2]
    la_cache = cache_a_k.shape[2]
    mp = nb * seq
    md = ns * tdec
    assert seq % FOX_TK == 0 and seq % BAND_Q == 0 and seq % SSD_Q == 0
    assert tdec == CHUNK and la_cache == A_WIN and past % LANES == 0

    xp = x_prompt.reshape(mp, d)
    xs = x_sample.reshape(md, d)
    mem = mem_prompt.reshape(nb * n_mem, d)

    def feature_major(c):
        dd, n, t, h, e = c.shape
        return jnp.transpose(c, (0, 1, 3, 4, 2)).reshape(dd, n, h * e, t)

    cak, cav = feature_major(cache_a_k), feature_major(cache_a_v)
    cck, ccv = feature_major(cache_c_k), feature_major(cache_c_v)
    cmk, cmv = feature_major(cache_mem_k), feature_major(cache_mem_v)

    bd256 = _block_diag_mean(MXU_WIDTH)
    cum_consts = _cum_consts()
    expand, tri_q, qmask = _ssd_static_consts()
    scale = HEAD_DIM ** -0.5 * LOG2E

    outs = {k: [] for k in ('pa_k', 'pa_v', 'pc_k', 'pc_v', 'pc_f', 'pb_s', 'pb_c', 'pm_k', 'pm_v',
                            'sa_k', 'sa_v', 'sc_k', 'sc_v', 'sc_f', 'sb_s', 'sb_c')}

    for l in range(depth):
        wt = jnp.transpose(w_in[l])
        pieces = [wt[_SRC[n][0]:_SRC[n][0] + _SRC[n][1]] for n in _U_ORDER]
        used = sum(p.shape[0] for p in pieces)
        w_u = jnp.concatenate(pieces + [jnp.zeros((U_WIDTH - used, d), F32)], axis=0).astype(BF16)
        g0, gw = _SRC['gate']
        w_g = wt[g0:g0 + gw].astype(BF16)
        g_row = _row(g_norm[l])
        gains = (_row(jnp.tile(a_qnorm[l], A_HEADS)) * scale, _row(jnp.tile(a_knorm[l], A_HEADS)),
                 _row(jnp.tile(c_qnorm[l], C_HEADS)) * scale, _row(jnp.tile(c_knorm[l], C_HEADS)),
                 _row(jnp.tile(m_qnorm[l], M_HEADS)) * scale, _lane_row(c_fbias[l], CF_LANE), bd256)

        u_p = _proj(xp, g_row, w_u, act=None, out_dtype=F32, name="proj_u", w_rows=True)
        gates_p = _proj(xp, g_row, w_g, act='sigmoid', out_dtype=BF16, name="proj_gate", w_rows=True)
        (a_qs_p, a_kb_p, a_vtb_p, pa_kt, pa_vt, c_qs_p, c_kb_p, pc_kt, pc_vt, c_vtb_p, m_qs_p, lf_p,
         pc_ft) = _prep_prompt(u_p, gains, nb, seq)
        u_s = _proj(xs, g_row, w_u, act=None, out_dtype=F32, name="proj_u", w_rows=True)
        gates_s = _proj(xs, g_row, w_g, act='sigmoid', out_dtype=BF16, name="proj_gate", w_rows=True)
        (a_qs_s, a_kb_s, a_vb_s, sa_k4, sa_v4, c_qs_s, c_kb_s, c_vb_s, sc_k4, sc_v4, m_qs_s,
         lf_s) = _prep_sample(u_s, gains)

        kv = _proj(mem, _row(m_norm[l]), w_mkv[l].astype(BF16), act=None, out_dtype=F32, name="proj_mem",
                   w_rows=False)
        mkt, mvt = _memkv(kv, _row(jnp.tile(m_knorm[l], M_HEADS)), bd256, nb, n_mem)

        o_a_p = _band_prompt(a_qs_p, a_kb_p, a_vtb_p, _band_bias(a_rel[l], BAND_Q, BAND_K, True), u_p, nb, seq)
        bias_s = _band_bias(a_rel[l], tdec, la_cache + LANES, False)
        o_a_s = _band_sample(a_qs_s, a_kb_s, a_vb_s, cak, cav, l, bias_s, u_s, ns, tdec)

        cq_p, ck_p = _cum(lf_p.reshape(nb, seq, LANES), cum_consts)
        o_c_p = _fox_prompt(c_qs_p, c_kb_p, c_vtb_p, cq_p.reshape(mp, LANES), ck_p.reshape(mp, LANES), u_p, nb, seq)
        lf_cache = jnp.pad(cache_c_logf[l].astype(F32), ((0, 0), (0, 0), (CF_LANE, LANES - CF_LANE - C_HEADS)))
        lf_cat = jnp.concatenate([lf_cache, lf_s.reshape(ns, tdec, LANES),
                                  jnp.zeros((ns, LANES - tdec, LANES), F32)], axis=1)
        cq_s, ck_s = _cum(lf_cat, cum_consts)
        o_c_s = _fox_sample(c_qs_s, c_kb_s, c_vb_s, cq_s, ck_s, cck, ccv, l, u_s, ns, tdec)

        ssd_consts = (jnp.pad(b_conv_w[l], ((0, SUBLANES - B_CONV), (0, 0))), _row(b_conv_b[l]),
                      _lane_row(b_dt_bias[l], DT_LANE), _lane_row(b_a_log[l], DT_LANE),
                      _row(jnp.repeat(b_d[l], HEAD_DIM)), _row(b_norm[l]), expand, tri_q, qmask)
        o_b_p, hs_p, cs_p = _ssd(u_p, jnp.zeros((nb, CONV_PAD, B_XBC), F32),
                                 jnp.zeros((nb, B_WIDTH, LANES), F32), ssd_consts, nb, seq)
        cs0 = jnp.pad(state_b_conv[l].astype(F32), ((0, 0), (CONV_PAD - (B_CONV - 1), 0), (0, 0)))
        o_b_s, hs_s, cs_s = _ssd(u_s, cs0, _stack_state(state_b_ssm[l].astype(F32)), ssd_consts, ns, tdec)

        o_m_p = _cross(m_qs_p, mkt[None], mvt[None], 0, u_p, seq, "cross_prompt")
        o_m_s = _cross(m_qs_s, cmk, cmv, l, u_s, tdec, "cross_sample")

        w_merge = (w_pa[l].astype(BF16), w_pb[l].astype(BF16), w_pc[l].astype(BF16), w_pm[l].astype(BF16),
                   w_out[l].astype(BF16))
        xp = _merge(xp, gates_p, o_a_p, o_b_p, o_c_p, o_m_p, *w_merge)
        xs = _merge(xs, gates_s, o_a_s, o_b_s, o_c_s, o_m_s, *w_merge)

        outs['pa_k'].append(pa_kt)
        outs['pa_v'].append(pa_vt)
        outs['pc_k'].append(pc_kt)
        outs['pc_v'].append(pc_vt)
        outs['pc_f'].append(pc_ft)
        outs['pb_s'].append(_unstack_state(hs_p))
        outs['pb_c'].append(cs_p[:, CONV_PAD - (B_CONV - 1):])
        outs['pm_k'].append(mkt)
        outs['pm_v'].append(mvt)
        outs['sa_k'].append(sa_k4)
        outs['sa_v'].append(sa_v4)
        outs['sc_k'].append(sc_k4)
        outs['sc_v'].append(sc_v4)
        outs['sc_f'].append(lf_s[:, CF_LANE:CF_LANE + C_HEADS].reshape(ns, tdec, C_HEADS))
        outs['sb_s'].append(_unstack_state(hs_s))
        outs['sb_c'].append(cs_s[:, CONV_PAD - (B_CONV - 1):])

    st = jnp.stack

    def token_major(name, heads):
        a = st(outs[name])
        dd, n, _, t = a.shape
        return jnp.transpose(a.reshape(dd, n, heads, HEAD_DIM, t), (0, 1, 4, 2, 3))

    def sample_heads(name):
        return st(outs[name]).reshape(depth, ns, tdec, A_HEADS, HEAD_DIM)

    return (xp.reshape(nb, seq, d), xs.reshape(ns, tdec, d),
            token_major('pa_k', A_HEADS), token_major('pa_v', A_HEADS),
            token_major('pc_k', C_HEADS), token_major('pc_v', C_HEADS),
            jnp.transpose(st(outs['pc_f']), (0, 1, 3, 2)),
            st(outs['pb_s']), st(outs['pb_c']), token_major('pm_k', M_HEADS), token_major('pm_v', M_HEADS),
            sample_heads('sa_k'), sample_heads('sa_v'), sample_heads('sc_k'), sample_heads('sc_v'),
            st(outs['sc_f']), st(outs['sb_s']), st(outs['sb_c']))
```

```python
import functools

import numpy as np
import jax
import jax.numpy as jnp
from jax import lax
from jax.experimental import pallas as pl
from jax.experimental.pallas import tpu as pltpu

F32 = jnp.float32
BF16 = jnp.bfloat16

EPS = 1e-6
NEG = -1e30
LOG2E = 1.4426950408889634

HEAD_DIM = 64
CHUNK = 64
A_HEADS = 8
A_WIDTH = 512
A_LEFT_CHUNKS = 8
A_WIN = A_LEFT_CHUNKS * CHUNK
A_REL_CLIP = 128
B_HEADS = 8
B_WIDTH = 512
B_GROUPS = 2
B_STATE = 64
B_CONV = 4
B_XBC = B_WIDTH + 2 * B_GROUPS * B_STATE
C_HEADS = 8
C_WIDTH = 512
M_HEADS = 4
M_WIDTH = 256
N_BRANCH = 4

LANES = 128
SUBLANES = 8
MXU_WIDTH = 256
VMEM_LIMIT = 48 * 1024 * 1024

_SPLITS = (('a_q', A_WIDTH), ('a_k', A_WIDTH), ('a_v', A_WIDTH), ('a_z', A_WIDTH),
           ('b_z', B_WIDTH), ('b_xbc', B_XBC), ('b_dt', B_HEADS),
           ('c_q', C_WIDTH), ('c_k', C_WIDTH), ('c_v', C_WIDTH), ('c_f', C_HEADS), ('c_z', C_WIDTH),
           ('m_q', M_WIDTH), ('m_z', M_WIDTH), ('gate', N_BRANCH * 1024))
_SRC = {}
_off = 0
for _n, _w in _SPLITS:
    _SRC[_n] = (_off, _w)
    _off += _w

_U_ORDER = ('a_q', 'a_k', 'a_v', 'a_z', 'c_q', 'c_k', 'c_v', 'c_z', 'b_z', 'b_xbc', 'm_q', 'm_z', 'b_dt', 'c_f')
_U_OFF = {}
_off = 0
for _n in _U_ORDER:
    _U_OFF[_n] = _off
    _off += _SRC[_n][1]
U_WIDTH = 6144
DT_LANE = 0
CF_LANE = B_HEADS
AUG_PER_HEAD = 6


def _cp(sem):
    return pltpu.CompilerParams(dimension_semantics=sem, vmem_limit_bytes=VMEM_LIMIT)


def _pick(n, cands):
    for c in cands:
        if n % c == 0:
            return c
    raise ValueError(f"no tile for {n} in {cands}")


def _dot(a, b):
    return jnp.dot(a, b, preferred_element_type=F32)


def _dot_nt(a, b):
    return lax.dot_general(a, b, (((1,), (1,)), ((), ())), preferred_element_type=F32)


def _dot_tn(a, b):
    return lax.dot_general(a, b, (((0,), (0,)), ((), ())), preferred_element_type=F32)


def _split3(x):
    hi = x.astype(BF16)
    r = x - hi.astype(F32)
    mid = r.astype(BF16)
    lo = (r - mid.astype(F32)).astype(BF16)
    return hi, mid, lo


def _dot_sel(x, sel):
    hi, mid, lo = _split3(x)
    return _dot(hi, sel) + _dot(mid, sel) + _dot(lo, sel)


def _sel_dot(sel, x):
    hi, mid, lo = _split3(x)
    return _dot(sel, hi) + _dot(sel, mid) + _dot(sel, lo)


def _sigmoid(x):
    return 1.0 / (1.0 + jnp.exp(-x))


def _silu(x):
    return x * _sigmoid(x)


def _softplus(x):
    return jnp.maximum(x, 0.0) + jnp.log1p(jnp.exp(-jnp.abs(x)))


def _log_sigmoid(x):
    return jnp.minimum(x, 0.0) - jnp.log1p(jnp.exp(-jnp.abs(x)))


def _head_norm(x, bd):
    x2 = x * x
    hi = x2.astype(BF16)
    lo = (x2 - hi.astype(F32)).astype(BF16)
    w = bd.shape[0]
    ms = [_dot(hi[:, c:c + w], bd) + _dot(lo[:, c:c + w], bd) for c in range(0, x.shape[1], w)]
    ms = ms[0] if len(ms) == 1 else jnp.concatenate(ms, axis=1)
    return x * lax.rsqrt(ms + EPS)


def _lane_iota(shape):
    return lax.broadcasted_iota(jnp.int32, shape, len(shape) - 1)


def _row_iota(shape):
    return lax.broadcasted_iota(jnp.int32, shape, len(shape) - 2)


def _proj_kernel(x_ref, g_ref, w_ref, o_ref, *rest, act, w_rows, side):
    h_ref = rest[-1]

    @pl.when(pl.program_id(1) == 0)
    def _():
        x = x_ref[...]
        ms = jnp.mean(x * x, axis=-1, keepdims=True)
        h_ref[...] = (x * lax.rsqrt(ms + EPS) * g_ref[...]).astype(BF16)

    u = _dot_nt(h_ref[...], w_ref[...]) if w_rows else _dot(h_ref[...], w_ref[...])
    if side is not None:
        tile, col = side

        @pl.when(pl.program_id(1) == tile)
        def _():
            rest[0][...] = u[:, col:col + LANES]
    if act == 'sigmoid':
        u = _sigmoid(u)
    o_ref[...] = u.astype(o_ref.dtype)


def _proj(x, g_row, w_bf, *, act, out_dtype, name, w_rows, side_col=None):
    m, d = x.shape
    n = w_bf.shape[0] if w_rows else w_bf.shape[1]
    tm = _pick(m, (2048, 1024, 512, 256, 128))
    tn = _pick(n, (1024, 512, 256, 128))
    w_spec = pl.BlockSpec((tn, d), lambda i, j: (j, 0)) if w_rows else pl.BlockSpec((d, tn), lambda i, j: (0, j))
    side = None if side_col is None else (side_col // tn, side_col % tn)
    out_specs = [pl.BlockSpec((tm, tn), lambda i, j: (i, j))]
    out_shape = [jax.ShapeDtypeStruct((m, n), out_dtype)]
    if side is not None:
        out_specs.append(pl.BlockSpec((tm, LANES), lambda i, j: (i, 0)))
        out_shape.append(jax.ShapeDtypeStruct((m, LANES), F32))
    res = pl.pallas_call(
        functools.partial(_proj_kernel, act=act, w_rows=w_rows, side=side),
        grid=(m // tm, n // tn),
        in_specs=[pl.BlockSpec((tm, d), lambda i, j: (i, 0)),
                  pl.BlockSpec((1, d), lambda i, j: (0, 0)),
                  w_spec],
        out_specs=out_specs,
        out_shape=out_shape,
        scratch_shapes=[pltpu.VMEM((tm, d), BF16)],
        compiler_params=_cp(("parallel", "arbitrary")),
        name=name,
    )(x, g_row, w_bf)
    return res if side is not None else res[0]


PREP_TS = 512
VT_CHUNK = 256


def _prep_common(aq, ak, av, cq, ck, cv, mq, sm, gaq, gak, gcq, gck, gmq, fb, bd):
    bdv = bd[...]
    f32 = lambda ref: ref[...].astype(F32)
    return dict(
        a_qs=(_head_norm(f32(aq), bdv) * gaq[...]).astype(BF16),
        a_kn=_head_norm(f32(ak), bdv) * gak[...],
        a_v=f32(av),
        c_qs=(_head_norm(f32(cq), bdv) * gcq[...]).astype(BF16),
        c_kn=_head_norm(f32(ck), bdv) * gck[...],
        c_v=f32(cv),
        m_qs=(_head_norm(f32(mq), bdv) * gmq[...]).astype(BF16),
        lf=_log_sigmoid(sm[...] + fb[...]))


def _prep_prompt_kernel(aq, ak, av, cq, ck, cv, mq, sm, gaq, gak, gcq, gck, gmq, fb, bd,
                        o_aq, o_akb, o_avtb, o_akt, o_avt, o_cq, o_ckb, o_ckt, o_cvt, o_cvtb, o_mq, o_lf, o_lft,
                        *, steps_per_seq):
    r = _prep_common(aq, ak, av, cq, ck, cv, mq, sm, gaq, gak, gcq, gck, gmq, fb, bd)
    o_aq[...] = r['a_qs']
    o_akb[...] = r['a_kn'].astype(BF16)
    avt = r['a_v'].T
    avtb = avt.astype(BF16)
    for c in range(PREP_TS // VT_CHUNK):
        o_avtb[c] = avtb[:, c * VT_CHUNK:(c + 1) * VT_CHUNK]

    @pl.when(pl.program_id(0) % steps_per_seq == steps_per_seq - 1)
    def _():
        o_akt[...] = r['a_kn'].T
        o_avt[...] = avt

    o_cq[...] = r['c_qs']
    o_ckb[...] = r['c_kn'].astype(BF16)
    o_ckt[...] = r['c_kn'].T
    cvt = r['c_v'].T
    o_cvt[...] = cvt
    cvtb = cvt.astype(BF16)
    for c in range(PREP_TS // VT_CHUNK):
        o_cvtb[c] = cvtb[:, c * VT_CHUNK:(c + 1) * VT_CHUNK]
    o_mq[...] = r['m_qs']
    o_lf[...] = r['lf']
    o_lft[...] = r['lf'].T[CF_LANE:CF_LANE + C_HEADS, :]


def _prep_sample_kernel(aq, ak, av, cq, ck, cv, mq, sm, gaq, gak, gcq, gck, gmq, fb, bd,
                        o_aq, o_akb, o_avb, o_ak4, o_av4, o_cq, o_ckb, o_cvb, o_ck4, o_cv4, o_mq, o_lf):
    r = _prep_common(aq, ak, av, cq, ck, cv, mq, sm, gaq, gak, gcq, gck, gmq, fb, bd)
    ts = o_aq.shape[0]
    o_aq[...] = r['a_qs']
    o_akb[...] = r['a_kn'].astype(BF16)
    o_avb[...] = r['a_v'].astype(BF16)
    o_ak4[...] = r['a_kn'].reshape(ts, A_HEADS, HEAD_DIM)
    o_av4[...] = r['a_v'].reshape(ts, A_HEADS, HEAD_DIM)
    o_cq[...] = r['c_qs']
    o_ckb[...] = r['c_kn'].astype(BF16)
    o_cvb[...] = r['c_v'].astype(BF16)
    o_ck4[...] = r['c_kn'].reshape(ts, C_HEADS, HEAD_DIM)
    o_cv4[...] = r['c_v'].reshape(ts, C_HEADS, HEAD_DIM)
    o_mq[...] = r['m_qs']
    o_lf[...] = r['lf']


def _prep_in_specs(ts):
    def ucol(name, width):
        idx = _U_OFF[name] // width
        assert idx * width == _U_OFF[name]
        return pl.BlockSpec((ts, width), lambda i: (i, idx))

    def row(width):
        return pl.BlockSpec((1, width), lambda i: (0, 0))

    return [ucol('a_q', 512), ucol('a_k', 512), ucol('a_v', 512),
            ucol('c_q', 512), ucol('c_k', 512), ucol('c_v', 512),
            ucol('m_q', 256), pl.BlockSpec((ts, LANES), lambda i: (i, 0)),
            row(512), row(512), row(512), row(512), row(256), row(LANES),
            pl.BlockSpec((MXU_WIDTH, MXU_WIDTH), lambda i: (0, 0))]


def _prep_prompt(u, small, gains, nb, seq):
    ts = PREP_TS
    assert seq % ts == 0 and min(A_WIN, seq) == ts
    sps = seq // ts
    m = nb * seq
    sds = jax.ShapeDtypeStruct
    rows = lambda width: pl.BlockSpec((ts, width), lambda i: (i, 0))
    last = pl.BlockSpec((None, 512, ts), lambda i: (i // sps, 0, 0))
    feat = pl.BlockSpec((None, 512, ts), lambda i: (i // sps, 0, i % sps))
    nvc = ts // VT_CHUNK
    chunks = pl.BlockSpec((None, nvc, 512, VT_CHUNK), lambda i: (i // sps, i % sps, 0, 0))
    out_specs = [rows(512), rows(512), chunks, last, last,
                 rows(512), rows(512), feat, feat, chunks,
                 rows(256), rows(LANES),
                 pl.BlockSpec((None, C_HEADS, ts), lambda i: (i // sps, 0, i % sps))]
    out_shape = [sds((m, 512), BF16), sds((m, 512), BF16), sds((nb, seq // VT_CHUNK, 512, VT_CHUNK), BF16),
                 sds((nb, 512, ts), F32), sds((nb, 512, ts), F32),
                 sds((m, 512), BF16), sds((m, 512), BF16), sds((nb, 512, seq), F32), sds((nb, 512, seq), F32),
                 sds((nb, seq // VT_CHUNK, 512, VT_CHUNK), BF16),
                 sds((m, 256), BF16), sds((m, LANES), F32), sds((nb, C_HEADS, seq), F32)]
    return pl.pallas_call(
        functools.partial(_prep_prompt_kernel, steps_per_seq=sps),
        grid=(m // ts,),
        in_specs=_prep_in_specs(ts),
        out_specs=out_specs,
        out_shape=out_shape,
        compiler_params=_cp(("arbitrary",)),
        name="prep_prompt",
    )(*([u] * 7), small, *gains)


def _prep_sample(u, small, gains):
    m = u.shape[0]
    ts = _pick(m, (512, 256, 128, 64))
    sds = jax.ShapeDtypeStruct
    rows = lambda width: pl.BlockSpec((ts, width), lambda i: (i, 0))
    heads = pl.BlockSpec((ts, A_HEADS, HEAD_DIM), lambda i: (i, 0, 0))
    return pl.pallas_call(
        _prep_sample_kernel,
        grid=(m // ts,),
        in_specs=_prep_in_specs(ts),
        out_specs=[rows(512), rows(512), rows(512), heads, heads,
                   rows(512), rows(512), rows(512), heads, heads, rows(256), rows(LANES)],
        out_shape=[sds((m, 512), BF16), sds((m, 512), BF16), sds((m, 512), BF16),
                   sds((m, A_HEADS, HEAD_DIM), F32), sds((m, A_HEADS, HEAD_DIM), F32),
                   sds((m, 512), BF16), sds((m, 512), BF16), sds((m, 512), BF16),
                   sds((m, C_HEADS, HEAD_DIM), F32), sds((m, C_HEADS, HEAD_DIM), F32),
                   sds((m, 256), BF16), sds((m, LANES), F32)],
        compiler_params=_cp(("parallel",)),
        name="prep_sample",
    )(*([u] * 7), small, *gains)


def _memkv_kernel(kv, g, bd, o_kt, o_vt):
    o_kt[...] = (_head_norm(kv[:, :M_WIDTH], bd[...]) * g[...]).T
    o_vt[...] = kv[:, M_WIDTH:].T


def _memkv(kv, g_row, bd256, nb, n_mem):
    spec = pl.BlockSpec((None, M_WIDTH, n_mem), lambda b: (b, 0, 0))
    return pl.pallas_call(
        _memkv_kernel,
        grid=(nb,),
        in_specs=[pl.BlockSpec((n_mem, 2 * M_WIDTH), lambda b: (b, 0)),
                  pl.BlockSpec((1, M_WIDTH), lambda b: (0, 0)),
                  pl.BlockSpec((M_WIDTH, M_WIDTH), lambda b: (0, 0))],
        out_specs=[spec, spec],
        out_shape=[jax.ShapeDtypeStruct((nb, M_WIDTH, n_mem), F32)] * 2,
        compiler_params=_cp(("parallel",)),
        name="memkv",
    )(kv, g_row, bd256)


def _cum_kernel(lf_ref, tri_ref, selq_ref, selk_ref, qc_ref, kc_ref, blk_ref, low_ref, cq_ref, ck_ref, carry_ref,
                *, nchunk, unroll):
    totals = _sel_dot(blk_ref[...], lf_ref[...])
    carry_ref[...] = _sel_dot(low_ref[...], totals)
    tri = tri_ref[...]

    def body(cc, carry):
        cs = [cc * unroll + k for k in range(unroll)]
        rs = [pl.multiple_of(c * LANES, LANES) for c in cs]
        xs = [lf_ref[pl.ds(r, LANES), :] for r in rs]
        zs = [(_sel_dot(tri, x) + carry_ref[pl.ds(c, 1), :]) * LOG2E for x, c in zip(xs, cs)]
        parts = [_split3(z) for z in zs]
        qs = [_dot(p[0], selq_ref[0]) + _dot(p[1], selq_ref[1]) + _dot(p[2], selq_ref[2]) + qc_ref[...]
              for p in parts]
        ks = [_dot(p[0], selk_ref[0]) + _dot(p[1], selk_ref[1]) + _dot(p[2], selk_ref[2]) + kc_ref[...]
              for p in parts]
        for r, qv, kv in zip(rs, qs, ks):
            cq_ref[pl.ds(r, LANES), :] = qv.astype(BF16)
            ck_ref[pl.ds(r, LANES), :] = kv.astype(BF16)
        return carry

    lax.fori_loop(0, nchunk // unroll, body, 0)


CUM_UNROLLS = (8, 4, 3, 2, 1)


def _cum(lf, consts):
    nseq, length, _ = lf.shape
    assert length % LANES == 0
    nchunk = length // LANES
    unroll = _pick(nchunk, CUM_UNROLLS)
    ncp = -(-nchunk // SUBLANES) * SUBLANES
    tri, selq, selk, qc, kc = consts
    chunk_of = np.arange(length) // LANES
    blk = jnp.asarray((np.arange(ncp)[:, None] == chunk_of[None, :]).astype(np.float32), BF16)
    low = jnp.asarray(np.tril(np.ones((ncp, ncp), np.float32), -1), BF16)
    full = lambda shape: pl.BlockSpec(shape, lambda s: (0,) * len(shape))
    seq = pl.BlockSpec((None, length, LANES), lambda s: (s, 0, 0))
    return pl.pallas_call(
        functools.partial(_cum_kernel, nchunk=nchunk, unroll=unroll),
        grid=(nseq,),
        in_specs=[seq, full((LANES, LANES)), full((3, LANES, LANES)), full((3, LANES, LANES)),
                  full((1, LANES)), full((1, LANES)), full((ncp, length)), full((ncp, ncp))],
        out_specs=[seq, seq],
        out_shape=[jax.ShapeDtypeStruct(lf.shape, BF16)] * 2,
        scratch_shapes=[pltpu.VMEM((ncp, LANES), F32)],
        compiler_params=_cp(("parallel",)),
        name="cum",
    )(lf, tri, selq, selk, qc, kc, blk, low)


def _cum_consts():
    tri = np.tril(np.ones((LANES, LANES), np.float32))
    selq = np.zeros((3, LANES, LANES), np.float32)
    selk = np.zeros((3, LANES, LANES), np.float32)
    qc = np.zeros((1, LANES), np.float32)
    kc = np.zeros((1, LANES), np.float32)
    for h in range(C_HEADS):
        for p in range(3):
            selq[p, CF_LANE + h, AUG_PER_HEAD * h + p] = 1.0
            selk[p, CF_LANE + h, AUG_PER_HEAD * h + 3 + p] = -1.0
            qc[0, AUG_PER_HEAD * h + 3 + p] = 1.0
            kc[0, AUG_PER_HEAD * h + p] = 1.0
    return (jnp.asarray(tri, BF16), jnp.asarray(selq, BF16), jnp.asarray(selk, BF16),
            jnp.asarray(qc), jnp.asarray(kc))


BAND_Q = 4 * CHUNK
BAND_K = 3 * BAND_Q
TAB_PAD = 384
DIAG_PAD = 1024


def _bias_kernel(tab_ref, idx_ref, neg_ref, o_ref):
    nrows, ncols = neg_ref.shape
    onehot = (lax.broadcasted_iota(jnp.int32, (TAB_PAD, DIAG_PAD), 0) == idx_ref[...]).astype(BF16)
    diag = _dot_sel(tab_ref[...], onehot) * LOG2E
    for h in range(A_HEADS):
        rows = jnp.broadcast_to(diag[h:h + 1, :], (nrows, DIAG_PAD))
        o_ref[h] = pltpu.roll(rows, 0, 1, stride=1, stride_axis=0)[:, :ncols] + neg_ref[...]


def _band_bias(table, nq, nk, key_major):
    assert nq + nk <= DIAG_PAD
    r = np.arange(nq)[None, :] if key_major else np.arange(nq)[:, None]
    t = np.arange(nk)[:, None] if key_major else np.arange(nk)[None, :]
    cb = t // CHUNK - r // CHUNK
    neg = np.where((cb >= 0) & (cb <= A_LEFT_CHUNKS), 0.0, NEG).astype(np.float32)
    p = np.arange(DIAG_PAD)
    ncols = nq if key_major else nk
    col_minus_row = np.where(p < ncols, p, p - DIAG_PAD)
    q_minus_k = col_minus_row if key_major else -col_minus_row
    idx = (np.clip(A_WIN + q_minus_k, -A_REL_CLIP, A_REL_CLIP) + A_REL_CLIP).astype(np.int32).reshape(1, -1)
    tab = jnp.zeros((A_HEADS, TAB_PAD), F32).at[:, :table.shape[0]].set(table.T)
    full = lambda shape: pl.BlockSpec(shape, lambda: (0,) * len(shape))
    return pl.pallas_call(
        _bias_kernel,
        in_specs=[full((A_HEADS, TAB_PAD)), full((1, DIAG_PAD)), full(neg.shape)],
        out_specs=full((A_HEADS,) + neg.shape),
        out_shape=jax.ShapeDtypeStruct((A_HEADS,) + neg.shape, F32),
        compiler_params=pltpu.CompilerParams(vmem_limit_bytes=VMEM_LIMIT),
        name="band_bias",
    )(tab, jnp.asarray(idx), jnp.asarray(neg))


def _half_mask(par):
    lane = _lane_iota((1, LANES))
    return (lane < HEAD_DIM) if par == 0 else (lane >= HEAD_DIM)


def _softmax_pv_many(s_lists, v_lists, v_feature_major=None):
    n = len(s_lists)
    nblk = len(s_lists[0])
    if v_feature_major is None:
        v_feature_major = (False,) * nblk
    ms = []
    for sb in s_lists:
        m = sb[0].max(axis=1, keepdims=True)
        for s in sb[1:]:
            m = jnp.maximum(m, s.max(axis=1, keepdims=True))
        ms.append(m)
    ps = [[jnp.exp2(s - ms[i]) for s in s_lists[i]] for i in range(n)]
    ls = []
    for i in range(n):
        l = ps[i][0].sum(axis=1, keepdims=True)
        for p in ps[i][1:]:
            l = l + p.sum(axis=1, keepdims=True)
        ls.append(l)
    outs = []
    for i in range(n):
        acc = None
        for p, v, fm in zip(ps[i], v_lists[i], v_feature_major):
            pv = _dot_nt(p.astype(BF16), v) if fm else _dot(p.astype(BF16), v)
            acc = pv if acc is None else acc + pv
        outs.append(acc)
    return [o / l for o, l in zip(outs, ls)]


ONES_ROWS = 16


def _band_prompt_kernel(q_ref, k0, k1, k2, v0, v1, v2, bias_ref, z_ref, o_ref):
    g = pl.program_id(1)
    krefs = (k0, k1, k2)
    vrefs = (v0, v1, v2)
    s_lists = []
    for hp in range(A_HEADS // 2):
        cols = slice(hp * LANES, (hp + 1) * LANES)
        qp = q_ref[:, cols]
        ks = [kr[:, cols] for kr in krefs]
        for par in range(2):
            h = 2 * hp + par
            qm = jnp.where(_half_mask(par), qp, jnp.zeros_like(qp))
            sb = []
            for j in range(3):
                s = _dot_nt(ks[j], qm) + bias_ref[h, j * BAND_Q:(j + 1) * BAND_Q, :]
                if j < 2:
                    s = jnp.where(g + j >= 2, s, NEG)
                sb.append(s)
            s_lists.append(sb)
    ms = []
    for sb in s_lists:
        m = sb[0].max(axis=0, keepdims=True)
        for s in sb[1:]:
            m = jnp.maximum(m, s.max(axis=0, keepdims=True))
        ms.append(m)
    ps = [[jnp.exp2(s - m).astype(BF16) for s in sb] for sb, m in zip(s_lists, ms)]
    ones = jnp.ones((ONES_ROWS, BAND_Q), BF16)
    outs = []
    for h in range(A_HEADS):
        acc = None
        for j in range(3):
            vt = jnp.concatenate([vrefs[j][h * HEAD_DIM:(h + 1) * HEAD_DIM, :], ones], axis=0)
            d = _dot(vt, ps[h][j])
            acc = d if acc is None else acc + d
        outs.append(acc[:HEAD_DIM] / acc[HEAD_DIM:HEAD_DIM + 1])
    o_ref[...] = (jnp.concatenate(outs, axis=0).T * _silu(z_ref[...].astype(F32))).astype(BF16)


def _band_prompt(qs, kb, vtb, bias_t, u, nb, seq):
    assert VT_CHUNK == BAND_Q
    ng = seq // BAND_Q

    def kblk(j):
        return pl.BlockSpec((BAND_Q, A_WIDTH), lambda b, g: (b * ng + jnp.maximum(g - 2 + j, 0), 0))

    def vblk(j):
        return pl.BlockSpec((None, None, A_WIDTH, BAND_Q), lambda b, g: (b, jnp.maximum(g - 2 + j, 0), 0, 0))

    cur = pl.BlockSpec((BAND_Q, A_WIDTH), lambda b, g: (b * ng + g, 0))
    zidx = _U_OFF['a_z'] // A_WIDTH
    return pl.pallas_call(
        _band_prompt_kernel,
        grid=(nb, ng),
        in_specs=[cur, kblk(0), kblk(1), kblk(2), vblk(0), vblk(1), vblk(2),
                  pl.BlockSpec((A_HEADS, BAND_K, BAND_Q), lambda b, g: (0, 0, 0)),
                  pl.BlockSpec((BAND_Q, A_WIDTH), lambda b, g: (b * ng + g, zidx))],
        out_specs=cur,
        out_shape=jax.ShapeDtypeStruct((nb * seq, A_WIDTH), BF16),
        compiler_params=_cp(("parallel", "parallel")),
        name="band_prompt",
    )(qs, kb, kb, kb, vtb, vtb, vtb, bias_t, u)


def _band_sample_kernel(q_ref, kc_ref, kn_ref, vc_ref, vn_ref, bias_ref, z_ref, o_ref, *, t):
    lane = _lane_iota((1, LANES))
    lc = kc_ref.shape[1]
    pad = jnp.zeros((LANES - t, LANES), BF16)
    s_lists, v_lists = [], []
    for hp in range(A_HEADS // 2):
        cols = slice(hp * LANES, (hp + 1) * LANES)
        qp = q_ref[:, cols]
        kc = kc_ref[cols, :].astype(BF16)
        vc = vc_ref[cols, :].astype(BF16)
        kn = jnp.concatenate([kn_ref[:, cols], pad], axis=0)
        vn = jnp.concatenate([vn_ref[:, cols], pad], axis=0)
        for par in range(2):
            h = 2 * hp + par
            qm = jnp.where(_half_mask(par), qp, jnp.zeros_like(qp))
            s_lists.append([_dot(qm, kc) + bias_ref[h, :, :lc], _dot_nt(qm, kn) + bias_ref[h, :, lc:]])
            v_lists.append([vc, vn])
    outs = _softmax_pv_many(s_lists, v_lists, (True, False))
    for hp in range(A_HEADS // 2):
        cols = slice(hp * LANES, (hp + 1) * LANES)
        o = jnp.where(lane < HEAD_DIM, outs[2 * hp], outs[2 * hp + 1])
        o_ref[:, cols] = (o * _silu(z_ref[:, cols].astype(F32))).astype(BF16)


def _band_sample(qs, kb, vb, cache_kt, cache_vt, layer, bias_s, u, nseq, t):
    lc = cache_kt.shape[3]
    new = pl.BlockSpec((t, A_WIDTH), lambda s: (s, 0))
    cache = pl.BlockSpec((None, None, A_WIDTH, lc), lambda s: (layer, s, 0, 0))
    return pl.pallas_call(
        functools.partial(_band_sample_kernel, t=t),
        grid=(nseq,),
        in_specs=[new, cache, new, cache, new,
                  pl.BlockSpec(bias_s.shape, lambda s: (0, 0, 0)),
                  pl.BlockSpec((t, A_WIDTH), lambda s: (s, _U_OFF['a_z'] // A_WIDTH))],
        out_specs=new,
        out_shape=jax.ShapeDtypeStruct((nseq * t, A_WIDTH), BF16),
        compiler_params=_cp(("parallel",)),
        name="band_sample",
    )(qs, cache_kt, kb, cache_vt, vb, bias_s, u)


FOX_TQ = 512
FOX_TK = 512


def _aug_mask(h):
    lane = _lane_iota((1, LANES))
    return (lane >= AUG_PER_HEAD * h) & (lane < AUG_PER_HEAD * (h + 1))


def _fox_prompt_kernel(q_ref, cq_ref, k_ref, ck_ref, vt_ref, z_ref, o_ref, st_ref, m_ref, acc_ref):
    hp = pl.program_id(1)
    i = pl.program_id(2)
    qp = q_ref[...]
    cq = cq_ref[...]
    qcs = []
    for par in range(2):
        qm = jnp.where(_half_mask(par), qp, jnp.zeros_like(qp))
        qa = jnp.where(_aug_mask(2 * hp + par), cq, jnp.zeros_like(cq))
        qcs.append(jnp.concatenate([qm, qa], axis=1))
    qcat = jnp.concatenate(qcs, axis=0)
    m_ref[...] = jnp.full(m_ref.shape, NEG, F32)
    acc_ref[...] = jnp.zeros(acc_ref.shape, F32)
    ratio = FOX_TK // FOX_TQ
    nvc = FOX_TK // VT_CHUNK
    ones = jnp.ones((ONES_ROWS, VT_CHUNK), BF16)

    def scores(j, slot):
        off = pl.multiple_of(j * FOX_TK, FOX_TK)
        kc = jnp.concatenate([k_ref[pl.ds(off, FOX_TK), :], ck_ref[pl.ds(off, FOX_TK), :]], axis=1)
        st_ref[slot] = _dot_nt(kc, qcat)

    def reduce(j, slot, masked):
        st = st_ref[slot]
        if masked:
            shape = (FOX_TK, 2 * FOX_TQ)
            qpos = (_lane_iota(shape) & (FOX_TQ - 1)) + (i % ratio) * FOX_TQ
            st = jnp.where(_row_iota(shape) <= qpos, st, NEG)
        m_old = m_ref[...]
        m_new = jnp.maximum(m_old, st.max(axis=0, keepdims=True))
        alpha = jnp.exp2(m_old - m_new)
        pb = jnp.exp2(st - m_new).astype(BF16)
        for par in range(2):
            cols = slice(par * FOX_TQ, (par + 1) * FOX_TQ)
            pv = None
            for c in range(nvc):
                vt = jnp.concatenate([vt_ref[j * nvc + c, par * HEAD_DIM:(par + 1) * HEAD_DIM, :], ones], axis=0)
                d = _dot(vt, pb[c * VT_CHUNK:(c + 1) * VT_CHUNK, cols])
                pv = d if pv is None else pv + d
            acc_ref[par] = alpha[:, cols] * acc_ref[par] + pv
        m_ref[...] = m_new

    nfull = i // ratio
    scores(0, 0)

    def body(jj, carry):
        j = 2 * jj
        scores(j + 1, 1)
        reduce(j, 0, False)
        scores(j + 2, 0)
        reduce(j + 1, 1, False)
        return carry

    lax.fori_loop(0, nfull // 2, body, 0)

    @pl.when(nfull % 2 == 0)
    def _():
        reduce(nfull, 0, True)

    @pl.when(nfull % 2 == 1)
    def _():
        scores(nfull, 1)
        reduce(nfull - 1, 0, False)
        reduce(nfull, 1, True)

    ot = jnp.concatenate([acc_ref[par, :HEAD_DIM] / acc_ref[par, HEAD_DIM:HEAD_DIM + 1] for par in range(2)],
                         axis=0)
    o_ref[...] = (ot.T * _silu(z_ref[...].astype(F32))).astype(BF16)


def _fox_prompt(qs, kb, vtb, cq, ck, u, nb, seq):
    nq = seq // FOX_TQ
    qspec = pl.BlockSpec((FOX_TQ, LANES), lambda b, hp, i: (b * nq + i, hp))
    cqspec = pl.BlockSpec((FOX_TQ, LANES), lambda b, hp, i: (b * nq + i, 0))
    kspec = pl.BlockSpec((seq, LANES), lambda b, hp, i: (b, hp))
    ckspec = pl.BlockSpec((seq, LANES), lambda b, hp, i: (b, 0))
    vspec = pl.BlockSpec((None, seq // VT_CHUNK, LANES, VT_CHUNK), lambda b, hp, i: (b, 0, hp, 0))
    z0 = _U_OFF['c_z'] // LANES
    zspec = pl.BlockSpec((FOX_TQ, LANES), lambda b, hp, i: (b * nq + i, z0 + hp))
    return pl.pallas_call(
        _fox_prompt_kernel,
        grid=(nb, C_HEADS // 2, nq),
        in_specs=[qspec, cqspec, kspec, ckspec, vspec, zspec],
        out_specs=qspec,
        out_shape=jax.ShapeDtypeStruct((nb * seq, C_WIDTH), BF16),
        scratch_shapes=[pltpu.VMEM((2, FOX_TK, 2 * FOX_TQ), F32),
                        pltpu.VMEM((1, 2 * FOX_TQ), F32),
                        pltpu.VMEM((2, HEAD_DIM + ONES_ROWS, FOX_TQ), F32)],
        compiler_params=_cp(("parallel", "parallel", "arbitrary")),
        name="fox_prompt",
    )(qs, cq, kb, ck, vtb, u)


def _fox_sample_kernel(q_ref, cq_ref, kc_ref, kn_ref, ck_ref, vc_ref, vn_ref, z_ref, o_ref, *, t, past):
    lane = _lane_iota((1, LANES))
    cq = cq_ref[past:past + t, :]
    pad = jnp.zeros((LANES - t, LANES), BF16)
    ck_cache_t = ck_ref[:past, :].astype(F32).T.astype(BF16)
    ck_new = ck_ref[past:, :]
    vis = _lane_iota((t, LANES)) <= _row_iota((t, LANES))
    s_lists, v_lists = [], []
    for hp in range(C_HEADS // 2):
        cols = slice(hp * LANES, (hp + 1) * LANES)
        qp = q_ref[:, cols]
        kc = jnp.concatenate([kc_ref[cols, :].astype(BF16), ck_cache_t], axis=0)
        kn = jnp.concatenate([jnp.concatenate([kn_ref[:, cols], pad], axis=0), ck_new], axis=1)
        vc = vc_ref[cols, :].astype(BF16)
        vn = jnp.concatenate([vn_ref[:, cols], pad], axis=0)
        for par in range(2):
            qm = jnp.where(_half_mask(par), qp, jnp.zeros_like(qp))
            qa = jnp.where(_aug_mask(2 * hp + par), cq, jnp.zeros_like(cq))
            qc = jnp.concatenate([qm, qa], axis=1)
            s_lists.append([_dot(qc, kc), jnp.where(vis, _dot_nt(qc, kn), NEG)])
            v_lists.append([vc, vn])
    outs = _softmax_pv_many(s_lists, v_lists, (True, False))
    for hp in range(C_HEADS // 2):
        cols = slice(hp * LANES, (hp + 1) * LANES)
        o = jnp.where(lane < HEAD_DIM, outs[2 * hp], outs[2 * hp + 1])
        o_ref[:, cols] = (o * _silu(z_ref[:, cols].astype(F32))).astype(BF16)


def _fox_sample(qs, kb, vb, cq, ck, cache_kt, cache_vt, layer, u, nseq, t):
    past = cache_kt.shape[3]
    new = pl.BlockSpec((t, C_WIDTH), lambda s: (s, 0))
    cache = pl.BlockSpec((None, None, C_WIDTH, past), lambda s: (layer, s, 0, 0))
    aug = pl.BlockSpec((None, past + LANES, LANES), lambda s: (s, 0, 0))
    return pl.pallas_call(
        functools.partial(_fox_sample_kernel, t=t, past=past),
        grid=(nseq,),
        in_specs=[new, aug, cache, new, aug, cache, new,
                  pl.BlockSpec((t, C_WIDTH), lambda s: (s, _U_OFF['c_z'] // C_WIDTH))],
        out_specs=new,
        out_shape=jax.ShapeDtypeStruct((nseq * t, C_WIDTH), BF16),
        compiler_params=_cp(("parallel",)),
        name="fox_sample",
    )(qs, cq, cache_kt, kb, ck, cache_vt, vb, u)


def _cross_kernel(q_ref, k_ref, v_ref, z_ref, o_ref):
    lane = _lane_iota((1, LANES))
    s_lists, v_lists = [], []
    for hp in range(M_HEADS // 2):
        cols = slice(hp * LANES, (hp + 1) * LANES)
        qp = q_ref[:, cols]
        k = k_ref[cols, :].astype(BF16)
        v = v_ref[cols, :].astype(BF16)
        for par in range(2):
            qm = jnp.where(_half_mask(par), qp, jnp.zeros_like(qp))
            s_lists.append([_dot(qm, k)])
            v_lists.append([v])
    outs = _softmax_pv_many(s_lists, v_lists, (True,))
    for hp in range(M_HEADS // 2):
        cols = slice(hp * LANES, (hp + 1) * LANES)
        o = jnp.where(lane < HEAD_DIM, outs[2 * hp], outs[2 * hp + 1])
        o_ref[:, cols] = (o * _silu(z_ref[:, cols].astype(F32))).astype(BF16)


def _cross(qs, mkt, mvt, layer, u, rows_per_seq, name):
    nrows = qs.shape[0]
    n_mem = mkt.shape[3]
    tq = _pick(rows_per_seq, (512, 256, 128, 64))
    per = rows_per_seq // tq
    mem = pl.BlockSpec((None, None, M_WIDTH, n_mem), lambda i: (layer, i // per, 0, 0))
    rows = pl.BlockSpec((tq, M_WIDTH), lambda i: (i, 0))
    return pl.pallas_call(
        _cross_kernel,
        grid=(nrows // tq,),
        in_specs=[rows, mem, mem, pl.BlockSpec((tq, M_WIDTH), lambda i: (i, _U_OFF['m_z'] // M_WIDTH))],
        out_specs=rows,
        out_shape=jax.ShapeDtypeStruct((nrows, M_WIDTH), BF16),
        compiler_params=_cp(("parallel",)),
        name=name,
    )(qs, mkt, mvt, u)


SSD_Q = 128
CONV_PAD = SUBLANES
SSD_GROUP = 4


def _ssd_kernel(xbc_ref, z_ref, sm_ref, cs0_ref, hs0_ref, cw_ref, cb_ref, dtb_ref, alog_ref, dsk_ref,
                gn_ref, exp_ref, tri_ref, qm_ref,
                o_ref, hs_out_ref, cs_out_ref, hs_ref, xext_ref, *, nv, nchunk):
    c = pl.program_id(1)
    q = SSD_Q
    grp_ids = range(SSD_GROUP)
    lane = _lane_iota((1, LANES))
    head_lane = (lane >= DT_LANE) & (lane < DT_LANE + B_HEADS)
    a_row = jnp.where(head_lane, -jnp.exp(alog_ref[...]), 0.0)
    causal = _lane_iota((q, q)) <= _row_iota((q, q))
    heads_per_group = B_HEADS // B_GROUPS
    expand = exp_ref[...]
    tri = tri_ref[...]
    qm = qm_ref[...]

    @pl.when(c == 0)
    def _():
        for g in grp_ids:
            hs_ref[g] = hs0_ref[g].T * qm
            xext_ref[g, 0:CONV_PAD, :] = cs0_ref[g]

    for g in grp_ids:
        xext_ref[g, CONV_PAD:CONV_PAD + nv, :] = xbc_ref[g].astype(F32)
        if nv < q:
            xext_ref[g, CONV_PAD + nv:CONV_PAD + q, :] = jnp.zeros((q - nv, B_XBC), F32)
    xa = []
    for g in grp_ids:
        y = cb_ref[...]
        for tap in range(B_CONV):
            y = y + xext_ref[g, pl.ds(CONV_PAD - (B_CONV - 1) + tap, q), :] * cw_ref[tap:tap + 1, :]
        xa.append(_silu(y))
    for g in grp_ids:
        tail = xext_ref[g, nv:nv + CONV_PAD, :]
        xext_ref[g, 0:CONV_PAD, :] = tail
        cs_out_ref[g] = tail
    xs = [x[:, :B_WIDTH] for x in xa]
    bm = [x[:, B_WIDTH:B_WIDTH + LANES].astype(BF16) for x in xa]
    cm = [x[:, B_WIDTH + LANES:] for x in xa]

    dt = []
    for g in grp_ids:
        d = _softplus(sm_ref[g] + dtb_ref[...])
        if nv < q:
            d = jnp.concatenate([d, jnp.zeros((q - nv, LANES), F32)], axis=0)
        dt.append(jnp.where(head_lane, d, 0.0))
    acum = [_sel_dot(tri, d * a_row) for d in dt]
    dt_e = [_dot_sel(d, expand) for d in dt]
    acum_e = [_dot_sel(a, expand) for a in acum]
    alast_e = [a[q - 1:q, :] for a in acum_e]
    xdt = [x * d for x, d in zip(xs, dt_e)]

    hs = [hs_ref[g] for g in grp_ids]
    y_off = [_dot(cm[g].astype(BF16), hs[g].astype(BF16)) * jnp.exp(acum_e[g]) for g in grp_ids]
    upd = [_dot_tn(bm[g], (xdt[g] * jnp.exp(alast_e[g] - acum_e[g])).astype(BF16)) for g in grp_ids]
    hs_new = [(jnp.exp(alast_e[g]) * hs[g] + upd[g]) * qm for g in grp_ids]
    for g in grp_ids:
        hs_ref[g] = hs_new[g]

    @pl.when(c == nchunk - 1)
    def _():
        for g in grp_ids:
            hs_out_ref[g] = hs_new[g].T

    acum_t = [a.T for a in acum]
    xdt_b = [x.astype(BF16) for x in xdt]
    cbm = [[None] * B_GROUPS for _ in grp_ids]
    for grp in range(B_GROUPS):
        for g in grp_ids:
            cg = jnp.where(_half_mask(grp), cm[g], 0.0).astype(BF16)
            cbm[g][grp] = _dot_nt(cg, bm[g])
    y_pairs = [[] for _ in grp_ids]
    for hp in range(B_HEADS // 2):
        pair = [[] for _ in grp_ids]
        for par in range(2):
            h = 2 * hp + par
            for g in grp_ids:
                seg = acum[g][:, DT_LANE + h:DT_LANE + h + 1] - acum_t[g][DT_LANE + h:DT_LANE + h + 1, :]
                lmat = jnp.exp(jnp.where(causal, seg, -jnp.inf))
                m = (cbm[g][h // heads_per_group] * lmat).astype(BF16)
                pair[g].append(_dot(m, xdt_b[g][:, hp * LANES:(hp + 1) * LANES]))
        for g in grp_ids:
            y_pairs[g].append(jnp.where(lane < HEAD_DIM, pair[g][0], pair[g][1]))

    for g in grp_ids:
        yt = y_off[g] + jnp.concatenate(y_pairs[g], axis=1) + dsk_ref[...] * xs[g]
        if nv < q:
            yt = yt[:nv]
        yz = yt * _silu(z_ref[g].astype(F32))
        ms = jnp.mean(yz * yz, axis=-1, keepdims=True)
        o_ref[g] = (yz * lax.rsqrt(ms + EPS) * gn_ref[...]).astype(BF16)


def _ssd(u, small, cs0, hs0, consts, nseq, rows_per_seq):
    nv = min(SSD_Q, rows_per_seq)
    nchunk = rows_per_seq // nv
    gsz = SSD_GROUP
    assert nseq % gsz == 0
    cw, cb, dtb, alog, dsk, gn, expand, tri, qmask = consts
    u3 = u.reshape(nseq, rows_per_seq, U_WIDTH)
    small3 = small.reshape(nseq, rows_per_seq, LANES)

    def ucol(name, width):
        idx = _U_OFF[name] // width
        assert idx * width == _U_OFF[name]
        return pl.BlockSpec((gsz, nv, width), lambda s, c: (s, c, idx))

    full = lambda a: pl.BlockSpec(a.shape, lambda s, c: (0,) * a.ndim)
    cs_spec = pl.BlockSpec((gsz, CONV_PAD, B_XBC), lambda s, c: (s, 0, 0))
    hs_spec = pl.BlockSpec((gsz, B_WIDTH, LANES), lambda s, c: (s, 0, 0))
    o, hs, cs = pl.pallas_call(
        functools.partial(_ssd_kernel, nv=nv, nchunk=nchunk),
        grid=(nseq // gsz, nchunk),
        in_specs=[ucol('b_xbc', B_XBC), ucol('b_z', B_WIDTH),
                  pl.BlockSpec((gsz, nv, LANES), lambda s, c: (s, c, 0)), cs_spec, hs_spec,
                  full(cw), full(cb), full(dtb), full(alog), full(dsk), full(gn), full(expand), full(tri),
                  full(qmask)],
        out_specs=[pl.BlockSpec((gsz, nv, B_WIDTH), lambda s, c: (s, c, 0)), hs_spec, cs_spec],
        out_shape=[jax.ShapeDtypeStruct((nseq, rows_per_seq, B_WIDTH), BF16),
                   jax.ShapeDtypeStruct((nseq, B_WIDTH, LANES), F32),
                   jax.ShapeDtypeStruct((nseq, CONV_PAD, B_XBC), F32)],
        scratch_shapes=[pltpu.VMEM((gsz, LANES, B_WIDTH), F32),
                        pltpu.VMEM((gsz, SSD_Q + CONV_PAD, B_XBC), F32)],
        compiler_params=_cp(("parallel", "arbitrary")),
        name="ssd",
    )(u3, u3, small3, cs0, hs0, cw, cb, dtb, alog, dsk, gn, expand, tri, qmask)
    return o.reshape(nseq * rows_per_seq, B_WIDTH), hs, cs


def _ssd_static_consts():
    expand = np.zeros((LANES, B_WIDTH), np.float32)
    for h in range(B_HEADS):
        expand[DT_LANE + h, h * HEAD_DIM:(h + 1) * HEAD_DIM] = 1.0
    tri = np.tril(np.ones((SSD_Q, SSD_Q), np.float32))
    qmask = np.zeros((LANES, B_WIDTH), np.float32)
    half = B_WIDTH // B_GROUPS
    for g in range(B_GROUPS):
        qmask[g * B_STATE:(g + 1) * B_STATE, g * half:(g + 1) * half] = 1.0
    return jnp.asarray(expand, BF16), jnp.asarray(tri, BF16), jnp.asarray(qmask)


def _stack_state(h):
    r = h.reshape(h.shape[0], B_WIDTH, B_STATE)
    return jnp.concatenate([r, r], axis=2)


def _unstack_state(hst):
    n = hst.shape[0]
    first = (np.arange(B_WIDTH) < B_WIDTH // B_GROUPS)[None, :, None]
    return jnp.where(first, hst[:, :, :B_STATE], hst[:, :, B_STATE:]).reshape(n, B_HEADS, HEAD_DIM, B_STATE)


def _merge_kernel(x, ya, yb, yc, ym, gt, wpa, wpb, wpc, wpm, wout, o):
    d = x.shape[1]
    pa = _dot(ya[...], wpa[...])
    pb = _dot(yb[...], wpb[...])
    pc = _dot(yc[...], wpc[...])
    pm = _dot(ym[...], wpm[...])
    mix = (gt[:, 0:d].astype(F32) * pa + gt[:, d:2 * d].astype(F32) * pb
           + gt[:, 2 * d:3 * d].astype(F32) * pc + gt[:, 3 * d:4 * d].astype(F32) * pm)
    o[...] = x[...] + _dot(mix.astype(BF16), wout[...])


def _merge(x, gates, ya, yb, yc, ym, wpa, wpb, wpc, wpm, wout):
    m, d = x.shape
    tm = _pick(m, (1024, 512, 256, 128))

    def rows(width):
        return pl.BlockSpec((tm, width), lambda i: (i, 0))

    full = lambda a: pl.BlockSpec(a.shape, lambda i: (0, 0))
    return pl.pallas_call(
        _merge_kernel,
        grid=(m // tm,),
        in_specs=[rows(d), rows(A_WIDTH), rows(B_WIDTH), rows(C_WIDTH), rows(M_WIDTH), rows(N_BRANCH * d),
                  full(wpa), full(wpb), full(wpc), full(wpm), full(wout)],
        out_specs=rows(d),
        out_shape=jax.ShapeDtypeStruct((m, d), F32),
        compiler_params=_cp(("parallel",)),
        name="merge",
    )(x, ya, yb, yc, ym, gates, wpa, wpb, wpc, wpm, wout)


def _row(v):
    return v.reshape(1, -1).astype(F32)


def _lane_row(v, lane0):
    return jnp.zeros((1, LANES), F32).at[0, lane0:lane0 + v.shape[0]].set(v.astype(F32))


def _block_diag_mean(width):
    seg = np.arange(width) // HEAD_DIM
    return jnp.asarray((seg[:, None] == seg[None, :]).astype(np.float32) / HEAD_DIM, BF16)


def kernel(x_prompt, x_sample, mem_prompt, cache_a_k, cache_a_v, cache_c_k, cache_c_v, cache_c_logf, state_b_ssm, state_b_conv, cache_mem_k, cache_mem_v, g_norm, w_in, a_qnorm, a_knorm, a_rel, b_conv_w, b_conv_b, b_dt_bias, b_a_log, b_d, b_norm, c_qnorm, c_knorm, c_fbias, m_norm, w_mkv, m_qnorm, m_knorm, w_pa, w_pb, w_pc, w_pm, w_out):
    nb, seq, d = x_prompt.shape
    ns, tdec, _ = x_sample.shape
    depth = g_norm.shape[0]
    n_mem = mem_prompt.shape[1]
    past = cache_c_k.shape[2]
    la_cache = cache_a_k.shape[2]
    mp = nb * seq
    md = ns * tdec
    assert seq % FOX_TK == 0 and seq % BAND_Q == 0 and seq % SSD_Q == 0
    assert tdec == CHUNK and la_cache == A_WIN and past % LANES == 0

    xp = x_prompt.reshape(mp, d)
    xs = x_sample.reshape(md, d)
    mem = mem_prompt.reshape(nb * n_mem, d)

    def feature_major(c):
        dd, n, t, h, e = c.shape
        return jnp.transpose(c, (0, 1, 3, 4, 2)).reshape(dd, n, h * e, t)

    cak, cav = feature_major(cache_a_k), feature_major(cache_a_v)
    cck, ccv = feature_major(cache_c_k), feature_major(cache_c_v)
    cmk, cmv = feature_major(cache_mem_k), feature_major(cache_mem_v)

    bd256 = _block_diag_mean(MXU_WIDTH)
    cum_consts = _cum_consts()
    expand, tri_q, qmask = _ssd_static_consts()
    scale = HEAD_DIM ** -0.5 * LOG2E

    outs = {k: [] for k in ('pa_k', 'pa_v', 'pc_k', 'pc_v', 'pc_f', 'pb_s', 'pb_c', 'pm_k', 'pm_v',
                            'sa_k', 'sa_v', 'sc_k', 'sc_v', 'sc_f', 'sb_s', 'sb_c')}

    for l in range(depth):
        wt = jnp.transpose(w_in[l])
        pieces = [wt[_SRC[n][0]:_SRC[n][0] + _SRC[n][1]] for n in _U_ORDER]
        used = sum(p.shape[0] for p in pieces)
        w_u = jnp.concatenate(pieces + [jnp.zeros((U_WIDTH - used, d), F32)], axis=0).astype(BF16)
        g0, gw = _SRC['gate']
        w_g = wt[g0:g0 + gw].astype(BF16)
        g_row = _row(g_norm[l])
        gains = (_row(jnp.tile(a_qnorm[l], A_HEADS)) * scale, _row(jnp.tile(a_knorm[l], A_HEADS)),
                 _row(jnp.tile(c_qnorm[l], C_HEADS)) * scale, _row(jnp.tile(c_knorm[l], C_HEADS)),
                 _row(jnp.tile(m_qnorm[l], M_HEADS)) * scale, _lane_row(c_fbias[l], CF_LANE), bd256)

        u_p, small_p = _proj(xp, g_row, w_u, act=None, out_dtype=BF16, name="proj_u", w_rows=True,
                             side_col=_U_OFF['b_dt'])
        gates_p = _proj(xp, g_row, w_g, act='sigmoid', out_dtype=BF16, name="proj_gate", w_rows=True)
        (a_qs_p, a_kb_p, a_vtb_p, pa_kt, pa_vt, c_qs_p, c_kb_p, pc_kt, pc_vt, c_vtb_p, m_qs_p, lf_p,
         pc_ft) = _prep_prompt(u_p, small_p, gains, nb, seq)
        u_s, small_s = _proj(xs, g_row, w_u, act=None, out_dtype=BF16, name="proj_u", w_rows=True,
                             side_col=_U_OFF['b_dt'])
        gates_s = _proj(xs, g_row, w_g, act='sigmoid', out_dtype=BF16, name="proj_gate", w_rows=True)
        (a_qs_s, a_kb_s, a_vb_s, sa_k4, sa_v4, c_qs_s, c_kb_s, c_vb_s, sc_k4, sc_v4, m_qs_s,
         lf_s) = _prep_sample(u_s, small_s, gains)

        kv = _proj(mem, _row(m_norm[l]), w_mkv[l].astype(BF16), act=None, out_dtype=F32, name="proj_mem",
                   w_rows=False)
        mkt, mvt = _memkv(kv, _row(jnp.tile(m_knorm[l], M_HEADS)), bd256, nb, n_mem)

        o_a_p = _band_prompt(a_qs_p, a_kb_p, a_vtb_p, _band_bias(a_rel[l], BAND_Q, BAND_K, True), u_p, nb, seq)
        bias_s = _band_bias(a_rel[l], tdec, la_cache + LANES, False)
        o_a_s = _band_sample(a_qs_s, a_kb_s, a_vb_s, cak, cav, l, bias_s, u_s, ns, tdec)

        cq_p, ck_p = _cum(lf_p.reshape(nb, seq, LANES), cum_consts)
        o_c_p = _fox_prompt(c_qs_p, c_kb_p, c_vtb_p, cq_p.reshape(mp, LANES), ck_p.reshape(mp, LANES), u_p, nb, seq)
        lf_cache = jnp.pad(cache_c_logf[l].astype(F32), ((0, 0), (0, 0), (CF_LANE, LANES - CF_LANE - C_HEADS)))
        lf_cat = jnp.concatenate([lf_cache, lf_s.reshape(ns, tdec, LANES),
                                  jnp.zeros((ns, LANES - tdec, LANES), F32)], axis=1)
        cq_s, ck_s = _cum(lf_cat, cum_consts)
        o_c_s = _fox_sample(c_qs_s, c_kb_s, c_vb_s, cq_s, ck_s, cck, ccv, l, u_s, ns, tdec)

        ssd_consts = (jnp.pad(b_conv_w[l], ((0, SUBLANES - B_CONV), (0, 0))), _row(b_conv_b[l]),
                      _lane_row(b_dt_bias[l], DT_LANE), _lane_row(b_a_log[l], DT_LANE),
                      _row(jnp.repeat(b_d[l], HEAD_DIM)), _row(b_norm[l]), expand, tri_q, qmask)
        o_b_p, hs_p, cs_p = _ssd(u_p, small_p, jnp.zeros((nb, CONV_PAD, B_XBC), F32),
                                 jnp.zeros((nb, B_WIDTH, LANES), F32), ssd_consts, nb, seq)
        cs0 = jnp.pad(state_b_conv[l].astype(F32), ((0, 0), (CONV_PAD - (B_CONV - 1), 0), (0, 0)))
        o_b_s, hs_s, cs_s = _ssd(u_s, small_s, cs0, _stack_state(state_b_ssm[l].astype(F32)), ssd_consts,
                                 ns, tdec)

        o_m_p = _cross(m_qs_p, mkt[None], mvt[None], 0, u_p, seq, "cross_prompt")
        o_m_s = _cross(m_qs_s, cmk, cmv, l, u_s, tdec, "cross_sample")

        w_merge = (w_pa[l].astype(BF16), w_pb[l].astype(BF16), w_pc[l].astype(BF16), w_pm[l].astype(BF16),
                   w_out[l].astype(BF16))
        xp = _merge(xp, gates_p, o_a_p, o_b_p, o_c_p, o_m_p, *w_merge)
        xs = _merge(xs, gates_s, o_a_s, o_b_s, o_c_s, o_m_s, *w_merge)

        outs['pa_k'].append(pa_kt)
        outs['pa_v'].append(pa_vt)
        outs['pc_k'].append(pc_kt)
        outs['pc_v'].append(pc_vt)
        outs['pc_f'].append(pc_ft)
        outs['pb_s'].append(_unstack_state(hs_p))
        outs['pb_c'].append(cs_p[:, CONV_PAD - (B_CONV - 1):])
        outs['pm_k'].append(mkt)
        outs['pm_v'].append(mvt)
        outs['sa_k'].append(sa_k4)
        outs['sa_v'].append(sa_v4)
        outs['sc_k'].append(sc_k4)
        outs['sc_v'].append(sc_v4)
        outs['sc_f'].append(lf_s[:, CF_LANE:CF_LANE + C_HEADS].reshape(ns, tdec, C_HEADS))
        outs['sb_s'].append(_unstack_state(hs_s))
        outs['sb_c'].append(cs_s[:, CONV_PAD - (B_CONV - 1):])

    st = jnp.stack

    def token_major(name, heads):
        a = st(outs[name])
        dd, n, _, t = a.shape
        return jnp.transpose(a.reshape(dd, n, heads, HEAD_DIM, t), (0, 1, 4, 2, 3))

    def sample_heads(name):
        return st(outs[name]).reshape(depth, ns, tdec, A_HEADS, HEAD_DIM)

    return (xp.reshape(nb, seq, d), xs.reshape(ns, tdec, d),
            token_major('pa_k', A_HEADS), token_major('pa_v', A_HEADS),
            token_major('pc_k', C_HEADS), token_major('pc_v', C_HEADS),
            jnp.transpose(st(outs['pc_f']), (0, 1, 3, 2)),
            st(outs['pb_s']), st(outs['pb_c']), token_major('pm_k', M_HEADS), token_major('pm_v', M_HEADS),
            sample_heads('sa_k'), sample_heads('sa_v'), sample_heads('sc_k'), sample_heads('sc_v'),
            st(outs['sc_f']), st(outs['sb_s']), st(outs['sb_c']))
```

```python
import functools

import numpy as np
import jax
import jax.numpy as jnp
from jax import lax
from jax.experimental import pallas as pl
from jax.experimental.pallas import tpu as pltpu

F32 = jnp.float32
BF16 = jnp.bfloat16

EPS = 1e-6
NEG = -1e30
LOG2E = 1.4426950408889634

HEAD_DIM = 64
CHUNK = 64
A_HEADS = 8
A_WIDTH = 512
A_LEFT_CHUNKS = 8
A_WIN = A_LEFT_CHUNKS * CHUNK
A_REL_CLIP = 128
B_HEADS = 8
B_WIDTH = 512
B_GROUPS = 2
B_STATE = 64
B_CONV = 4
B_XBC = B_WIDTH + 2 * B_GROUPS * B_STATE
C_HEADS = 8
C_WIDTH = 512
M_HEADS = 4
M_WIDTH = 256
N_BRANCH = 4

LANES = 128
SUBLANES = 8
MXU_WIDTH = 256
VMEM_LIMIT = 48 * 1024 * 1024

_SPLITS = (('a_q', A_WIDTH), ('a_k', A_WIDTH), ('a_v', A_WIDTH), ('a_z', A_WIDTH),
           ('b_z', B_WIDTH), ('b_xbc', B_XBC), ('b_dt', B_HEADS),
           ('c_q', C_WIDTH), ('c_k', C_WIDTH), ('c_v', C_WIDTH), ('c_f', C_HEADS), ('c_z', C_WIDTH),
           ('m_q', M_WIDTH), ('m_z', M_WIDTH), ('gate', N_BRANCH * 1024))
_SRC = {}
_off = 0
for _n, _w in _SPLITS:
    _SRC[_n] = (_off, _w)
    _off += _w

_U_ORDER = ('a_q', 'a_k', 'a_v', 'a_z', 'c_q', 'c_k', 'c_v', 'c_z', 'b_z', 'b_xbc', 'm_q', 'm_z', 'b_dt', 'c_f')
_U_OFF = {}
_off = 0
for _n in _U_ORDER:
    _U_OFF[_n] = _off
    _off += _SRC[_n][1]
U_WIDTH = 6144
DT_LANE = 0
CF_LANE = B_HEADS
AUG_PER_HEAD = 6


def _cp(sem):
    return pltpu.CompilerParams(dimension_semantics=sem, vmem_limit_bytes=VMEM_LIMIT)


def _pick(n, cands):
    for c in cands:
        if n % c == 0:
            return c
    raise ValueError(f"no tile for {n} in {cands}")


def _dot(a, b):
    return jnp.dot(a, b, preferred_element_type=F32)


def _dot_nt(a, b):
    return lax.dot_general(a, b, (((1,), (1,)), ((), ())), preferred_element_type=F32)


def _dot_tn(a, b):
    return lax.dot_general(a, b, (((0,), (0,)), ((), ())), preferred_element_type=F32)


def _split3(x):
    hi = x.astype(BF16)
    r = x - hi.astype(F32)
    mid = r.astype(BF16)
    lo = (r - mid.astype(F32)).astype(BF16)
    return hi, mid, lo


def _dot_sel(x, sel):
    hi, mid, lo = _split3(x)
    return _dot(hi, sel) + _dot(mid, sel) + _dot(lo, sel)


def _sel_dot(sel, x):
    hi, mid, lo = _split3(x)
    return _dot(sel, hi) + _dot(sel, mid) + _dot(sel, lo)


def _sigmoid(x):
    return 1.0 / (1.0 + jnp.exp(-x))


def _silu(x):
    return x * _sigmoid(x)


def _softplus(x):
    return jnp.maximum(x, 0.0) + jnp.log1p(jnp.exp(-jnp.abs(x)))


def _log_sigmoid(x):
    return jnp.minimum(x, 0.0) - jnp.log1p(jnp.exp(-jnp.abs(x)))


def _head_norm(x, bd):
    x2 = x * x
    hi = x2.astype(BF16)
    lo = (x2 - hi.astype(F32)).astype(BF16)
    w = bd.shape[0]
    ms = [_dot(hi[:, c:c + w], bd) + _dot(lo[:, c:c + w], bd) for c in range(0, x.shape[1], w)]
    ms = ms[0] if len(ms) == 1 else jnp.concatenate(ms, axis=1)
    return x * lax.rsqrt(ms + EPS)


def _lane_iota(shape):
    return lax.broadcasted_iota(jnp.int32, shape, len(shape) - 1)


def _row_iota(shape):
    return lax.broadcasted_iota(jnp.int32, shape, len(shape) - 2)


def _proj_kernel(x_ref, g_ref, w_ref, o_ref, *rest, act, w_rows, side):
    h_ref = rest[-1]

    @pl.when(pl.program_id(1) == 0)
    def _():
        x = x_ref[...]
        ms = jnp.mean(x * x, axis=-1, keepdims=True)
        h_ref[...] = (x * lax.rsqrt(ms + EPS) * g_ref[...]).astype(BF16)

    u = _dot_nt(h_ref[...], w_ref[...]) if w_rows else _dot(h_ref[...], w_ref[...])
    if side is not None:
        tile, col = side

        @pl.when(pl.program_id(1) == tile)
        def _():
            rest[0][...] = u[:, col:col + LANES]
    if act == 'sigmoid':
        u = _sigmoid(u)
    o_ref[...] = u.astype(o_ref.dtype)


def _proj(x, g_row, w_bf, *, act, out_dtype, name, w_rows, side_col=None):
    m, d = x.shape
    n = w_bf.shape[0] if w_rows else w_bf.shape[1]
    tm = _pick(m, (2048, 1024, 512, 256, 128))
    tn = _pick(n, (1024, 512, 256, 128))
    w_spec = pl.BlockSpec((tn, d), lambda i, j: (j, 0)) if w_rows else pl.BlockSpec((d, tn), lambda i, j: (0, j))
    side = None if side_col is None else (side_col // tn, side_col % tn)
    out_specs = [pl.BlockSpec((tm, tn), lambda i, j: (i, j))]
    out_shape = [jax.ShapeDtypeStruct((m, n), out_dtype)]
    if side is not None:
        out_specs.append(pl.BlockSpec((tm, LANES), lambda i, j: (i, 0)))
        out_shape.append(jax.ShapeDtypeStruct((m, LANES), F32))
    res = pl.pallas_call(
        functools.partial(_proj_kernel, act=act, w_rows=w_rows, side=side),
        grid=(m // tm, n // tn),
        in_specs=[pl.BlockSpec((tm, d), lambda i, j: (i, 0)),
                  pl.BlockSpec((1, d), lambda i, j: (0, 0)),
                  w_spec],
        out_specs=out_specs,
        out_shape=out_shape,
        scratch_shapes=[pltpu.VMEM((tm, d), BF16)],
        compiler_params=_cp(("parallel", "arbitrary")),
        name=name,
    )(x, g_row, w_bf)
    return res if side is not None else res[0]


PREP_TS = 512
VT_CHUNK = 256


def _prep_common(aq, ak, av, cq, ck, cv, mq, sm, gaq, gak, gcq, gck, gmq, fb, bd):
    bdv = bd[...]
    f32 = lambda ref: ref[...].astype(F32)
    return dict(
        a_qs=(_head_norm(f32(aq), bdv) * gaq[...]).astype(BF16),
        a_kn=_head_norm(f32(ak), bdv) * gak[...],
        a_v=f32(av),
        c_qs=(_head_norm(f32(cq), bdv) * gcq[...]).astype(BF16),
        c_kn=_head_norm(f32(ck), bdv) * gck[...],
        c_v=f32(cv),
        m_qs=(_head_norm(f32(mq), bdv) * gmq[...]).astype(BF16),
        lf=_log_sigmoid(sm[...] + fb[...]))


def _prep_prompt_kernel(aq, ak, av, cq, ck, cv, mq, sm, gaq, gak, gcq, gck, gmq, fb, bd, *refs, steps_per_seq):
    (o_aq, o_akb, o_avtb, o_akt, o_avt, o_cq, o_ckb, o_ckt, o_cvt, o_cvtb, o_mq, o_lf, o_lft) = refs[-13:]
    r = _prep_common(aq, ak, av, cq, ck, cv, mq, sm, gaq, gak, gcq, gck, gmq, fb, bd)
    o_aq[...] = r['a_qs']
    o_akb[...] = r['a_kn'].astype(BF16)
    avt = r['a_v'].T
    avtb = avt.astype(BF16)
    for c in range(PREP_TS // VT_CHUNK):
        o_avtb[c] = avtb[:, c * VT_CHUNK:(c + 1) * VT_CHUNK]

    @pl.when(pl.program_id(0) % steps_per_seq == steps_per_seq - 1)
    def _():
        o_akt[...] = r['a_kn'].T
        o_avt[...] = avt

    o_cq[...] = r['c_qs']
    o_ckb[...] = r['c_kn'].astype(BF16)
    o_ckt[...] = r['c_kn'].T
    cvt = r['c_v'].T
    o_cvt[...] = cvt
    cvtb = cvt.astype(BF16)
    for c in range(PREP_TS // VT_CHUNK):
        o_cvtb[c] = cvtb[:, c * VT_CHUNK:(c + 1) * VT_CHUNK]
    o_mq[...] = r['m_qs']
    o_lf[...] = r['lf']
    o_lft[...] = r['lf'].T[CF_LANE:CF_LANE + C_HEADS, :]


def _prep_sample_kernel(aq, ak, av, cq, ck, cv, mq, sm, gaq, gak, gcq, gck, gmq, fb, bd,
                        o_aq, o_akb, o_avb, o_ak4, o_av4, o_cq, o_ckb, o_cvb, o_ck4, o_cv4, o_mq, o_lf):
    r = _prep_common(aq, ak, av, cq, ck, cv, mq, sm, gaq, gak, gcq, gck, gmq, fb, bd)
    ts = o_aq.shape[0]
    o_aq[...] = r['a_qs']
    o_akb[...] = r['a_kn'].astype(BF16)
    o_avb[...] = r['a_v'].astype(BF16)
    o_ak4[...] = r['a_kn'].reshape(ts, A_HEADS, HEAD_DIM)
    o_av4[...] = r['a_v'].reshape(ts, A_HEADS, HEAD_DIM)
    o_cq[...] = r['c_qs']
    o_ckb[...] = r['c_kn'].astype(BF16)
    o_cvb[...] = r['c_v'].astype(BF16)
    o_ck4[...] = r['c_kn'].reshape(ts, C_HEADS, HEAD_DIM)
    o_cv4[...] = r['c_v'].reshape(ts, C_HEADS, HEAD_DIM)
    o_mq[...] = r['m_qs']
    o_lf[...] = r['lf']


def _prep_in_specs(ts):
    def ucol(name, width):
        idx = _U_OFF[name] // width
        assert idx * width == _U_OFF[name]
        return pl.BlockSpec((ts, width), lambda i: (i, idx))

    def row(width):
        return pl.BlockSpec((1, width), lambda i: (0, 0))

    return [ucol('a_q', 512), ucol('a_k', 512), ucol('a_v', 512),
            ucol('c_q', 512), ucol('c_k', 512), ucol('c_v', 512),
            ucol('m_q', 256), pl.BlockSpec((ts, LANES), lambda i: (i, 0)),
            row(512), row(512), row(512), row(512), row(256), row(LANES),
            pl.BlockSpec((MXU_WIDTH, MXU_WIDTH), lambda i: (0, 0))]


_CACHE_OUTS = (3, 4, 7, 8, 12)


def _prep_prompt(u, small, gains, nb, seq, layer, depth, prev):
    ts = PREP_TS
    assert seq % ts == 0 and min(A_WIN, seq) == ts
    sps = seq // ts
    m = nb * seq
    sds = jax.ShapeDtypeStruct
    rows = lambda width: pl.BlockSpec((ts, width), lambda i: (i, 0))
    last = pl.BlockSpec((None, None, 512, ts), lambda i: (layer, i // sps, 0, 0))
    feat = pl.BlockSpec((None, None, 512, ts), lambda i: (layer, i // sps, 0, i % sps))
    nvc = ts // VT_CHUNK
    chunks = pl.BlockSpec((None, nvc, 512, VT_CHUNK), lambda i: (i // sps, i % sps, 0, 0))
    out_specs = [rows(512), rows(512), chunks, last, last,
                 rows(512), rows(512), feat, feat, chunks,
                 rows(256), rows(LANES),
                 pl.BlockSpec((None, None, C_HEADS, ts), lambda i: (layer, i // sps, 0, i % sps))]
    out_shape = [sds((m, 512), BF16), sds((m, 512), BF16), sds((nb, seq // VT_CHUNK, 512, VT_CHUNK), BF16),
                 sds((depth, nb, 512, ts), F32), sds((depth, nb, 512, ts), F32),
                 sds((m, 512), BF16), sds((m, 512), BF16),
                 sds((depth, nb, 512, seq), F32), sds((depth, nb, 512, seq), F32),
                 sds((nb, seq // VT_CHUNK, 512, VT_CHUNK), BF16),
                 sds((m, 256), BF16), sds((m, LANES), F32), sds((depth, nb, C_HEADS, seq), F32)]
    in_specs = _prep_in_specs(ts)
    extra = ()
    aliases = {}
    if prev is not None:
        extra = tuple(prev)
        aliases = {len(in_specs) + k: pos for k, pos in enumerate(_CACHE_OUTS)}
        in_specs = in_specs + [pl.BlockSpec(memory_space=pl.ANY)] * len(extra)
    return pl.pallas_call(
        functools.partial(_prep_prompt_kernel, steps_per_seq=sps),
        grid=(m // ts,),
        in_specs=in_specs,
        out_specs=out_specs,
        out_shape=out_shape,
        input_output_aliases=aliases,
        compiler_params=_cp(("arbitrary",)),
        name="prep_prompt",
    )(*([u] * 7), small, *gains, *extra)


def _prep_sample(u, small, gains):
    m = u.shape[0]
    ts = _pick(m, (512, 256, 128, 64))
    sds = jax.ShapeDtypeStruct
    rows = lambda width: pl.BlockSpec((ts, width), lambda i: (i, 0))
    heads = pl.BlockSpec((ts, A_HEADS, HEAD_DIM), lambda i: (i, 0, 0))
    return pl.pallas_call(
        _prep_sample_kernel,
        grid=(m // ts,),
        in_specs=_prep_in_specs(ts),
        out_specs=[rows(512), rows(512), rows(512), heads, heads,
                   rows(512), rows(512), rows(512), heads, heads, rows(256), rows(LANES)],
        out_shape=[sds((m, 512), BF16), sds((m, 512), BF16), sds((m, 512), BF16),
                   sds((m, A_HEADS, HEAD_DIM), F32), sds((m, A_HEADS, HEAD_DIM), F32),
                   sds((m, 512), BF16), sds((m, 512), BF16), sds((m, 512), BF16),
                   sds((m, C_HEADS, HEAD_DIM), F32), sds((m, C_HEADS, HEAD_DIM), F32),
                   sds((m, 256), BF16), sds((m, LANES), F32)],
        compiler_params=_cp(("parallel",)),
        name="prep_sample",
    )(*([u] * 7), small, *gains)


def _memkv_kernel(kv, g, bd, o_kt, o_vt):
    o_kt[...] = (_head_norm(kv[:, :M_WIDTH], bd[...]) * g[...]).T
    o_vt[...] = kv[:, M_WIDTH:].T


def _memkv(kv, g_row, bd256, nb, n_mem):
    spec = pl.BlockSpec((None, M_WIDTH, n_mem), lambda b: (b, 0, 0))
    return pl.pallas_call(
        _memkv_kernel,
        grid=(nb,),
        in_specs=[pl.BlockSpec((n_mem, 2 * M_WIDTH), lambda b: (b, 0)),
                  pl.BlockSpec((1, M_WIDTH), lambda b: (0, 0)),
                  pl.BlockSpec((M_WIDTH, M_WIDTH), lambda b: (0, 0))],
        out_specs=[spec, spec],
        out_shape=[jax.ShapeDtypeStruct((nb, M_WIDTH, n_mem), F32)] * 2,
        compiler_params=_cp(("parallel",)),
        name="memkv",
    )(kv, g_row, bd256)


def _cum_kernel(lf_ref, tri_ref, selq_ref, selk_ref, qc_ref, kc_ref, low_ref, cq_ref, ck_ref, carry_ref,
                *, nchunk, unroll):
    ncp = carry_ref.shape[0]
    totals = lf_ref[...].reshape(nchunk, LANES, LANES).sum(axis=1)
    if ncp > nchunk:
        totals = jnp.concatenate([totals, jnp.zeros((ncp - nchunk, LANES), F32)], axis=0)
    carry_ref[...] = _sel_dot(low_ref[...], totals)
    tri = tri_ref[...]

    def body(cc, carry):
        cs = [cc * unroll + k for k in range(unroll)]
        rs = [pl.multiple_of(c * LANES, LANES) for c in cs]
        xs = [lf_ref[pl.ds(r, LANES), :] for r in rs]
        zs = [(_sel_dot(tri, x) + carry_ref[pl.ds(c, 1), :]) * LOG2E for x, c in zip(xs, cs)]
        parts = [_split3(z) for z in zs]
        qs = [_dot(p[0], selq_ref[0]) + _dot(p[1], selq_ref[1]) + _dot(p[2], selq_ref[2]) + qc_ref[...]
              for p in parts]
        ks = [_dot(p[0], selk_ref[0]) + _dot(p[1], selk_ref[1]) + _dot(p[2], selk_ref[2]) + kc_ref[...]
              for p in parts]
        for r, qv, kv in zip(rs, qs, ks):
            cq_ref[pl.ds(r, LANES), :] = qv.astype(BF16)
            ck_ref[pl.ds(r, LANES), :] = kv.astype(BF16)
        return carry

    lax.fori_loop(0, nchunk // unroll, body, 0)


CUM_UNROLLS = (8, 4, 3, 2, 1)


def _cum(lf, consts):
    nseq, length, _ = lf.shape
    assert length % LANES == 0
    nchunk = length // LANES
    unroll = _pick(nchunk, CUM_UNROLLS)
    ncp = -(-nchunk // SUBLANES) * SUBLANES
    tri, selq, selk, qc, kc = consts
    low = jnp.asarray(np.tril(np.ones((ncp, ncp), np.float32), -1), BF16)
    full = lambda shape: pl.BlockSpec(shape, lambda s: (0,) * len(shape))
    seq = pl.BlockSpec((None, length, LANES), lambda s: (s, 0, 0))
    return pl.pallas_call(
        functools.partial(_cum_kernel, nchunk=nchunk, unroll=unroll),
        grid=(nseq,),
        in_specs=[seq, full((LANES, LANES)), full((3, LANES, LANES)), full((3, LANES, LANES)),
                  full((1, LANES)), full((1, LANES)), full((ncp, ncp))],
        out_specs=[seq, seq],
        out_shape=[jax.ShapeDtypeStruct(lf.shape, BF16)] * 2,
        scratch_shapes=[pltpu.VMEM((ncp, LANES), F32)],
        compiler_params=_cp(("parallel",)),
        name="cum",
    )(lf, tri, selq, selk, qc, kc, low)


def _cum_consts():
    tri = np.tril(np.ones((LANES, LANES), np.float32))
    selq = np.zeros((3, LANES, LANES), np.float32)
    selk = np.zeros((3, LANES, LANES), np.float32)
    qc = np.zeros((1, LANES), np.float32)
    kc = np.zeros((1, LANES), np.float32)
    for h in range(C_HEADS):
        for p in range(3):
            selq[p, CF_LANE + h, AUG_PER_HEAD * h + p] = 1.0
            selk[p, CF_LANE + h, AUG_PER_HEAD * h + 3 + p] = -1.0
            qc[0, AUG_PER_HEAD * h + 3 + p] = 1.0
            kc[0, AUG_PER_HEAD * h + p] = 1.0
    return (jnp.asarray(tri, BF16), jnp.asarray(selq, BF16), jnp.asarray(selk, BF16),
            jnp.asarray(qc), jnp.asarray(kc))


BAND_Q = 4 * CHUNK
BAND_K = 3 * BAND_Q
TAB_PAD = 384
DIAG_PAD = 1024


def _bias_kernel(tab_ref, idx_ref, neg_ref, o_ref):
    nrows, ncols = neg_ref.shape
    onehot = (lax.broadcasted_iota(jnp.int32, (TAB_PAD, DIAG_PAD), 0) == idx_ref[...]).astype(BF16)
    diag = _dot_sel(tab_ref[...], onehot) * LOG2E
    for h in range(A_HEADS):
        rows = jnp.broadcast_to(diag[h:h + 1, :], (nrows, DIAG_PAD))
        o_ref[h] = pltpu.roll(rows, 0, 1, stride=1, stride_axis=0)[:, :ncols] + neg_ref[...]


def _band_bias(table, nq, nk, key_major):
    assert nq + nk <= DIAG_PAD
    r = np.arange(nq)[None, :] if key_major else np.arange(nq)[:, None]
    t = np.arange(nk)[:, None] if key_major else np.arange(nk)[None, :]
    cb = t // CHUNK - r // CHUNK
    neg = np.where((cb >= 0) & (cb <= A_LEFT_CHUNKS), 0.0, NEG).astype(np.float32)
    p = np.arange(DIAG_PAD)
    ncols = nq if key_major else nk
    col_minus_row = np.where(p < ncols, p, p - DIAG_PAD)
    q_minus_k = col_minus_row if key_major else -col_minus_row
    idx = (np.clip(A_WIN + q_minus_k, -A_REL_CLIP, A_REL_CLIP) + A_REL_CLIP).astype(np.int32).reshape(1, -1)
    tab = jnp.zeros((A_HEADS, TAB_PAD), F32).at[:, :table.shape[0]].set(table.T)
    full = lambda shape: pl.BlockSpec(shape, lambda: (0,) * len(shape))
    return pl.pallas_call(
        _bias_kernel,
        in_specs=[full((A_HEADS, TAB_PAD)), full((1, DIAG_PAD)), full(neg.shape)],
        out_specs=full((A_HEADS,) + neg.shape),
        out_shape=jax.ShapeDtypeStruct((A_HEADS,) + neg.shape, F32),
        compiler_params=pltpu.CompilerParams(vmem_limit_bytes=VMEM_LIMIT),
        name="band_bias",
    )(tab, jnp.asarray(idx), jnp.asarray(neg))


def _half_mask(par):
    lane = _lane_iota((1, LANES))
    return (lane < HEAD_DIM) if par == 0 else (lane >= HEAD_DIM)


def _softmax_pv_many(s_lists, v_lists, v_feature_major=None):
    n = len(s_lists)
    nblk = len(s_lists[0])
    if v_feature_major is None:
        v_feature_major = (False,) * nblk
    ms = []
    for sb in s_lists:
        m = sb[0].max(axis=1, keepdims=True)
        for s in sb[1:]:
            m = jnp.maximum(m, s.max(axis=1, keepdims=True))
        ms.append(m)
    ps = [[jnp.exp2(s - ms[i]) for s in s_lists[i]] for i in range(n)]
    ls = []
    for i in range(n):
        l = ps[i][0].sum(axis=1, keepdims=True)
        for p in ps[i][1:]:
            l = l + p.sum(axis=1, keepdims=True)
        ls.append(l)
    outs = []
    for i in range(n):
        acc = None
        for p, v, fm in zip(ps[i], v_lists[i], v_feature_major):
            pv = _dot_nt(p.astype(BF16), v) if fm else _dot(p.astype(BF16), v)
            acc = pv if acc is None else acc + pv
        outs.append(acc)
    return [o / l for o, l in zip(outs, ls)]


ONES_ROWS = 16


def _band_prompt_kernel(q_ref, k0, k1, k2, v0, v1, v2, bias_ref, z_ref, o_ref):
    g = pl.program_id(1)
    krefs = (k0, k1, k2)
    vrefs = (v0, v1, v2)
    s_lists = []
    for hp in range(A_HEADS // 2):
        cols = slice(hp * LANES, (hp + 1) * LANES)
        qp = q_ref[:, cols]
        ks = [kr[:, cols] for kr in krefs]
        for par in range(2):
            h = 2 * hp + par
            qm = jnp.where(_half_mask(par), qp, jnp.zeros_like(qp))
            sb = []
            for j in range(3):
                s = _dot_nt(ks[j], qm) + bias_ref[h, j * BAND_Q:(j + 1) * BAND_Q, :]
                if j < 2:
                    s = jnp.where(g + j >= 2, s, NEG)
                sb.append(s)
            s_lists.append(sb)
    ms = []
    for sb in s_lists:
        m = sb[0].max(axis=0, keepdims=True)
        for s in sb[1:]:
            m = jnp.maximum(m, s.max(axis=0, keepdims=True))
        ms.append(m)
    ps = [[jnp.exp2(s - m).astype(BF16) for s in sb] for sb, m in zip(s_lists, ms)]
    ones = jnp.ones((ONES_ROWS, BAND_Q), BF16)
    outs = []
    for h in range(A_HEADS):
        acc = None
        for j in range(3):
            vt = jnp.concatenate([vrefs[j][h * HEAD_DIM:(h + 1) * HEAD_DIM, :], ones], axis=0)
            d = _dot(vt, ps[h][j])
            acc = d if acc is None else acc + d
        outs.append(acc[:HEAD_DIM] / acc[HEAD_DIM:HEAD_DIM + 1])
    o_ref[...] = (jnp.concatenate(outs, axis=0).T * _silu(z_ref[...].astype(F32))).astype(BF16)


def _band_prompt(qs, kb, vtb, bias_t, u, nb, seq):
    assert VT_CHUNK == BAND_Q
    ng = seq // BAND_Q

    def kblk(j):
        return pl.BlockSpec((BAND_Q, A_WIDTH), lambda b, g: (b * ng + jnp.maximum(g - 2 + j, 0), 0))

    def vblk(j):
        return pl.BlockSpec((None, None, A_WIDTH, BAND_Q), lambda b, g: (b, jnp.maximum(g - 2 + j, 0), 0, 0))

    cur = pl.BlockSpec((BAND_Q, A_WIDTH), lambda b, g: (b * ng + g, 0))
    zidx = _U_OFF['a_z'] // A_WIDTH
    return pl.pallas_call(
        _band_prompt_kernel,
        grid=(nb, ng),
        in_specs=[cur, kblk(0), kblk(1), kblk(2), vblk(0), vblk(1), vblk(2),
                  pl.BlockSpec((A_HEADS, BAND_K, BAND_Q), lambda b, g: (0, 0, 0)),
                  pl.BlockSpec((BAND_Q, A_WIDTH), lambda b, g: (b * ng + g, zidx))],
        out_specs=cur,
        out_shape=jax.ShapeDtypeStruct((nb * seq, A_WIDTH), BF16),
        compiler_params=_cp(("parallel", "parallel")),
        name="band_prompt",
    )(qs, kb, kb, kb, vtb, vtb, vtb, bias_t, u)


def _band_sample_kernel(q_ref, kc_ref, kn_ref, vc_ref, vn_ref, bias_ref, z_ref, o_ref, *, t):
    lane = _lane_iota((1, LANES))
    lc = kc_ref.shape[1]
    pad = jnp.zeros((LANES - t, LANES), BF16)
    s_lists, v_lists = [], []
    for hp in range(A_HEADS // 2):
        cols = slice(hp * LANES, (hp + 1) * LANES)
        qp = q_ref[:, cols]
        kc = kc_ref[cols, :].astype(BF16)
        vc = vc_ref[cols, :].astype(BF16)
        kn = jnp.concatenate([kn_ref[:, cols], pad], axis=0)
        vn = jnp.concatenate([vn_ref[:, cols], pad], axis=0)
        for par in range(2):
            h = 2 * hp + par
            qm = jnp.where(_half_mask(par), qp, jnp.zeros_like(qp))
            s_lists.append([_dot(qm, kc) + bias_ref[h, :, :lc], _dot_nt(qm, kn) + bias_ref[h, :, lc:]])
            v_lists.append([vc, vn])
    outs = _softmax_pv_many(s_lists, v_lists, (True, False))
    for hp in range(A_HEADS // 2):
        cols = slice(hp * LANES, (hp + 1) * LANES)
        o = jnp.where(lane < HEAD_DIM, outs[2 * hp], outs[2 * hp + 1])
        o_ref[:, cols] = (o * _silu(z_ref[:, cols].astype(F32))).astype(BF16)


def _band_sample(qs, kb, vb, cache_kt, cache_vt, layer, bias_s, u, nseq, t):
    lc = cache_kt.shape[3]
    new = pl.BlockSpec((t, A_WIDTH), lambda s: (s, 0))
    cache = pl.BlockSpec((None, None, A_WIDTH, lc), lambda s: (layer, s, 0, 0))
    return pl.pallas_call(
        functools.partial(_band_sample_kernel, t=t),
        grid=(nseq,),
        in_specs=[new, cache, new, cache, new,
                  pl.BlockSpec(bias_s.shape, lambda s: (0, 0, 0)),
                  pl.BlockSpec((t, A_WIDTH), lambda s: (s, _U_OFF['a_z'] // A_WIDTH))],
        out_specs=new,
        out_shape=jax.ShapeDtypeStruct((nseq * t, A_WIDTH), BF16),
        compiler_params=_cp(("parallel",)),
        name="band_sample",
    )(qs, cache_kt, kb, cache_vt, vb, bias_s, u)


FOX_TQ = 512
FOX_TK = 512


def _aug_mask(h):
    lane = _lane_iota((1, LANES))
    return (lane >= AUG_PER_HEAD * h) & (lane < AUG_PER_HEAD * (h + 1))


def _fox_prompt_kernel(q_ref, cq_ref, k_ref, ck_ref, vt_ref, z_ref, o_ref, st_ref, m_ref, acc_ref):
    hp = pl.program_id(1)
    i = pl.program_id(2)
    qp = q_ref[...]
    cq = cq_ref[...]
    qcs = []
    for par in range(2):
        qm = jnp.where(_half_mask(par), qp, jnp.zeros_like(qp))
        qa = jnp.where(_aug_mask(2 * hp + par), cq, jnp.zeros_like(cq))
        qcs.append(jnp.concatenate([qm, qa], axis=1))
    qcat = jnp.concatenate(qcs, axis=0)
    m_ref[...] = jnp.full(m_ref.shape, NEG, F32)
    acc_ref[...] = jnp.zeros(acc_ref.shape, F32)
    ratio = FOX_TK // FOX_TQ
    nvc = FOX_TK // VT_CHUNK
    ones = jnp.ones((ONES_ROWS, VT_CHUNK), BF16)

    def scores(j, slot):
        off = pl.multiple_of(j * FOX_TK, FOX_TK)
        kc = jnp.concatenate([k_ref[pl.ds(off, FOX_TK), :], ck_ref[pl.ds(off, FOX_TK), :]], axis=1)
        st_ref[slot] = _dot_nt(kc, qcat)

    def reduce(j, slot, masked):
        st = st_ref[slot]
        if masked:
            shape = (FOX_TK, 2 * FOX_TQ)
            qpos = (_lane_iota(shape) & (FOX_TQ - 1)) + (i % ratio) * FOX_TQ
            st = jnp.where(_row_iota(shape) <= qpos, st, NEG)
        m_old = m_ref[...]
        m_new = jnp.maximum(m_old, st.max(axis=0, keepdims=True))
        alpha = jnp.exp2(m_old - m_new)
        pb = jnp.exp2(st - m_new).astype(BF16)
        for par in range(2):
            cols = slice(par * FOX_TQ, (par + 1) * FOX_TQ)
            pv = None
            for c in range(nvc):
                vt = jnp.concatenate([vt_ref[j * nvc + c, par * HEAD_DIM:(par + 1) * HEAD_DIM, :], ones], axis=0)
                d = _dot(vt, pb[c * VT_CHUNK:(c + 1) * VT_CHUNK, cols])
                pv = d if pv is None else pv + d
            acc_ref[par] = alpha[:, cols] * acc_ref[par] + pv
        m_ref[...] = m_new

    nfull = i // ratio
    scores(0, 0)

    def body(jj, carry):
        j = 2 * jj
        scores(j + 1, 1)
        reduce(j, 0, False)
        scores(j + 2, 0)
        reduce(j + 1, 1, False)
        return carry

    lax.fori_loop(0, nfull // 2, body, 0)

    @pl.when(nfull % 2 == 0)
    def _():
        reduce(nfull, 0, True)

    @pl.when(nfull % 2 == 1)
    def _():
        scores(nfull, 1)
        reduce(nfull - 1, 0, False)
        reduce(nfull, 1, True)

    ot = jnp.concatenate([acc_ref[par, :HEAD_DIM] / acc_ref[par, HEAD_DIM:HEAD_DIM + 1] for par in range(2)],
                         axis=0)
    o_ref[...] = (ot.T * _silu(z_ref[...].astype(F32))).astype(BF16)


def _fox_prompt(qs, kb, vtb, cq, ck, u, nb, seq):
    nq = seq // FOX_TQ
    qspec = pl.BlockSpec((FOX_TQ, LANES), lambda b, hp, i: (b * nq + i, hp))
    cqspec = pl.BlockSpec((FOX_TQ, LANES), lambda b, hp, i: (b * nq + i, 0))
    kspec = pl.BlockSpec((seq, LANES), lambda b, hp, i: (b, hp))
    ckspec = pl.BlockSpec((seq, LANES), lambda b, hp, i: (b, 0))
    vspec = pl.BlockSpec((None, seq // VT_CHUNK, LANES, VT_CHUNK), lambda b, hp, i: (b, 0, hp, 0))
    z0 = _U_OFF['c_z'] // LANES
    zspec = pl.BlockSpec((FOX_TQ, LANES), lambda b, hp, i: (b * nq + i, z0 + hp))
    return pl.pallas_call(
        _fox_prompt_kernel,
        grid=(nb, C_HEADS // 2, nq),
        in_specs=[qspec, cqspec, kspec, ckspec, vspec, zspec],
        out_specs=qspec,
        out_shape=jax.ShapeDtypeStruct((nb * seq, C_WIDTH), BF16),
        scratch_shapes=[pltpu.VMEM((2, FOX_TK, 2 * FOX_TQ), F32),
                        pltpu.VMEM((1, 2 * FOX_TQ), F32),
                        pltpu.VMEM((2, HEAD_DIM + ONES_ROWS, FOX_TQ), F32)],
        compiler_params=_cp(("parallel", "parallel", "arbitrary")),
        name="fox_prompt",
    )(qs, cq, kb, ck, vtb, u)


def _fox_sample_kernel(q_ref, cq_ref, kc_ref, kn_ref, ck_ref, vc_ref, vn_ref, z_ref, o_ref, *, t, past):
    lane = _lane_iota((1, LANES))
    cq = cq_ref[past:past + t, :]
    pad = jnp.zeros((LANES - t, LANES), BF16)
    ck_cache_t = ck_ref[:past, :].astype(F32).T.astype(BF16)
    ck_new = ck_ref[past:, :]
    vis = _lane_iota((t, LANES)) <= _row_iota((t, LANES))
    s_lists, v_lists = [], []
    for hp in range(C_HEADS // 2):
        cols = slice(hp * LANES, (hp + 1) * LANES)
        qp = q_ref[:, cols]
        kc = jnp.concatenate([kc_ref[cols, :].astype(BF16), ck_cache_t], axis=0)
        kn = jnp.concatenate([jnp.concatenate([kn_ref[:, cols], pad], axis=0), ck_new], axis=1)
        vc = vc_ref[cols, :].astype(BF16)
        vn = jnp.concatenate([vn_ref[:, cols], pad], axis=0)
        for par in range(2):
            qm = jnp.where(_half_mask(par), qp, jnp.zeros_like(qp))
            qa = jnp.where(_aug_mask(2 * hp + par), cq, jnp.zeros_like(cq))
            qc = jnp.concatenate([qm, qa], axis=1)
            s_lists.append([_dot(qc, kc), jnp.where(vis, _dot_nt(qc, kn), NEG)])
            v_lists.append([vc, vn])
    outs = _softmax_pv_many(s_lists, v_lists, (True, False))
    for hp in range(C_HEADS // 2):
        cols = slice(hp * LANES, (hp + 1) * LANES)
        o = jnp.where(lane < HEAD_DIM, outs[2 * hp], outs[2 * hp + 1])
        o_ref[:, cols] = (o * _silu(z_ref[:, cols].astype(F32))).astype(BF16)


def _fox_sample(qs, kb, vb, cq, ck, cache_kt, cache_vt, layer, u, nseq, t):
    past = cache_kt.shape[3]
    new = pl.BlockSpec((t, C_WIDTH), lambda s: (s, 0))
    cache = pl.BlockSpec((None, None, C_WIDTH, past), lambda s: (layer, s, 0, 0))
    aug = pl.BlockSpec((None, past + LANES, LANES), lambda s: (s, 0, 0))
    return pl.pallas_call(
        functools.partial(_fox_sample_kernel, t=t, past=past),
        grid=(nseq,),
        in_specs=[new, aug, cache, new, aug, cache, new,
                  pl.BlockSpec((t, C_WIDTH), lambda s: (s, _U_OFF['c_z'] // C_WIDTH))],
        out_specs=new,
        out_shape=jax.ShapeDtypeStruct((nseq * t, C_WIDTH), BF16),
        compiler_params=_cp(("parallel",)),
        name="fox_sample",
    )(qs, cq, cache_kt, kb, ck, cache_vt, vb, u)


def _cross_kernel(q_ref, k_ref, v_ref, z_ref, o_ref):
    lane = _lane_iota((1, LANES))
    s_lists, v_lists = [], []
    for hp in range(M_HEADS // 2):
        cols = slice(hp * LANES, (hp + 1) * LANES)
        qp = q_ref[:, cols]
        k = k_ref[cols, :].astype(BF16)
        v = v_ref[cols, :].astype(BF16)
        for par in range(2):
            qm = jnp.where(_half_mask(par), qp, jnp.zeros_like(qp))
            s_lists.append([_dot(qm, k)])
            v_lists.append([v])
    outs = _softmax_pv_many(s_lists, v_lists, (True,))
    for hp in range(M_HEADS // 2):
        cols = slice(hp * LANES, (hp + 1) * LANES)
        o = jnp.where(lane < HEAD_DIM, outs[2 * hp], outs[2 * hp + 1])
        o_ref[:, cols] = (o * _silu(z_ref[:, cols].astype(F32))).astype(BF16)


def _cross(qs, mkt, mvt, layer, u, rows_per_seq, name):
    nrows = qs.shape[0]
    n_mem = mkt.shape[3]
    tq = _pick(rows_per_seq, (512, 256, 128, 64))
    per = rows_per_seq // tq
    mem = pl.BlockSpec((None, None, M_WIDTH, n_mem), lambda i: (layer, i // per, 0, 0))
    rows = pl.BlockSpec((tq, M_WIDTH), lambda i: (i, 0))
    return pl.pallas_call(
        _cross_kernel,
        grid=(nrows // tq,),
        in_specs=[rows, mem, mem, pl.BlockSpec((tq, M_WIDTH), lambda i: (i, _U_OFF['m_z'] // M_WIDTH))],
        out_specs=rows,
        out_shape=jax.ShapeDtypeStruct((nrows, M_WIDTH), BF16),
        compiler_params=_cp(("parallel",)),
        name=name,
    )(qs, mkt, mvt, u)


SSD_Q = 128
CONV_PAD = SUBLANES
SSD_GROUP = 4


def _ssd_kernel(xbc_ref, z_ref, sm_ref, cs0_ref, hs0_ref, cw_ref, cb_ref, dtb_ref, alog_ref, dsk_ref,
                gn_ref, exp_ref, tri_ref, qm_ref,
                o_ref, hs_out_ref, cs_out_ref, hs_ref, xext_ref, *, nv, nchunk):
    c = pl.program_id(1)
    q = SSD_Q
    grp_ids = range(SSD_GROUP)
    lane = _lane_iota((1, LANES))
    head_lane = (lane >= DT_LANE) & (lane < DT_LANE + B_HEADS)
    a_row = jnp.where(head_lane, -jnp.exp(alog_ref[...]), 0.0)
    causal = _lane_iota((q, q)) <= _row_iota((q, q))
    heads_per_group = B_HEADS // B_GROUPS
    expand = exp_ref[...]
    tri = tri_ref[...]
    qm = qm_ref[...]

    @pl.when(c == 0)
    def _():
        for g in grp_ids:
            hs_ref[g] = hs0_ref[g].T * qm
            xext_ref[g, 0:CONV_PAD, :] = cs0_ref[g]

    for g in grp_ids:
        xext_ref[g, CONV_PAD:CONV_PAD + nv, :] = xbc_ref[g].astype(F32)
        if nv < q:
            xext_ref[g, CONV_PAD + nv:CONV_PAD + q, :] = jnp.zeros((q - nv, B_XBC), F32)
    xa = []
    for g in grp_ids:
        y = cb_ref[...]
        for tap in range(B_CONV):
            y = y + xext_ref[g, pl.ds(CONV_PAD - (B_CONV - 1) + tap, q), :] * cw_ref[tap:tap + 1, :]
        xa.append(_silu(y))
    for g in grp_ids:
        tail = xext_ref[g, nv:nv + CONV_PAD, :]
        xext_ref[g, 0:CONV_PAD, :] = tail
        cs_out_ref[g] = tail
    xs = [x[:, :B_WIDTH] for x in xa]
    bm = [x[:, B_WIDTH:B_WIDTH + LANES].astype(BF16) for x in xa]
    cm = [x[:, B_WIDTH + LANES:] for x in xa]

    dt = []
    for g in grp_ids:
        d = _softplus(sm_ref[g] + dtb_ref[...])
        if nv < q:
            d = jnp.concatenate([d, jnp.zeros((q - nv, LANES), F32)], axis=0)
        dt.append(jnp.where(head_lane, d, 0.0))
    acum = [_sel_dot(tri, d * a_row) for d in dt]
    dt_e = [_dot_sel(d, expand) for d in dt]
    acum_e = [_dot_sel(a, expand) for a in acum]
    alast_e = [a[q - 1:q, :] for a in acum_e]
    xdt = [x * d for x, d in zip(xs, dt_e)]

    hs = [hs_ref[g] for g in grp_ids]
    y_off = [_dot(cm[g].astype(BF16), hs[g].astype(BF16)) * jnp.exp(acum_e[g]) for g in grp_ids]
    upd = [_dot_tn(bm[g], (xdt[g] * jnp.exp(alast_e[g] - acum_e[g])).astype(BF16)) for g in grp_ids]
    hs_new = [(jnp.exp(alast_e[g]) * hs[g] + upd[g]) * qm for g in grp_ids]
    for g in grp_ids:
        hs_ref[g] = hs_new[g]

    @pl.when(c == nchunk - 1)
    def _():
        for g in grp_ids:
            hs_out_ref[g] = hs_new[g].T

    acum_t = [a.T for a in acum]
    xdt_b = [x.astype(BF16) for x in xdt]
    cbm = [[None] * B_GROUPS for _ in grp_ids]
    for grp in range(B_GROUPS):
        for g in grp_ids:
            cg = jnp.where(_half_mask(grp), cm[g], 0.0).astype(BF16)
            cbm[g][grp] = _dot_nt(cg, bm[g])
    y_pairs = [[] for _ in grp_ids]
    for hp in range(B_HEADS // 2):
        pair = [[] for _ in grp_ids]
        for par in range(2):
            h = 2 * hp + par
            for g in grp_ids:
                seg = acum[g][:, DT_LANE + h:DT_LANE + h + 1] - acum_t[g][DT_LANE + h:DT_LANE + h + 1, :]
                lmat = jnp.exp(jnp.where(causal, seg, -jnp.inf))
                m = (cbm[g][h // heads_per_group] * lmat).astype(BF16)
                pair[g].append(_dot(m, xdt_b[g][:, hp * LANES:(hp + 1) * LANES]))
        for g in grp_ids:
            y_pairs[g].append(jnp.where(lane < HEAD_DIM, pair[g][0], pair[g][1]))

    for g in grp_ids:
        yt = y_off[g] + jnp.concatenate(y_pairs[g], axis=1) + dsk_ref[...] * xs[g]
        if nv < q:
            yt = yt[:nv]
        yz = yt * _silu(z_ref[g].astype(F32))
        ms = jnp.mean(yz * yz, axis=-1, keepdims=True)
        o_ref[g] = (yz * lax.rsqrt(ms + EPS) * gn_ref[...]).astype(BF16)


def _ssd(u, small, cs0, hs0, consts, nseq, rows_per_seq):
    nv = min(SSD_Q, rows_per_seq)
    nchunk = rows_per_seq // nv
    gsz = SSD_GROUP
    assert nseq % gsz == 0
    cw, cb, dtb, alog, dsk, gn, expand, tri, qmask = consts
    u3 = u.reshape(nseq, rows_per_seq, U_WIDTH)
    small3 = small.reshape(nseq, rows_per_seq, LANES)

    def ucol(name, width):
        idx = _U_OFF[name] // width
        assert idx * width == _U_OFF[name]
        return pl.BlockSpec((gsz, nv, width), lambda s, c: (s, c, idx))

    full = lambda a: pl.BlockSpec(a.shape, lambda s, c: (0,) * a.ndim)
    cs_spec = pl.BlockSpec((gsz, CONV_PAD, B_XBC), lambda s, c: (s, 0, 0))
    hs_spec = pl.BlockSpec((gsz, B_WIDTH, LANES), lambda s, c: (s, 0, 0))
    o, hs, cs = pl.pallas_call(
        functools.partial(_ssd_kernel, nv=nv, nchunk=nchunk),
        grid=(nseq // gsz, nchunk),
        in_specs=[ucol('b_xbc', B_XBC), ucol('b_z', B_WIDTH),
                  pl.BlockSpec((gsz, nv, LANES), lambda s, c: (s, c, 0)), cs_spec, hs_spec,
                  full(cw), full(cb), full(dtb), full(alog), full(dsk), full(gn), full(expand), full(tri),
                  full(qmask)],
        out_specs=[pl.BlockSpec((gsz, nv, B_WIDTH), lambda s, c: (s, c, 0)), hs_spec, cs_spec],
        out_shape=[jax.ShapeDtypeStruct((nseq, rows_per_seq, B_WIDTH), BF16),
                   jax.ShapeDtypeStruct((nseq, B_WIDTH, LANES), F32),
                   jax.ShapeDtypeStruct((nseq, CONV_PAD, B_XBC), F32)],
        scratch_shapes=[pltpu.VMEM((gsz, LANES, B_WIDTH), F32),
                        pltpu.VMEM((gsz, SSD_Q + CONV_PAD, B_XBC), F32)],
        compiler_params=_cp(("parallel", "arbitrary")),
        name="ssd",
    )(u3, u3, small3, cs0, hs0, cw, cb, dtb, alog, dsk, gn, expand, tri, qmask)
    return o.reshape(nseq * rows_per_seq, B_WIDTH), hs, cs


def _ssd_static_consts():
    expand = np.zeros((LANES, B_WIDTH), np.float32)
    for h in range(B_HEADS):
        expand[DT_LANE + h, h * HEAD_DIM:(h + 1) * HEAD_DIM] = 1.0
    tri = np.tril(np.ones((SSD_Q, SSD_Q), np.float32))
    qmask = np.zeros((LANES, B_WIDTH), np.float32)
    half = B_WIDTH // B_GROUPS
    for g in range(B_GROUPS):
        qmask[g * B_STATE:(g + 1) * B_STATE, g * half:(g + 1) * half] = 1.0
    return jnp.asarray(expand, BF16), jnp.asarray(tri, BF16), jnp.asarray(qmask)


def _stack_state(h):
    r = h.reshape(h.shape[0], B_WIDTH, B_STATE)
    return jnp.concatenate([r, r], axis=2)


def _unstack_state(hst):
    n = hst.shape[0]
    first = (np.arange(B_WIDTH) < B_WIDTH // B_GROUPS)[None, :, None]
    return jnp.where(first, hst[:, :, :B_STATE], hst[:, :, B_STATE:]).reshape(n, B_HEADS, HEAD_DIM, B_STATE)


def _merge_kernel(x, ya, yb, yc, ym, gt, wpa, wpb, wpc, wpm, wout, o):
    d = x.shape[1]
    pa = _dot(ya[...], wpa[...])
    pb = _dot(yb[...], wpb[...])
    pc = _dot(yc[...], wpc[...])
    pm = _dot(ym[...], wpm[...])
    mix = (gt[:, 0:d].astype(F32) * pa + gt[:, d:2 * d].astype(F32) * pb
           + gt[:, 2 * d:3 * d].astype(F32) * pc + gt[:, 3 * d:4 * d].astype(F32) * pm)
    o[...] = x[...] + _dot(mix.astype(BF16), wout[...])


def _merge(x, gates, ya, yb, yc, ym, wpa, wpb, wpc, wpm, wout):
    m, d = x.shape
    tm = _pick(m, (1024, 512, 256, 128))

    def rows(width):
        return pl.BlockSpec((tm, width), lambda i: (i, 0))

    full = lambda a: pl.BlockSpec(a.shape, lambda i: (0, 0))
    return pl.pallas_call(
        _merge_kernel,
        grid=(m // tm,),
        in_specs=[rows(d), rows(A_WIDTH), rows(B_WIDTH), rows(C_WIDTH), rows(M_WIDTH), rows(N_BRANCH * d),
                  full(wpa), full(wpb), full(wpc), full(wpm), full(wout)],
        out_specs=rows(d),
        out_shape=jax.ShapeDtypeStruct((m, d), F32),
        compiler_params=_cp(("parallel",)),
        name="merge",
    )(x, ya, yb, yc, ym, gates, wpa, wpb, wpc, wpm, wout)


def _row(v):
    return v.reshape(1, -1).astype(F32)


def _lane_row(v, lane0):
    return jnp.zeros((1, LANES), F32).at[0, lane0:lane0 + v.shape[0]].set(v.astype(F32))


def _block_diag_mean(width):
    seg = np.arange(width) // HEAD_DIM
    return jnp.asarray((seg[:, None] == seg[None, :]).astype(np.float32) / HEAD_DIM, BF16)


def kernel(x_prompt, x_sample, mem_prompt, cache_a_k, cache_a_v, cache_c_k, cache_c_v, cache_c_logf, state_b_ssm, state_b_conv, cache_mem_k, cache_mem_v, g_norm, w_in, a_qnorm, a_knorm, a_rel, b_conv_w, b_conv_b, b_dt_bias, b_a_log, b_d, b_norm, c_qnorm, c_knorm, c_fbias, m_norm, w_mkv, m_qnorm, m_knorm, w_pa, w_pb, w_pc, w_pm, w_out):
    nb, seq, d = x_prompt.shape
    ns, tdec, _ = x_sample.shape
    depth = g_norm.shape[0]
    n_mem = mem_prompt.shape[1]
    past = cache_c_k.shape[2]
    la_cache = cache_a_k.shape[2]
    mp = nb * seq
    md = ns * tdec
    assert seq % FOX_TK == 0 and seq % BAND_Q == 0 and seq % SSD_Q == 0
    assert tdec == CHUNK and la_cache == A_WIN and past % LANES == 0

    xp = x_prompt.reshape(mp, d)
    xs = x_sample.reshape(md, d)
    mem = mem_prompt.reshape(nb * n_mem, d)

    def feature_major(c):
        dd, n, t, h, e = c.shape
        return jnp.transpose(c, (0, 1, 3, 4, 2)).reshape(dd, n, h * e, t)

    cak, cav = feature_major(cache_a_k), feature_major(cache_a_v)
    cck, ccv = feature_major(cache_c_k), feature_major(cache_c_v)
    cmk, cmv = feature_major(cache_mem_k), feature_major(cache_mem_v)

    bd256 = _block_diag_mean(MXU_WIDTH)
    cum_consts = _cum_consts()
    expand, tri_q, qmask = _ssd_static_consts()
    scale = HEAD_DIM ** -0.5 * LOG2E

    outs = {k: [] for k in ('pb_s', 'pb_c', 'pm_k', 'pm_v', 'sa_k', 'sa_v', 'sc_k', 'sc_v', 'sc_f', 'sb_s', 'sb_c')}
    prompt_caches = None

    for l in range(depth):
        wt = jnp.transpose(w_in[l])
        pieces = [wt[_SRC[n][0]:_SRC[n][0] + _SRC[n][1]] for n in _U_ORDER]
        used = sum(p.shape[0] for p in pieces)
        w_u = jnp.concatenate(pieces + [jnp.zeros((U_WIDTH - used, d), F32)], axis=0).astype(BF16)
        g0, gw = _SRC['gate']
        w_g = wt[g0:g0 + gw].astype(BF16)
        g_row = _row(g_norm[l])
        gains = (_row(jnp.tile(a_qnorm[l], A_HEADS)) * scale, _row(jnp.tile(a_knorm[l], A_HEADS)),
                 _row(jnp.tile(c_qnorm[l], C_HEADS)) * scale, _row(jnp.tile(c_knorm[l], C_HEADS)),
                 _row(jnp.tile(m_qnorm[l], M_HEADS)) * scale, _lane_row(c_fbias[l], CF_LANE), bd256)

        u_p, small_p = _proj(xp, g_row, w_u, act=None, out_dtype=BF16, name="proj_u", w_rows=True,
                             side_col=_U_OFF['b_dt'])
        gates_p = _proj(xp, g_row, w_g, act='sigmoid', out_dtype=BF16, name="proj_gate", w_rows=True)
        (a_qs_p, a_kb_p, a_vtb_p, pa_kt, pa_vt, c_qs_p, c_kb_p, pc_kt, pc_vt, c_vtb_p, m_qs_p, lf_p,
         pc_ft) = _prep_prompt(u_p, small_p, gains, nb, seq, l, depth, prompt_caches)
        prompt_caches = (pa_kt, pa_vt, pc_kt, pc_vt, pc_ft)
        u_s, small_s = _proj(xs, g_row, w_u, act=None, out_dtype=BF16, name="proj_u", w_rows=True,
                             side_col=_U_OFF['b_dt'])
        gates_s = _proj(xs, g_row, w_g, act='sigmoid', out_dtype=BF16, name="proj_gate", w_rows=True)
        (a_qs_s, a_kb_s, a_vb_s, sa_k4, sa_v4, c_qs_s, c_kb_s, c_vb_s, sc_k4, sc_v4, m_qs_s,
         lf_s) = _prep_sample(u_s, small_s, gains)

        kv = _proj(mem, _row(m_norm[l]), w_mkv[l].astype(BF16), act=None, out_dtype=F32, name="proj_mem",
                   w_rows=False)
        mkt, mvt = _memkv(kv, _row(jnp.tile(m_knorm[l], M_HEADS)), bd256, nb, n_mem)

        o_a_p = _band_prompt(a_qs_p, a_kb_p, a_vtb_p, _band_bias(a_rel[l], BAND_Q, BAND_K, True), u_p, nb, seq)
        bias_s = _band_bias(a_rel[l], tdec, la_cache + LANES, False)
        o_a_s = _band_sample(a_qs_s, a_kb_s, a_vb_s, cak, cav, l, bias_s, u_s, ns, tdec)

        cq_p, ck_p = _cum(lf_p.reshape(nb, seq, LANES), cum_consts)
        o_c_p = _fox_prompt(c_qs_p, c_kb_p, c_vtb_p, cq_p.reshape(mp, LANES), ck_p.reshape(mp, LANES), u_p, nb, seq)
        lf_cache = jnp.pad(cache_c_logf[l].astype(F32), ((0, 0), (0, 0), (CF_LANE, LANES - CF_LANE - C_HEADS)))
        lf_cat = jnp.concatenate([lf_cache, lf_s.reshape(ns, tdec, LANES),
                                  jnp.zeros((ns, LANES - tdec, LANES), F32)], axis=1)
        cq_s, ck_s = _cum(lf_cat, cum_consts)
        o_c_s = _fox_sample(c_qs_s, c_kb_s, c_vb_s, cq_s, ck_s, cck, ccv, l, u_s, ns, tdec)

        ssd_consts = (jnp.pad(b_conv_w[l], ((0, SUBLANES - B_CONV), (0, 0))), _row(b_conv_b[l]),
                      _lane_row(b_dt_bias[l], DT_LANE), _lane_row(b_a_log[l], DT_LANE),
                      _row(jnp.repeat(b_d[l], HEAD_DIM)), _row(b_norm[l]), expand, tri_q, qmask)
        o_b_p, hs_p, cs_p = _ssd(u_p, small_p, jnp.zeros((nb, CONV_PAD, B_XBC), F32),
                                 jnp.zeros((nb, B_WIDTH, LANES), F32), ssd_consts, nb, seq)
        cs0 = jnp.pad(state_b_conv[l].astype(F32), ((0, 0), (CONV_PAD - (B_CONV - 1), 0), (0, 0)))
        o_b_s, hs_s, cs_s = _ssd(u_s, small_s, cs0, _stack_state(state_b_ssm[l].astype(F32)), ssd_consts,
                                 ns, tdec)

        o_m_p = _cross(m_qs_p, mkt[None], mvt[None], 0, u_p, seq, "cross_prompt")
        o_m_s = _cross(m_qs_s, cmk, cmv, l, u_s, tdec, "cross_sample")

        w_merge = (w_pa[l].astype(BF16), w_pb[l].astype(BF16), w_pc[l].astype(BF16), w_pm[l].astype(BF16),
                   w_out[l].astype(BF16))
        xp = _merge(xp, gates_p, o_a_p, o_b_p, o_c_p, o_m_p, *w_merge)
        xs = _merge(xs, gates_s, o_a_s, o_b_s, o_c_s, o_m_s, *w_merge)

        outs['pb_s'].append(_unstack_state(hs_p))
        outs['pb_c'].append(cs_p[:, CONV_PAD - (B_CONV - 1):])
        outs['pm_k'].append(mkt)
        outs['pm_v'].append(mvt)
        outs['sa_k'].append(sa_k4)
        outs['sa_v'].append(sa_v4)
        outs['sc_k'].append(sc_k4)
        outs['sc_v'].append(sc_v4)
        outs['sc_f'].append(lf_s[:, CF_LANE:CF_LANE + C_HEADS].reshape(ns, tdec, C_HEADS))
        outs['sb_s'].append(_unstack_state(hs_s))
        outs['sb_c'].append(cs_s[:, CONV_PAD - (B_CONV - 1):])

    st = jnp.stack

    def token_major(a, heads):
        dd, n, _, t = a.shape
        return jnp.transpose(a.reshape(dd, n, heads, HEAD_DIM, t), (0, 1, 4, 2, 3))

    def sample_heads(name):
        return st(outs[name]).reshape(depth, ns, tdec, A_HEADS, HEAD_DIM)

    return (xp.reshape(nb, seq, d), xs.reshape(ns, tdec, d),
            token_major(pa_kt, A_HEADS), token_major(pa_vt, A_HEADS),
            token_major(pc_kt, C_HEADS), token_major(pc_vt, C_HEADS),
            jnp.transpose(pc_ft, (0, 1, 3, 2)),
            st(outs['pb_s']), st(outs['pb_c']),
            token_major(st(outs['pm_k']), M_HEADS), token_major(st(outs['pm_v']), M_HEADS),
            sample_heads('sa_k'), sample_heads('sa_v'), sample_heads('sc_k'), sample_heads('sc_v'),
            st(outs['sc_f']), st(outs['sb_s']), st(outs['sb_c']))
```

```python
import functools

import numpy as np
import jax
import jax.numpy as jnp
from jax import lax
from jax.experimental import pallas as pl
from jax.experimental.pallas import tpu as pltpu

F32 = jnp.float32
BF16 = jnp.bfloat16

EPS = 1e-6
NEG = -1e30
LOG2E = 1.4426950408889634

HEAD_DIM = 64
CHUNK = 64
A_HEADS = 8
A_WIDTH = 512
A_LEFT_CHUNKS = 8
A_WIN = A_LEFT_CHUNKS * CHUNK
A_REL_CLIP = 128
B_HEADS = 8
B_WIDTH = 512
B_GROUPS = 2
B_STATE = 64
B_CONV = 4
B_XBC = B_WIDTH + 2 * B_GROUPS * B_STATE
C_HEADS = 8
C_WIDTH = 512
M_HEADS = 4
M_WIDTH = 256
N_BRANCH = 4

LANES = 128
SUBLANES = 8
MXU_WIDTH = 256
VMEM_LIMIT = 48 * 1024 * 1024

_SPLITS = (('a_q', A_WIDTH), ('a_k', A_WIDTH), ('a_v', A_WIDTH), ('a_z', A_WIDTH),
           ('b_z', B_WIDTH), ('b_xbc', B_XBC), ('b_dt', B_HEADS),
           ('c_q', C_WIDTH), ('c_k', C_WIDTH), ('c_v', C_WIDTH), ('c_f', C_HEADS), ('c_z', C_WIDTH),
           ('m_q', M_WIDTH), ('m_z', M_WIDTH), ('gate', N_BRANCH * 1024))
_SRC = {}
_off = 0
for _n, _w in _SPLITS:
    _SRC[_n] = (_off, _w)
    _off += _w

_U_ORDER = ('a_q', 'a_k', 'a_v', 'a_z', 'c_q', 'c_k', 'c_v', 'c_z', 'b_z', 'b_xbc', 'm_q', 'm_z', 'b_dt', 'c_f')
_U_OFF = {}
_off = 0
for _n in _U_ORDER:
    _U_OFF[_n] = _off
    _off += _SRC[_n][1]
U_WIDTH = 6144
DT_LANE = 0
CF_LANE = B_HEADS
AUG_PER_HEAD = 6


def _cp(sem):
    return pltpu.CompilerParams(dimension_semantics=sem, vmem_limit_bytes=VMEM_LIMIT)


def _pick(n, cands):
    for c in cands:
        if n % c == 0:
            return c
    raise ValueError(f"no tile for {n} in {cands}")


def _dot(a, b):
    return jnp.dot(a, b, preferred_element_type=F32)


def _dot_nt(a, b):
    return lax.dot_general(a, b, (((1,), (1,)), ((), ())), preferred_element_type=F32)


def _dot_tn(a, b):
    return lax.dot_general(a, b, (((0,), (0,)), ((), ())), preferred_element_type=F32)


def _split3(x):
    hi = x.astype(BF16)
    r = x - hi.astype(F32)
    mid = r.astype(BF16)
    lo = (r - mid.astype(F32)).astype(BF16)
    return hi, mid, lo


def _dot_sel(x, sel):
    hi, mid, lo = _split3(x)
    return _dot(hi, sel) + _dot(mid, sel) + _dot(lo, sel)


def _sel_dot(sel, x):
    hi, mid, lo = _split3(x)
    return _dot(sel, hi) + _dot(sel, mid) + _dot(sel, lo)


def _sigmoid(x):
    return 0.5 * jnp.tanh(0.5 * x) + 0.5


def _silu(x):
    return x * _sigmoid(x)


def _softplus(x):
    return jnp.maximum(x, 0.0) + jnp.log1p(jnp.exp(-jnp.abs(x)))


def _log_sigmoid(x):
    return jnp.minimum(x, 0.0) - jnp.log1p(jnp.exp(-jnp.abs(x)))


def _head_norm(x, bd):
    x2 = x * x
    hi = x2.astype(BF16)
    lo = (x2 - hi.astype(F32)).astype(BF16)
    w = bd.shape[0]
    ms = [_dot(hi[:, c:c + w], bd) + _dot(lo[:, c:c + w], bd) for c in range(0, x.shape[1], w)]
    ms = ms[0] if len(ms) == 1 else jnp.concatenate(ms, axis=1)
    return x * lax.rsqrt(ms + EPS)


def _lane_iota(shape):
    return lax.broadcasted_iota(jnp.int32, shape, len(shape) - 1)


def _row_iota(shape):
    return lax.broadcasted_iota(jnp.int32, shape, len(shape) - 2)


def _proj_kernel(x_ref, g_ref, w_ref, o_ref, *rest, act, w_rows, side):
    h_ref = rest[-1]

    @pl.when(pl.program_id(1) == 0)
    def _():
        x = x_ref[...]
        ms = jnp.mean(x * x, axis=-1, keepdims=True)
        h_ref[...] = (x * lax.rsqrt(ms + EPS) * g_ref[...]).astype(BF16)

    u = _dot_nt(h_ref[...], w_ref[...]) if w_rows else _dot(h_ref[...], w_ref[...])
    if side is not None:
        tile, col = side

        @pl.when(pl.program_id(1) == tile)
        def _():
            rest[0][...] = u[:, col:col + LANES]
    if act == 'sigmoid':
        u = _sigmoid(u)
    o_ref[...] = u.astype(o_ref.dtype)


def _proj(x, g_row, w_bf, *, act, out_dtype, name, w_rows, side_col=None):
    m, d = x.shape
    n = w_bf.shape[0] if w_rows else w_bf.shape[1]
    tm = _pick(m, (2048, 1024, 512, 256, 128))
    tn = _pick(n, (1024, 512, 256, 128))
    w_spec = pl.BlockSpec((tn, d), lambda i, j: (j, 0)) if w_rows else pl.BlockSpec((d, tn), lambda i, j: (0, j))
    side = None if side_col is None else (side_col // tn, side_col % tn)
    out_specs = [pl.BlockSpec((tm, tn), lambda i, j: (i, j))]
    out_shape = [jax.ShapeDtypeStruct((m, n), out_dtype)]
    if side is not None:
        out_specs.append(pl.BlockSpec((tm, LANES), lambda i, j: (i, 0)))
        out_shape.append(jax.ShapeDtypeStruct((m, LANES), F32))
    res = pl.pallas_call(
        functools.partial(_proj_kernel, act=act, w_rows=w_rows, side=side),
        grid=(m // tm, n // tn),
        in_specs=[pl.BlockSpec((tm, d), lambda i, j: (i, 0)),
                  pl.BlockSpec((1, d), lambda i, j: (0, 0)),
                  w_spec],
        out_specs=out_specs,
        out_shape=out_shape,
        scratch_shapes=[pltpu.VMEM((tm, d), BF16)],
        compiler_params=_cp(("parallel", "arbitrary")),
        name=name,
    )(x, g_row, w_bf)
    return res if side is not None else res[0]


PREP_TS = 512
VT_CHUNK = 256


def _prep_common(aq, ak, av, cq, ck, cv, mq, sm, gaq, gak, gcq, gck, gmq, fb, bd):
    bdv = bd[...]
    f32 = lambda ref: ref[...].astype(F32)
    return dict(
        a_qs=(_head_norm(f32(aq), bdv) * gaq[...]).astype(BF16),
        a_kn=_head_norm(f32(ak), bdv) * gak[...],
        a_v=f32(av),
        c_qs=(_head_norm(f32(cq), bdv) * gcq[...]).astype(BF16),
        c_kn=_head_norm(f32(ck), bdv) * gck[...],
        c_v=f32(cv),
        m_qs=(_head_norm(f32(mq), bdv) * gmq[...]).astype(BF16),
        lf=_log_sigmoid(sm[...] + fb[...]))


def _prep_prompt_kernel(aq, ak, av, cq, ck, cv, mq, sm, gaq, gak, gcq, gck, gmq, fb, bd, *refs, steps_per_seq):
    (o_aq, o_akb, o_avtb, o_akt, o_avt, o_cq, o_ckb, o_ckt, o_cvt, o_cvtb, o_mq, o_lf, o_lft) = refs[-13:]
    r = _prep_common(aq, ak, av, cq, ck, cv, mq, sm, gaq, gak, gcq, gck, gmq, fb, bd)
    o_aq[...] = r['a_qs']
    o_akb[...] = r['a_kn'].astype(BF16)
    avt = r['a_v'].T
    avtb = avt.astype(BF16)
    for c in range(PREP_TS // VT_CHUNK):
        o_avtb[c] = avtb[:, c * VT_CHUNK:(c + 1) * VT_CHUNK]

    @pl.when(pl.program_id(0) % steps_per_seq == steps_per_seq - 1)
    def _():
        o_akt[...] = r['a_kn'].T
        o_avt[...] = avt

    o_cq[...] = r['c_qs']
    o_ckb[...] = r['c_kn'].astype(BF16)
    o_ckt[...] = r['c_kn'].T
    cvt = r['c_v'].T
    o_cvt[...] = cvt
    cvtb = cvt.astype(BF16)
    for c in range(PREP_TS // VT_CHUNK):
        o_cvtb[c] = cvtb[:, c * VT_CHUNK:(c + 1) * VT_CHUNK]
    o_mq[...] = r['m_qs']
    o_lf[...] = r['lf']
    o_lft[...] = r['lf'].T[CF_LANE:CF_LANE + C_HEADS, :]


def _prep_sample_kernel(aq, ak, av, cq, ck, cv, mq, sm, gaq, gak, gcq, gck, gmq, fb, bd,
                        o_aq, o_akb, o_avb, o_ak4, o_av4, o_cq, o_ckb, o_cvb, o_ck4, o_cv4, o_mq, o_lf):
    r = _prep_common(aq, ak, av, cq, ck, cv, mq, sm, gaq, gak, gcq, gck, gmq, fb, bd)
    ts = o_aq.shape[0]
    o_aq[...] = r['a_qs']
    o_akb[...] = r['a_kn'].astype(BF16)
    o_avb[...] = r['a_v'].astype(BF16)
    o_ak4[...] = r['a_kn'].reshape(ts, A_HEADS, HEAD_DIM)
    o_av4[...] = r['a_v'].reshape(ts, A_HEADS, HEAD_DIM)
    o_cq[...] = r['c_qs']
    o_ckb[...] = r['c_kn'].astype(BF16)
    o_cvb[...] = r['c_v'].astype(BF16)
    o_ck4[...] = r['c_kn'].reshape(ts, C_HEADS, HEAD_DIM)
    o_cv4[...] = r['c_v'].reshape(ts, C_HEADS, HEAD_DIM)
    o_mq[...] = r['m_qs']
    o_lf[...] = r['lf']


def _prep_in_specs(ts):
    def ucol(name, width):
        idx = _U_OFF[name] // width
        assert idx * width == _U_OFF[name]
        return pl.BlockSpec((ts, width), lambda i: (i, idx))

    def row(width):
        return pl.BlockSpec((1, width), lambda i: (0, 0))

    return [ucol('a_q', 512), ucol('a_k', 512), ucol('a_v', 512),
            ucol('c_q', 512), ucol('c_k', 512), ucol('c_v', 512),
            ucol('m_q', 256), pl.BlockSpec((ts, LANES), lambda i: (i, 0)),
            row(512), row(512), row(512), row(512), row(256), row(LANES),
            pl.BlockSpec((MXU_WIDTH, MXU_WIDTH), lambda i: (0, 0))]


_CACHE_OUTS = (3, 4, 7, 8, 12)


def _prep_prompt(u, small, gains, nb, seq, layer, depth, prev):
    ts = PREP_TS
    assert seq % ts == 0 and min(A_WIN, seq) == ts
    sps = seq // ts
    m = nb * seq
    sds = jax.ShapeDtypeStruct
    rows = lambda width: pl.BlockSpec((ts, width), lambda i: (i, 0))
    last = pl.BlockSpec((None, None, 512, ts), lambda i: (layer, i // sps, 0, 0))
    feat = pl.BlockSpec((None, None, 512, ts), lambda i: (layer, i // sps, 0, i % sps))
    nvc = ts // VT_CHUNK
    chunks = pl.BlockSpec((None, nvc, 512, VT_CHUNK), lambda i: (i // sps, i % sps, 0, 0))
    out_specs = [rows(512), rows(512), chunks, last, last,
                 rows(512), rows(512), feat, feat, chunks,
                 rows(256), rows(LANES),
                 pl.BlockSpec((None, None, C_HEADS, ts), lambda i: (layer, i // sps, 0, i % sps))]
    out_shape = [sds((m, 512), BF16), sds((m, 512), BF16), sds((nb, seq // VT_CHUNK, 512, VT_CHUNK), BF16),
                 sds((depth, nb, 512, ts), F32), sds((depth, nb, 512, ts), F32),
                 sds((m, 512), BF16), sds((m, 512), BF16),
                 sds((depth, nb, 512, seq), F32), sds((depth, nb, 512, seq), F32),
                 sds((nb, seq // VT_CHUNK, 512, VT_CHUNK), BF16),
                 sds((m, 256), BF16), sds((m, LANES), F32), sds((depth, nb, C_HEADS, seq), F32)]
    in_specs = _prep_in_specs(ts)
    extra = ()
    aliases = {}
    if prev is not None:
        extra = tuple(prev)
        aliases = {len(in_specs) + k: pos for k, pos in enumerate(_CACHE_OUTS)}
        in_specs = in_specs + [pl.BlockSpec(memory_space=pl.ANY)] * len(extra)
    return pl.pallas_call(
        functools.partial(_prep_prompt_kernel, steps_per_seq=sps),
        grid=(m // ts,),
        in_specs=in_specs,
        out_specs=out_specs,
        out_shape=out_shape,
        input_output_aliases=aliases,
        compiler_params=_cp(("arbitrary",)),
        name="prep_prompt",
    )(*([u] * 7), small, *gains, *extra)


def _prep_sample(u, small, gains):
    m = u.shape[0]
    ts = _pick(m, (512, 256, 128, 64))
    sds = jax.ShapeDtypeStruct
    rows = lambda width: pl.BlockSpec((ts, width), lambda i: (i, 0))
    heads = pl.BlockSpec((ts, A_HEADS, HEAD_DIM), lambda i: (i, 0, 0))
    return pl.pallas_call(
        _prep_sample_kernel,
        grid=(m // ts,),
        in_specs=_prep_in_specs(ts),
        out_specs=[rows(512), rows(512), rows(512), heads, heads,
                   rows(512), rows(512), rows(512), heads, heads, rows(256), rows(LANES)],
        out_shape=[sds((m, 512), BF16), sds((m, 512), BF16), sds((m, 512), BF16),
                   sds((m, A_HEADS, HEAD_DIM), F32), sds((m, A_HEADS, HEAD_DIM), F32),
                   sds((m, 512), BF16), sds((m, 512), BF16), sds((m, 512), BF16),
                   sds((m, C_HEADS, HEAD_DIM), F32), sds((m, C_HEADS, HEAD_DIM), F32),
                   sds((m, 256), BF16), sds((m, LANES), F32)],
        compiler_params=_cp(("parallel",)),
        name="prep_sample",
    )(*([u] * 7), small, *gains)


def _memkv_kernel(kv, g, bd, o_kt, o_vt):
    o_kt[...] = (_head_norm(kv[:, :M_WIDTH], bd[...]) * g[...]).T
    o_vt[...] = kv[:, M_WIDTH:].T


def _memkv(kv, g_row, bd256, nb, n_mem):
    spec = pl.BlockSpec((None, M_WIDTH, n_mem), lambda b: (b, 0, 0))
    return pl.pallas_call(
        _memkv_kernel,
        grid=(nb,),
        in_specs=[pl.BlockSpec((n_mem, 2 * M_WIDTH), lambda b: (b, 0)),
                  pl.BlockSpec((1, M_WIDTH), lambda b: (0, 0)),
                  pl.BlockSpec((M_WIDTH, M_WIDTH), lambda b: (0, 0))],
        out_specs=[spec, spec],
        out_shape=[jax.ShapeDtypeStruct((nb, M_WIDTH, n_mem), F32)] * 2,
        compiler_params=_cp(("parallel",)),
        name="memkv",
    )(kv, g_row, bd256)


def _cum_kernel(lf_ref, tri_ref, selq_ref, selk_ref, qc_ref, kc_ref, low_ref, cq_ref, ck_ref, carry_ref,
                *, nchunk, unroll):
    ncp = carry_ref.shape[0]
    totals = lf_ref[...].reshape(nchunk, LANES, LANES).sum(axis=1)
    if ncp > nchunk:
        totals = jnp.concatenate([totals, jnp.zeros((ncp - nchunk, LANES), F32)], axis=0)
    carry_ref[...] = _sel_dot(low_ref[...], totals)
    tri = tri_ref[...]

    def body(cc, carry):
        cs = [cc * unroll + k for k in range(unroll)]
        rs = [pl.multiple_of(c * LANES, LANES) for c in cs]
        xs = [lf_ref[pl.ds(r, LANES), :] for r in rs]
        zs = [(_sel_dot(tri, x) + carry_ref[pl.ds(c, 1), :]) * LOG2E for x, c in zip(xs, cs)]
        parts = [_split3(z) for z in zs]
        qs = [_dot(p[0], selq_ref[0]) + _dot(p[1], selq_ref[1]) + _dot(p[2], selq_ref[2]) + qc_ref[...]
              for p in parts]
        ks = [_dot(p[0], selk_ref[0]) + _dot(p[1], selk_ref[1]) + _dot(p[2], selk_ref[2]) + kc_ref[...]
              for p in parts]
        for r, qv, kv in zip(rs, qs, ks):
            cq_ref[pl.ds(r, LANES), :] = qv.astype(BF16)
            ck_ref[pl.ds(r, LANES), :] = kv.astype(BF16)
        return carry

    lax.fori_loop(0, nchunk // unroll, body, 0)


CUM_UNROLLS = (8, 4, 3, 2, 1)


def _cum(lf, consts):
    nseq, length, _ = lf.shape
    assert length % LANES == 0
    nchunk = length // LANES
    unroll = _pick(nchunk, CUM_UNROLLS)
    ncp = -(-nchunk // SUBLANES) * SUBLANES
    tri, selq, selk, qc, kc = consts
    low = jnp.asarray(np.tril(np.ones((ncp, ncp), np.float32), -1), BF16)
    full = lambda shape: pl.BlockSpec(shape, lambda s: (0,) * len(shape))
    seq = pl.BlockSpec((None, length, LANES), lambda s: (s, 0, 0))
    return pl.pallas_call(
        functools.partial(_cum_kernel, nchunk=nchunk, unroll=unroll),
        grid=(nseq,),
        in_specs=[seq, full((LANES, LANES)), full((3, LANES, LANES)), full((3, LANES, LANES)),
                  full((1, LANES)), full((1, LANES)), full((ncp, ncp))],
        out_specs=[seq, seq],
        out_shape=[jax.ShapeDtypeStruct(lf.shape, BF16)] * 2,
        scratch_shapes=[pltpu.VMEM((ncp, LANES), F32)],
        compiler_params=_cp(("parallel",)),
        name="cum",
    )(lf, tri, selq, selk, qc, kc, low)


def _cum_consts():
    tri = np.tril(np.ones((LANES, LANES), np.float32))
    selq = np.zeros((3, LANES, LANES), np.float32)
    selk = np.zeros((3, LANES, LANES), np.float32)
    qc = np.zeros((1, LANES), np.float32)
    kc = np.zeros((1, LANES), np.float32)
    for h in range(C_HEADS):
        for p in range(3):
            selq[p, CF_LANE + h, AUG_PER_HEAD * h + p] = 1.0
            selk[p, CF_LANE + h, AUG_PER_HEAD * h + 3 + p] = -1.0
            qc[0, AUG_PER_HEAD * h + 3 + p] = 1.0
            kc[0, AUG_PER_HEAD * h + p] = 1.0
    return (jnp.asarray(tri, BF16), jnp.asarray(selq, BF16), jnp.asarray(selk, BF16),
            jnp.asarray(qc), jnp.asarray(kc))


BAND_Q = 4 * CHUNK
BAND_K = 3 * BAND_Q
TAB_PAD = 384
DIAG_PAD = 1024


def _bias_kernel(tab_ref, idx_ref, neg_ref, o_ref):
    nrows, ncols = neg_ref.shape
    onehot = (lax.broadcasted_iota(jnp.int32, (TAB_PAD, DIAG_PAD), 0) == idx_ref[...]).astype(BF16)
    diag = _dot_sel(tab_ref[...], onehot) * LOG2E
    for h in range(A_HEADS):
        rows = jnp.broadcast_to(diag[h:h + 1, :], (nrows, DIAG_PAD))
        o_ref[h] = pltpu.roll(rows, 0, 1, stride=1, stride_axis=0)[:, :ncols] + neg_ref[...]


def _band_bias(table, nq, nk, key_major):
    assert nq + nk <= DIAG_PAD
    r = np.arange(nq)[None, :] if key_major else np.arange(nq)[:, None]
    t = np.arange(nk)[:, None] if key_major else np.arange(nk)[None, :]
    cb = t // CHUNK - r // CHUNK
    neg = np.where((cb >= 0) & (cb <= A_LEFT_CHUNKS), 0.0, NEG).astype(np.float32)
    p = np.arange(DIAG_PAD)
    ncols = nq if key_major else nk
    col_minus_row = np.where(p < ncols, p, p - DIAG_PAD)
    q_minus_k = col_minus_row if key_major else -col_minus_row
    idx = (np.clip(A_WIN + q_minus_k, -A_REL_CLIP, A_REL_CLIP) + A_REL_CLIP).astype(np.int32).reshape(1, -1)
    tab = jnp.zeros((A_HEADS, TAB_PAD), F32).at[:, :table.shape[0]].set(table.T)
    full = lambda shape: pl.BlockSpec(shape, lambda: (0,) * len(shape))
    return pl.pallas_call(
        _bias_kernel,
        in_specs=[full((A_HEADS, TAB_PAD)), full((1, DIAG_PAD)), full(neg.shape)],
        out_specs=full((A_HEADS,) + neg.shape),
        out_shape=jax.ShapeDtypeStruct((A_HEADS,) + neg.shape, F32),
        compiler_params=pltpu.CompilerParams(vmem_limit_bytes=VMEM_LIMIT),
        name="band_bias",
    )(tab, jnp.asarray(idx), jnp.asarray(neg))


def _half_mask(par):
    lane = _lane_iota((1, LANES))
    return (lane < HEAD_DIM) if par == 0 else (lane >= HEAD_DIM)


def _softmax_pv_many(s_lists, v_lists, v_feature_major=None):
    n = len(s_lists)
    nblk = len(s_lists[0])
    if v_feature_major is None:
        v_feature_major = (False,) * nblk
    ms = []
    for sb in s_lists:
        m = sb[0].max(axis=1, keepdims=True)
        for s in sb[1:]:
            m = jnp.maximum(m, s.max(axis=1, keepdims=True))
        ms.append(m)
    ps = [[jnp.exp2(s - ms[i]) for s in s_lists[i]] for i in range(n)]
    ls = []
    for i in range(n):
        l = ps[i][0].sum(axis=1, keepdims=True)
        for p in ps[i][1:]:
            l = l + p.sum(axis=1, keepdims=True)
        ls.append(l)
    outs = []
    for i in range(n):
        acc = None
        for p, v, fm in zip(ps[i], v_lists[i], v_feature_major):
            pv = _dot_nt(p.astype(BF16), v) if fm else _dot(p.astype(BF16), v)
            acc = pv if acc is None else acc + pv
        outs.append(acc)
    return [o / l for o, l in zip(outs, ls)]


ONES_ROWS = 16


def _band_prompt_kernel(q_ref, k0, k1, k2, v0, v1, v2, bias_ref, z_ref, o_ref):
    g = pl.program_id(1)
    krefs = (k0, k1, k2)
    vrefs = (v0, v1, v2)
    s_lists = []
    for hp in range(A_HEADS // 2):
        cols = slice(hp * LANES, (hp + 1) * LANES)
        qp = q_ref[:, cols]
        ks = [kr[:, cols] for kr in krefs]
        for par in range(2):
            h = 2 * hp + par
            qm = jnp.where(_half_mask(par), qp, jnp.zeros_like(qp))
            sb = []
            for j in range(3):
                s = _dot_nt(ks[j], qm) + bias_ref[h, j * BAND_Q:(j + 1) * BAND_Q, :]
                if j < 2:
                    s = jnp.where(g + j >= 2, s, NEG)
                sb.append(s)
            s_lists.append(sb)
    ms = []
    for sb in s_lists:
        m = sb[0].max(axis=0, keepdims=True)
        for s in sb[1:]:
            m = jnp.maximum(m, s.max(axis=0, keepdims=True))
        ms.append(m)
    ps = [[jnp.exp2(s - m).astype(BF16) for s in sb] for sb, m in zip(s_lists, ms)]
    ones = jnp.ones((ONES_ROWS, BAND_Q), BF16)
    outs = []
    for h in range(A_HEADS):
        acc = None
        for j in range(3):
            vt = jnp.concatenate([vrefs[j][h * HEAD_DIM:(h + 1) * HEAD_DIM, :], ones], axis=0)
            d = _dot(vt, ps[h][j])
            acc = d if acc is None else acc + d
        outs.append(acc[:HEAD_DIM] / acc[HEAD_DIM:HEAD_DIM + 1])
    o_ref[...] = (jnp.concatenate(outs, axis=0).T * _silu(z_ref[...].astype(F32))).astype(BF16)


def _band_prompt(qs, kb, vtb, bias_t, u, nb, seq):
    assert VT_CHUNK == BAND_Q
    ng = seq // BAND_Q

    def kblk(j):
        return pl.BlockSpec((BAND_Q, A_WIDTH), lambda b, g: (b * ng + jnp.maximum(g - 2 + j, 0), 0))

    def vblk(j):
        return pl.BlockSpec((None, None, A_WIDTH, BAND_Q), lambda b, g: (b, jnp.maximum(g - 2 + j, 0), 0, 0))

    cur = pl.BlockSpec((BAND_Q, A_WIDTH), lambda b, g: (b * ng + g, 0))
    zidx = _U_OFF['a_z'] // A_WIDTH
    return pl.pallas_call(
        _band_prompt_kernel,
        grid=(nb, ng),
        in_specs=[cur, kblk(0), kblk(1), kblk(2), vblk(0), vblk(1), vblk(2),
                  pl.BlockSpec((A_HEADS, BAND_K, BAND_Q), lambda b, g: (0, 0, 0)),
                  pl.BlockSpec((BAND_Q, A_WIDTH), lambda b, g: (b * ng + g, zidx))],
        out_specs=cur,
        out_shape=jax.ShapeDtypeStruct((nb * seq, A_WIDTH), BF16),
        compiler_params=_cp(("parallel", "parallel")),
        name="band_prompt",
    )(qs, kb, kb, kb, vtb, vtb, vtb, bias_t, u)


def _band_sample_kernel(q_ref, kc_ref, kn_ref, vc_ref, vn_ref, bias_ref, z_ref, o_ref, *, t):
    lane = _lane_iota((1, LANES))
    lc = kc_ref.shape[1]
    pad = jnp.zeros((LANES - t, LANES), BF16)
    s_lists, v_lists = [], []
    for hp in range(A_HEADS // 2):
        cols = slice(hp * LANES, (hp + 1) * LANES)
        qp = q_ref[:, cols]
        kc = kc_ref[cols, :].astype(BF16)
        vc = vc_ref[cols, :].astype(BF16)
        kn = jnp.concatenate([kn_ref[:, cols], pad], axis=0)
        vn = jnp.concatenate([vn_ref[:, cols], pad], axis=0)
        for par in range(2):
            h = 2 * hp + par
            qm = jnp.where(_half_mask(par), qp, jnp.zeros_like(qp))
            s_lists.append([_dot(qm, kc) + bias_ref[h, :, :lc], _dot_nt(qm, kn) + bias_ref[h, :, lc:]])
            v_lists.append([vc, vn])
    outs = _softmax_pv_many(s_lists, v_lists, (True, False))
    for hp in range(A_HEADS // 2):
        cols = slice(hp * LANES, (hp + 1) * LANES)
        o = jnp.where(lane < HEAD_DIM, outs[2 * hp], outs[2 * hp + 1])
        o_ref[:, cols] = (o * _silu(z_ref[:, cols].astype(F32))).astype(BF16)


def _band_sample(qs, kb, vb, cache_kt, cache_vt, layer, bias_s, u, nseq, t):
    lc = cache_kt.shape[3]
    new = pl.BlockSpec((t, A_WIDTH), lambda s: (s, 0))
    cache = pl.BlockSpec((None, None, A_WIDTH, lc), lambda s: (layer, s, 0, 0))
    return pl.pallas_call(
        functools.partial(_band_sample_kernel, t=t),
        grid=(nseq,),
        in_specs=[new, cache, new, cache, new,
                  pl.BlockSpec(bias_s.shape, lambda s: (0, 0, 0)),
                  pl.BlockSpec((t, A_WIDTH), lambda s: (s, _U_OFF['a_z'] // A_WIDTH))],
        out_specs=new,
        out_shape=jax.ShapeDtypeStruct((nseq * t, A_WIDTH), BF16),
        compiler_params=_cp(("parallel",)),
        name="band_sample",
    )(qs, cache_kt, kb, cache_vt, vb, bias_s, u)


FOX_TQ = 512
FOX_TK = 512


def _aug_mask(h):
    lane = _lane_iota((1, LANES))
    return (lane >= AUG_PER_HEAD * h) & (lane < AUG_PER_HEAD * (h + 1))


def _fox_prompt_kernel(q_ref, cq_ref, k_ref, ck_ref, vt_ref, z_ref, o_ref, st_ref, m_ref, acc_ref):
    hp = pl.program_id(1)
    i = pl.program_id(2)
    qp = q_ref[...]
    cq = cq_ref[...]
    qcs = []
    for par in range(2):
        qm = jnp.where(_half_mask(par), qp, jnp.zeros_like(qp))
        qa = jnp.where(_aug_mask(2 * hp + par), cq, jnp.zeros_like(cq))
        qcs.append(jnp.concatenate([qm, qa], axis=1))
    qcat = jnp.concatenate(qcs, axis=0)
    m_ref[...] = jnp.full(m_ref.shape, NEG, F32)
    acc_ref[...] = jnp.zeros(acc_ref.shape, F32)
    ratio = FOX_TK // FOX_TQ
    nvc = FOX_TK // VT_CHUNK
    ones = jnp.ones((ONES_ROWS, VT_CHUNK), BF16)

    def scores(j, slot):
        off = pl.multiple_of(j * FOX_TK, FOX_TK)
        kc = jnp.concatenate([k_ref[pl.ds(off, FOX_TK), :], ck_ref[pl.ds(off, FOX_TK), :]], axis=1)
        st_ref[slot] = _dot_nt(kc, qcat)

    def reduce(j, slot, masked):
        st = st_ref[slot]
        if masked:
            shape = (FOX_TK, 2 * FOX_TQ)
            qpos = (_lane_iota(shape) & (FOX_TQ - 1)) + (i % ratio) * FOX_TQ
            st = jnp.where(_row_iota(shape) <= qpos, st, NEG)
        m_old = m_ref[...]
        m_new = jnp.maximum(m_old, st.max(axis=0, keepdims=True))
        alpha = jnp.exp2(m_old - m_new)
        pb = jnp.exp2(st - m_new).astype(BF16)
        for par in range(2):
            cols = slice(par * FOX_TQ, (par + 1) * FOX_TQ)
            pv = None
            for c in range(nvc):
                vt = jnp.concatenate([vt_ref[j * nvc + c, par * HEAD_DIM:(par + 1) * HEAD_DIM, :], ones], axis=0)
                d = _dot(vt, pb[c * VT_CHUNK:(c + 1) * VT_CHUNK, cols])
                pv = d if pv is None else pv + d
            acc_ref[par] = alpha[:, cols] * acc_ref[par] + pv
        m_ref[...] = m_new

    nfull = i // ratio
    scores(0, 0)

    def body(jj, carry):
        j = 2 * jj
        scores(j + 1, 1)
        reduce(j, 0, False)
        scores(j + 2, 0)
        reduce(j + 1, 1, False)
        return carry

    lax.fori_loop(0, nfull // 2, body, 0)

    @pl.when(nfull % 2 == 0)
    def _():
        reduce(nfull, 0, True)

    @pl.when(nfull % 2 == 1)
    def _():
        scores(nfull, 1)
        reduce(nfull - 1, 0, False)
        reduce(nfull, 1, True)

    ot = jnp.concatenate([acc_ref[par, :HEAD_DIM] / acc_ref[par, HEAD_DIM:HEAD_DIM + 1] for par in range(2)],
                         axis=0)
    o_ref[...] = (ot.T * _silu(z_ref[...].astype(F32))).astype(BF16)


def _fox_prompt(qs, kb, vtb, cq, ck, u, nb, seq):
    nq = seq // FOX_TQ
    qspec = pl.BlockSpec((FOX_TQ, LANES), lambda b, hp, i: (b * nq + i, hp))
    cqspec = pl.BlockSpec((FOX_TQ, LANES), lambda b, hp, i: (b * nq + i, 0))
    kspec = pl.BlockSpec((seq, LANES), lambda b, hp, i: (b, hp))
    ckspec = pl.BlockSpec((seq, LANES), lambda b, hp, i: (b, 0))
    vspec = pl.BlockSpec((None, seq // VT_CHUNK, LANES, VT_CHUNK), lambda b, hp, i: (b, 0, hp, 0))
    z0 = _U_OFF['c_z'] // LANES
    zspec = pl.BlockSpec((FOX_TQ, LANES), lambda b, hp, i: (b * nq + i, z0 + hp))
    return pl.pallas_call(
        _fox_prompt_kernel,
        grid=(nb, C_HEADS // 2, nq),
        in_specs=[qspec, cqspec, kspec, ckspec, vspec, zspec],
        out_specs=qspec,
        out_shape=jax.ShapeDtypeStruct((nb * seq, C_WIDTH), BF16),
        scratch_shapes=[pltpu.VMEM((2, FOX_TK, 2 * FOX_TQ), F32),
                        pltpu.VMEM((1, 2 * FOX_TQ), F32),
                        pltpu.VMEM((2, HEAD_DIM + ONES_ROWS, FOX_TQ), F32)],
        compiler_params=_cp(("parallel", "parallel", "arbitrary")),
        name="fox_prompt",
    )(qs, cq, kb, ck, vtb, u)


def _fox_sample_kernel(q_ref, cq_ref, kc_ref, kn_ref, ck_ref, vc_ref, vn_ref, z_ref, o_ref, *, t, past):
    lane = _lane_iota((1, LANES))
    cq = cq_ref[past:past + t, :]
    pad = jnp.zeros((LANES - t, LANES), BF16)
    ck_cache_t = ck_ref[:past, :].astype(F32).T.astype(BF16)
    ck_new = ck_ref[past:, :]
    vis = _lane_iota((t, LANES)) <= _row_iota((t, LANES))
    s_lists, v_lists = [], []
    for hp in range(C_HEADS // 2):
        cols = slice(hp * LANES, (hp + 1) * LANES)
        qp = q_ref[:, cols]
        kc = jnp.concatenate([kc_ref[cols, :].astype(BF16), ck_cache_t], axis=0)
        kn = jnp.concatenate([jnp.concatenate([kn_ref[:, cols], pad], axis=0), ck_new], axis=1)
        vc = vc_ref[cols, :].astype(BF16)
        vn = jnp.concatenate([vn_ref[:, cols], pad], axis=0)
        for par in range(2):
            qm = jnp.where(_half_mask(par), qp, jnp.zeros_like(qp))
            qa = jnp.where(_aug_mask(2 * hp + par), cq, jnp.zeros_like(cq))
            qc = jnp.concatenate([qm, qa], axis=1)
            s_lists.append([_dot(qc, kc), jnp.where(vis, _dot_nt(qc, kn), NEG)])
            v_lists.append([vc, vn])
    outs = _softmax_pv_many(s_lists, v_lists, (True, False))
    for hp in range(C_HEADS // 2):
        cols = slice(hp * LANES, (hp + 1) * LANES)
        o = jnp.where(lane < HEAD_DIM, outs[2 * hp], outs[2 * hp + 1])
        o_ref[:, cols] = (o * _silu(z_ref[:, cols].astype(F32))).astype(BF16)


def _fox_sample(qs, kb, vb, cq, ck, cache_kt, cache_vt, layer, u, nseq, t):
    past = cache_kt.shape[3]
    new = pl.BlockSpec((t, C_WIDTH), lambda s: (s, 0))
    cache = pl.BlockSpec((None, None, C_WIDTH, past), lambda s: (layer, s, 0, 0))
    aug = pl.BlockSpec((None, past + LANES, LANES), lambda s: (s, 0, 0))
    return pl.pallas_call(
        functools.partial(_fox_sample_kernel, t=t, past=past),
        grid=(nseq,),
        in_specs=[new, aug, cache, new, aug, cache, new,
                  pl.BlockSpec((t, C_WIDTH), lambda s: (s, _U_OFF['c_z'] // C_WIDTH))],
        out_specs=new,
        out_shape=jax.ShapeDtypeStruct((nseq * t, C_WIDTH), BF16),
        compiler_params=_cp(("parallel",)),
        name="fox_sample",
    )(qs, cq, cache_kt, kb, ck, cache_vt, vb, u)


def _cross_kernel(q_ref, k_ref, v_ref, z_ref, o_ref):
    lane = _lane_iota((1, LANES))
    s_lists, v_lists = [], []
    for hp in range(M_HEADS // 2):
        cols = slice(hp * LANES, (hp + 1) * LANES)
        qp = q_ref[:, cols]
        k = k_ref[cols, :].astype(BF16)
        v = v_ref[cols, :].astype(BF16)
        for par in range(2):
            qm = jnp.where(_half_mask(par), qp, jnp.zeros_like(qp))
            s_lists.append([_dot(qm, k)])
            v_lists.append([v])
    outs = _softmax_pv_many(s_lists, v_lists, (True,))
    for hp in range(M_HEADS // 2):
        cols = slice(hp * LANES, (hp + 1) * LANES)
        o = jnp.where(lane < HEAD_DIM, outs[2 * hp], outs[2 * hp + 1])
        o_ref[:, cols] = (o * _silu(z_ref[:, cols].astype(F32))).astype(BF16)


def _cross(qs, mkt, mvt, layer, u, rows_per_seq, name):
    nrows = qs.shape[0]
    n_mem = mkt.shape[3]
    tq = _pick(rows_per_seq, (512, 256, 128, 64))
    per = rows_per_seq // tq
    mem = pl.BlockSpec((None, None, M_WIDTH, n_mem), lambda i: (layer, i // per, 0, 0))
    rows = pl.BlockSpec((tq, M_WIDTH), lambda i: (i, 0))
    return pl.pallas_call(
        _cross_kernel,
        grid=(nrows // tq,),
        in_specs=[rows, mem, mem, pl.BlockSpec((tq, M_WIDTH), lambda i: (i, _U_OFF['m_z'] // M_WIDTH))],
        out_specs=rows,
        out_shape=jax.ShapeDtypeStruct((nrows, M_WIDTH), BF16),
        compiler_params=_cp(("parallel",)),
        name=name,
    )(qs, mkt, mvt, u)


SSD_Q = 128
CONV_PAD = SUBLANES
SSD_GROUP = 4


def _ssd_kernel(xbc_ref, z_ref, sm_ref, cs0_ref, hs0_ref, cw_ref, cb_ref, dtb_ref, alog_ref, dsk_ref,
                gn_ref, exp_ref, tri_ref, qm_ref,
                o_ref, hs_out_ref, cs_out_ref, hs_ref, xext_ref, *, nv, nchunk):
    c = pl.program_id(1)
    q = SSD_Q
    grp_ids = range(SSD_GROUP)
    lane = _lane_iota((1, LANES))
    head_lane = (lane >= DT_LANE) & (lane < DT_LANE + B_HEADS)
    a_row = jnp.where(head_lane, -jnp.exp(alog_ref[...]), 0.0)
    causal = _lane_iota((q, q)) <= _row_iota((q, q))
    heads_per_group = B_HEADS // B_GROUPS
    expand = exp_ref[...]
    tri = tri_ref[...]
    qm = qm_ref[...]

    @pl.when(c == 0)
    def _():
        for g in grp_ids:
            hs_ref[g] = hs0_ref[g].T * qm
            xext_ref[g, 0:CONV_PAD, :] = cs0_ref[g]

    for g in grp_ids:
        xext_ref[g, CONV_PAD:CONV_PAD + nv, :] = xbc_ref[g].astype(F32)
        if nv < q:
            xext_ref[g, CONV_PAD + nv:CONV_PAD + q, :] = jnp.zeros((q - nv, B_XBC), F32)
    xa = []
    for g in grp_ids:
        y = cb_ref[...]
        for tap in range(B_CONV):
            y = y + xext_ref[g, pl.ds(CONV_PAD - (B_CONV - 1) + tap, q), :] * cw_ref[tap:tap + 1, :]
        xa.append(_silu(y))
    for g in grp_ids:
        tail = xext_ref[g, nv:nv + CONV_PAD, :]
        xext_ref[g, 0:CONV_PAD, :] = tail
        cs_out_ref[g] = tail
    xs = [x[:, :B_WIDTH] for x in xa]
    bm = [x[:, B_WIDTH:B_WIDTH + LANES].astype(BF16) for x in xa]
    cm = [x[:, B_WIDTH + LANES:] for x in xa]

    dt = []
    for g in grp_ids:
        d = _softplus(sm_ref[g] + dtb_ref[...])
        if nv < q:
            d = jnp.concatenate([d, jnp.zeros((q - nv, LANES), F32)], axis=0)
        dt.append(jnp.where(head_lane, d, 0.0))
    acum = [_sel_dot(tri, d * a_row) for d in dt]
    dt_e = [_dot_sel(d, expand) for d in dt]
    acum_e = [_dot_sel(a, expand) for a in acum]
    alast_e = [a[q - 1:q, :] for a in acum_e]
    xdt = [x * d for x, d in zip(xs, dt_e)]

    hs = [hs_ref[g] for g in grp_ids]
    y_off = [_dot(cm[g].astype(BF16), hs[g].astype(BF16)) * jnp.exp(acum_e[g]) for g in grp_ids]
    upd = [_dot_tn(bm[g], (xdt[g] * jnp.exp(alast_e[g] - acum_e[g])).astype(BF16)) for g in grp_ids]
    hs_new = [(jnp.exp(alast_e[g]) * hs[g] + upd[g]) * qm for g in grp_ids]
    for g in grp_ids:
        hs_ref[g] = hs_new[g]

    @pl.when(c == nchunk - 1)
    def _():
        for g in grp_ids:
            hs_out_ref[g] = hs_new[g].T

    acum_t = [a.T for a in acum]
    xdt_b = [x.astype(BF16) for x in xdt]
    cbm = [[None] * B_GROUPS for _ in grp_ids]
    for grp in range(B_GROUPS):
        for g in grp_ids:
            cg = jnp.where(_half_mask(grp), cm[g], 0.0).astype(BF16)
            cbm[g][grp] = _dot_nt(cg, bm[g])
    y_pairs = [[] for _ in grp_ids]
    for hp in range(B_HEADS // 2):
        pair = [[] for _ in grp_ids]
        for par in range(2):
            h = 2 * hp + par
            for g in grp_ids:
                seg = acum[g][:, DT_LANE + h:DT_LANE + h + 1] - acum_t[g][DT_LANE + h:DT_LANE + h + 1, :]
                lmat = jnp.exp(jnp.where(causal, seg, -jnp.inf))
                m = (cbm[g][h // heads_per_group] * lmat).astype(BF16)
                pair[g].append(_dot(m, xdt_b[g][:, hp * LANES:(hp + 1) * LANES]))
        for g in grp_ids:
            y_pairs[g].append(jnp.where(lane < HEAD_DIM, pair[g][0], pair[g][1]))

    for g in grp_ids:
        yt = y_off[g] + jnp.concatenate(y_pairs[g], axis=1) + dsk_ref[...] * xs[g]
        if nv < q:
            yt = yt[:nv]
        yz = yt * _silu(z_ref[g].astype(F32))
        ms = jnp.mean(yz * yz, axis=-1, keepdims=True)
        o_ref[g] = (yz * lax.rsqrt(ms + EPS) * gn_ref[...]).astype(BF16)


def _ssd(u, small, cs0, hs0, consts, nseq, rows_per_seq):
    nv = min(SSD_Q, rows_per_seq)
    nchunk = rows_per_seq // nv
    gsz = SSD_GROUP
    assert nseq % gsz == 0
    cw, cb, dtb, alog, dsk, gn, expand, tri, qmask = consts
    u3 = u.reshape(nseq, rows_per_seq, U_WIDTH)
    small3 = small.reshape(nseq, rows_per_seq, LANES)

    def ucol(name, width):
        idx = _U_OFF[name] // width
        assert idx * width == _U_OFF[name]
        return pl.BlockSpec((gsz, nv, width), lambda s, c: (s, c, idx))

    full = lambda a: pl.BlockSpec(a.shape, lambda s, c: (0,) * a.ndim)
    cs_spec = pl.BlockSpec((gsz, CONV_PAD, B_XBC), lambda s, c: (s, 0, 0))
    hs_spec = pl.BlockSpec((gsz, B_WIDTH, LANES), lambda s, c: (s, 0, 0))
    o, hs, cs = pl.pallas_call(
        functools.partial(_ssd_kernel, nv=nv, nchunk=nchunk),
        grid=(nseq // gsz, nchunk),
        in_specs=[ucol('b_xbc', B_XBC), ucol('b_z', B_WIDTH),
                  pl.BlockSpec((gsz, nv, LANES), lambda s, c: (s, c, 0)), cs_spec, hs_spec,
                  full(cw), full(cb), full(dtb), full(alog), full(dsk), full(gn), full(expand), full(tri),
                  full(qmask)],
        out_specs=[pl.BlockSpec((gsz, nv, B_WIDTH), lambda s, c: (s, c, 0)), hs_spec, cs_spec],
        out_shape=[jax.ShapeDtypeStruct((nseq, rows_per_seq, B_WIDTH), BF16),
                   jax.ShapeDtypeStruct((nseq, B_WIDTH, LANES), F32),
                   jax.ShapeDtypeStruct((nseq, CONV_PAD, B_XBC), F32)],
        scratch_shapes=[pltpu.VMEM((gsz, LANES, B_WIDTH), F32),
                        pltpu.VMEM((gsz, SSD_Q + CONV_PAD, B_XBC), F32)],
        compiler_params=_cp(("parallel", "arbitrary")),
        name="ssd",
    )(u3, u3, small3, cs0, hs0, cw, cb, dtb, alog, dsk, gn, expand, tri, qmask)
    return o.reshape(nseq * rows_per_seq, B_WIDTH), hs, cs


def _ssd_static_consts():
    expand = np.zeros((LANES, B_WIDTH), np.float32)
    for h in range(B_HEADS):
        expand[DT_LANE + h, h * HEAD_DIM:(h + 1) * HEAD_DIM] = 1.0
    tri = np.tril(np.ones((SSD_Q, SSD_Q), np.float32))
    qmask = np.zeros((LANES, B_WIDTH), np.float32)
    half = B_WIDTH // B_GROUPS
    for g in range(B_GROUPS):
        qmask[g * B_STATE:(g + 1) * B_STATE, g * half:(g + 1) * half] = 1.0
    return jnp.asarray(expand, BF16), jnp.asarray(tri, BF16), jnp.asarray(qmask)


def _stack_state(h):
    r = h.reshape(h.shape[0], B_WIDTH, B_STATE)
    return jnp.concatenate([r, r], axis=2)


def _unstack_state(hst):
    n = hst.shape[0]
    first = (np.arange(B_WIDTH) < B_WIDTH // B_GROUPS)[None, :, None]
    return jnp.where(first, hst[:, :, :B_STATE], hst[:, :, B_STATE:]).reshape(n, B_HEADS, HEAD_DIM, B_STATE)


def _merge_kernel(x, ya, yb, yc, ym, gt, wpa, wpb, wpc, wpm, wout, o):
    d = x.shape[1]
    pa = _dot(ya[...], wpa[...])
    pb = _dot(yb[...], wpb[...])
    pc = _dot(yc[...], wpc[...])
    pm = _dot(ym[...], wpm[...])
    mix = (gt[:, 0:d].astype(F32) * pa + gt[:, d:2 * d].astype(F32) * pb
           + gt[:, 2 * d:3 * d].astype(F32) * pc + gt[:, 3 * d:4 * d].astype(F32) * pm)
    o[...] = x[...] + _dot(mix.astype(BF16), wout[...])


def _merge(x, gates, ya, yb, yc, ym, wpa, wpb, wpc, wpm, wout):
    m, d = x.shape
    tm = _pick(m, (1024, 512, 256, 128))

    def rows(width):
        return pl.BlockSpec((tm, width), lambda i: (i, 0))

    full = lambda a: pl.BlockSpec(a.shape, lambda i: (0, 0))
    return pl.pallas_call(
        _merge_kernel,
        grid=(m // tm,),
        in_specs=[rows(d), rows(A_WIDTH), rows(B_WIDTH), rows(C_WIDTH), rows(M_WIDTH), rows(N_BRANCH * d),
                  full(wpa), full(wpb), full(wpc), full(wpm), full(wout)],
        out_specs=rows(d),
        out_shape=jax.ShapeDtypeStruct((m, d), F32),
        compiler_params=_cp(("parallel",)),
        name="merge",
    )(x, ya, yb, yc, ym, gates, wpa, wpb, wpc, wpm, wout)


def _row(v):
    return v.reshape(1, -1).astype(F32)


def _lane_row(v, lane0):
    return jnp.zeros((1, LANES), F32).at[0, lane0:lane0 + v.shape[0]].set(v.astype(F32))


def _block_diag_mean(width):
    seg = np.arange(width) // HEAD_DIM
    return jnp.asarray((seg[:, None] == seg[None, :]).astype(np.float32) / HEAD_DIM, BF16)


def kernel(x_prompt, x_sample, mem_prompt, cache_a_k, cache_a_v, cache_c_k, cache_c_v, cache_c_logf, state_b_ssm, state_b_conv, cache_mem_k, cache_mem_v, g_norm, w_in, a_qnorm, a_knorm, a_rel, b_conv_w, b_conv_b, b_dt_bias, b_a_log, b_d, b_norm, c_qnorm, c_knorm, c_fbias, m_norm, w_mkv, m_qnorm, m_knorm, w_pa, w_pb, w_pc, w_pm, w_out):
    nb, seq, d = x_prompt.shape
    ns, tdec, _ = x_sample.shape
    depth = g_norm.shape[0]
    n_mem = mem_prompt.shape[1]
    past = cache_c_k.shape[2]
    la_cache = cache_a_k.shape[2]
    mp = nb * seq
    md = ns * tdec
    assert seq % FOX_TK == 0 and seq % BAND_Q == 0 and seq % SSD_Q == 0
    assert tdec == CHUNK and la_cache == A_WIN and past % LANES == 0

    xp = x_prompt.reshape(mp, d)
    xs = x_sample.reshape(md, d)
    mem = mem_prompt.reshape(nb * n_mem, d)

    def feature_major(c):
        dd, n, t, h, e = c.shape
        return jnp.transpose(c, (0, 1, 3, 4, 2)).reshape(dd, n, h * e, t)

    cak, cav = feature_major(cache_a_k), feature_major(cache_a_v)
    cck, ccv = feature_major(cache_c_k), feature_major(cache_c_v)
    cmk, cmv = feature_major(cache_mem_k), feature_major(cache_mem_v)

    bd256 = _block_diag_mean(MXU_WIDTH)
    cum_consts = _cum_consts()
    expand, tri_q, qmask = _ssd_static_consts()
    scale = HEAD_DIM ** -0.5 * LOG2E

    outs = {k: [] for k in ('pb_s', 'pb_c', 'pm_k', 'pm_v', 'sa_k', 'sa_v', 'sc_k', 'sc_v', 'sc_f', 'sb_s', 'sb_c')}
    prompt_caches = None

    for l in range(depth):
        wt = jnp.transpose(w_in[l])
        pieces = [wt[_SRC[n][0]:_SRC[n][0] + _SRC[n][1]] for n in _U_ORDER]
        used = sum(p.shape[0] for p in pieces)
        w_u = jnp.concatenate(pieces + [jnp.zeros((U_WIDTH - used, d), F32)], axis=0).astype(BF16)
        g0, gw = _SRC['gate']
        w_g = wt[g0:g0 + gw].astype(BF16)
        g_row = _row(g_norm[l])
        gains = (_row(jnp.tile(a_qnorm[l], A_HEADS)) * scale, _row(jnp.tile(a_knorm[l], A_HEADS)),
                 _row(jnp.tile(c_qnorm[l], C_HEADS)) * scale, _row(jnp.tile(c_knorm[l], C_HEADS)),
                 _row(jnp.tile(m_qnorm[l], M_HEADS)) * scale, _lane_row(c_fbias[l], CF_LANE), bd256)

        u_p, small_p = _proj(xp, g_row, w_u, act=None, out_dtype=BF16, name="proj_u", w_rows=True,
                             side_col=_U_OFF['b_dt'])
        gates_p = _proj(xp, g_row, w_g, act='sigmoid', out_dtype=BF16, name="proj_gate", w_rows=True)
        (a_qs_p, a_kb_p, a_vtb_p, pa_kt, pa_vt, c_qs_p, c_kb_p, pc_kt, pc_vt, c_vtb_p, m_qs_p, lf_p,
         pc_ft) = _prep_prompt(u_p, small_p, gains, nb, seq, l, depth, prompt_caches)
        prompt_caches = (pa_kt, pa_vt, pc_kt, pc_vt, pc_ft)
        u_s, small_s = _proj(xs, g_row, w_u, act=None, out_dtype=BF16, name="proj_u", w_rows=True,
                             side_col=_U_OFF['b_dt'])
        gates_s = _proj(xs, g_row, w_g, act='sigmoid', out_dtype=BF16, name="proj_gate", w_rows=True)
        (a_qs_s, a_kb_s, a_vb_s, sa_k4, sa_v4, c_qs_s, c_kb_s, c_vb_s, sc_k4, sc_v4, m_qs_s,
         lf_s) = _prep_sample(u_s, small_s, gains)

        kv = _proj(mem, _row(m_norm[l]), w_mkv[l].astype(BF16), act=None, out_dtype=F32, name="proj_mem",
                   w_rows=False)
        mkt, mvt = _memkv(kv, _row(jnp.tile(m_knorm[l], M_HEADS)), bd256, nb, n_mem)

        o_a_p = _band_prompt(a_qs_p, a_kb_p, a_vtb_p, _band_bias(a_rel[l], BAND_Q, BAND_K, True), u_p, nb, seq)
        bias_s = _band_bias(a_rel[l], tdec, la_cache + LANES, False)
        o_a_s = _band_sample(a_qs_s, a_kb_s, a_vb_s, cak, cav, l, bias_s, u_s, ns, tdec)

        cq_p, ck_p = _cum(lf_p.reshape(nb, seq, LANES), cum_consts)
        o_c_p = _fox_prompt(c_qs_p, c_kb_p, c_vtb_p, cq_p.reshape(mp, LANES), ck_p.reshape(mp, LANES), u_p, nb, seq)
        lf_cache = jnp.pad(cache_c_logf[l].astype(F32), ((0, 0), (0, 0), (CF_LANE, LANES - CF_LANE - C_HEADS)))
        lf_cat = jnp.concatenate([lf_cache, lf_s.reshape(ns, tdec, LANES),
                                  jnp.zeros((ns, LANES - tdec, LANES), F32)], axis=1)
        cq_s, ck_s = _cum(lf_cat, cum_consts)
        o_c_s = _fox_sample(c_qs_s, c_kb_s, c_vb_s, cq_s, ck_s, cck, ccv, l, u_s, ns, tdec)

        ssd_consts = (jnp.pad(b_conv_w[l], ((0, SUBLANES - B_CONV), (0, 0))), _row(b_conv_b[l]),
                      _lane_row(b_dt_bias[l], DT_LANE), _lane_row(b_a_log[l], DT_LANE),
                      _row(jnp.repeat(b_d[l], HEAD_DIM)), _row(b_norm[l]), expand, tri_q, qmask)
        o_b_p, hs_p, cs_p = _ssd(u_p, small_p, jnp.zeros((nb, CONV_PAD, B_XBC), F32),
                                 jnp.zeros((nb, B_WIDTH, LANES), F32), ssd_consts, nb, seq)
        cs0 = jnp.pad(state_b_conv[l].astype(F32), ((0, 0), (CONV_PAD - (B_CONV - 1), 0), (0, 0)))
        o_b_s, hs_s, cs_s = _ssd(u_s, small_s, cs0, _stack_state(state_b_ssm[l].astype(F32)), ssd_consts,
                                 ns, tdec)

        o_m_p = _cross(m_qs_p, mkt[None], mvt[None], 0, u_p, seq, "cross_prompt")
        o_m_s = _cross(m_qs_s, cmk, cmv, l, u_s, tdec, "cross_sample")

        w_merge = (w_pa[l].astype(BF16), w_pb[l].astype(BF16), w_pc[l].astype(BF16), w_pm[l].astype(BF16),
                   w_out[l].astype(BF16))
        xp = _merge(xp, gates_p, o_a_p, o_b_p, o_c_p, o_m_p, *w_merge)
        xs = _merge(xs, gates_s, o_a_s, o_b_s, o_c_s, o_m_s, *w_merge)

        outs['pb_s'].append(_unstack_state(hs_p))
        outs['pb_c'].append(cs_p[:, CONV_PAD - (B_CONV - 1):])
        outs['pm_k'].append(mkt)
        outs['pm_v'].append(mvt)
        outs['sa_k'].append(sa_k4)
        outs['sa_v'].append(sa_v4)
        outs['sc_k'].append(sc_k4)
        outs['sc_v'].append(sc_v4)
        outs['sc_f'].append(lf_s[:, CF_LANE:CF_LANE + C_HEADS].reshape(ns, tdec, C_HEADS))
        outs['sb_s'].append(_unstack_state(hs_s))
        outs['sb_c'].append(cs_s[:, CONV_PAD - (B_CONV - 1):])

    st = jnp.stack

    def token_major(a, heads):
        dd, n, _, t = a.shape
        return jnp.transpose(a.reshape(dd, n, heads, HEAD_DIM, t), (0, 1, 4, 2, 3))

    def sample_heads(name):
        return st(outs[name]).reshape(depth, ns, tdec, A_HEADS, HEAD_DIM)

    return (xp.reshape(nb, seq, d), xs.reshape(ns, tdec, d),
            token_major(pa_kt, A_HEADS), token_major(pa_vt, A_HEADS),
            token_major(pc_kt, C_HEADS), token_major(pc_vt, C_HEADS),
            jnp.transpose(pc_ft, (0, 1, 3, 2)),
            st(outs['pb_s']), st(outs['pb_c']),
            token_major(st(outs['pm_k']), M_HEADS), token_major(st(outs['pm_v']), M_HEADS),
            sample_heads('sa_k'), sample_heads('sa_v'), sample_heads('sc_k'), sample_heads('sc_v'),
            st(outs['sc_f']), st(outs['sb_s']), st(outs['sb_c']))
```

```python
import functools

import numpy as np
import jax
import jax.numpy as jnp
from jax import lax
from jax.experimental import pallas as pl
from jax.experimental.pallas import tpu as pltpu

F32 = jnp.float32
BF16 = jnp.bfloat16

EPS = 1e-6
NEG = -1e30
LOG2E = 1.4426950408889634

HEAD_DIM = 64
CHUNK = 64
A_HEADS = 8
A_WIDTH = 512
A_LEFT_CHUNKS = 8
A_WIN = A_LEFT_CHUNKS * CHUNK
A_REL_CLIP = 128
B_HEADS = 8
B_WIDTH = 512
B_GROUPS = 2
B_STATE = 64
B_CONV = 4
B_XBC = B_WIDTH + 2 * B_GROUPS * B_STATE
C_HEADS = 8
C_WIDTH = 512
M_HEADS = 4
M_WIDTH = 256
N_BRANCH = 4

LANES = 128
SUBLANES = 8
MXU_WIDTH = 256
VMEM_LIMIT = 48 * 1024 * 1024

_SPLITS = (('a_q', A_WIDTH), ('a_k', A_WIDTH), ('a_v', A_WIDTH), ('a_z', A_WIDTH),
           ('b_z', B_WIDTH), ('b_xbc', B_XBC), ('b_dt', B_HEADS),
           ('c_q', C_WIDTH), ('c_k', C_WIDTH), ('c_v', C_WIDTH), ('c_f', C_HEADS), ('c_z', C_WIDTH),
           ('m_q', M_WIDTH), ('m_z', M_WIDTH), ('gate', N_BRANCH * 1024))
_SRC = {}
_off = 0
for _n, _w in _SPLITS:
    _SRC[_n] = (_off, _w)
    _off += _w

_U_ORDER = ('a_q', 'a_k', 'a_v', 'a_z', 'c_q', 'c_k', 'c_v', 'c_z', 'b_z', 'b_xbc', 'm_q', 'm_z', 'b_dt', 'c_f')
_U_OFF = {}
_off = 0
for _n in _U_ORDER:
    _U_OFF[_n] = _off
    _off += _SRC[_n][1]
U_WIDTH = 6144
DT_LANE = 0
CF_LANE = B_HEADS
AUG_PER_HEAD = 6


def _cp(sem):
    return pltpu.CompilerParams(dimension_semantics=sem, vmem_limit_bytes=VMEM_LIMIT)


def _pick(n, cands):
    for c in cands:
        if n % c == 0:
            return c
    raise ValueError(f"no tile for {n} in {cands}")


def _dot(a, b):
    return jnp.dot(a, b, preferred_element_type=F32)


def _dot_nt(a, b):
    return lax.dot_general(a, b, (((1,), (1,)), ((), ())), preferred_element_type=F32)


def _dot_tn(a, b):
    return lax.dot_general(a, b, (((0,), (0,)), ((), ())), preferred_element_type=F32)


def _split3(x):
    hi = x.astype(BF16)
    r = x - hi.astype(F32)
    mid = r.astype(BF16)
    lo = (r - mid.astype(F32)).astype(BF16)
    return hi, mid, lo


def _dot_sel(x, sel):
    hi, mid, lo = _split3(x)
    return _dot(hi, sel) + _dot(mid, sel) + _dot(lo, sel)


def _sel_dot(sel, x):
    hi, mid, lo = _split3(x)
    return _dot(sel, hi) + _dot(sel, mid) + _dot(sel, lo)


def _sigmoid(x):
    return 0.5 * jnp.tanh(0.5 * x) + 0.5


def _silu(x):
    return x * _sigmoid(x)


def _softplus(x):
    return jnp.maximum(x, 0.0) + jnp.log1p(jnp.exp(-jnp.abs(x)))


def _log_sigmoid(x):
    return jnp.minimum(x, 0.0) - jnp.log1p(jnp.exp(-jnp.abs(x)))


def _head_norm(x, bd):
    x2 = x * x
    hi = x2.astype(BF16)
    lo = (x2 - hi.astype(F32)).astype(BF16)
    w = bd.shape[0]
    ms = [_dot(hi[:, c:c + w], bd) + _dot(lo[:, c:c + w], bd) for c in range(0, x.shape[1], w)]
    ms = ms[0] if len(ms) == 1 else jnp.concatenate(ms, axis=1)
    return x * lax.rsqrt(ms + EPS)


def _lane_iota(shape):
    return lax.broadcasted_iota(jnp.int32, shape, len(shape) - 1)


def _row_iota(shape):
    return lax.broadcasted_iota(jnp.int32, shape, len(shape) - 2)


def _proj_kernel(x_ref, g_ref, w_ref, o_ref, *rest, act, w_rows, side):
    h_ref = rest[-1]

    @pl.when(pl.program_id(1) == 0)
    def _():
        x = x_ref[...]
        ms = jnp.mean(x * x, axis=-1, keepdims=True)
        h_ref[...] = (x * lax.rsqrt(ms + EPS) * g_ref[...]).astype(BF16)

    u = _dot_nt(h_ref[...], w_ref[...]) if w_rows else _dot(h_ref[...], w_ref[...])
    if side is not None:
        tile, col = side

        @pl.when(pl.program_id(1) == tile)
        def _():
            rest[0][...] = u[:, col:col + LANES]
    if act == 'sigmoid':
        u = _sigmoid(u)
    o_ref[...] = u.astype(o_ref.dtype)


def _proj(x, g_row, w_bf, *, act, out_dtype, name, w_rows, side_col=None):
    m, d = x.shape
    n = w_bf.shape[0] if w_rows else w_bf.shape[1]
    tm = _pick(m, (2048, 1024, 512, 256, 128))
    tn = _pick(n, (1024, 512, 256, 128))
    w_spec = pl.BlockSpec((tn, d), lambda i, j: (j, 0)) if w_rows else pl.BlockSpec((d, tn), lambda i, j: (0, j))
    side = None if side_col is None else (side_col // tn, side_col % tn)
    out_specs = [pl.BlockSpec((tm, tn), lambda i, j: (i, j))]
    out_shape = [jax.ShapeDtypeStruct((m, n), out_dtype)]
    if side is not None:
        out_specs.append(pl.BlockSpec((tm, LANES), lambda i, j: (i, 0)))
        out_shape.append(jax.ShapeDtypeStruct((m, LANES), F32))
    res = pl.pallas_call(
        functools.partial(_proj_kernel, act=act, w_rows=w_rows, side=side),
        grid=(m // tm, n // tn),
        in_specs=[pl.BlockSpec((tm, d), lambda i, j: (i, 0)),
                  pl.BlockSpec((1, d), lambda i, j: (0, 0)),
                  w_spec],
        out_specs=out_specs,
        out_shape=out_shape,
        scratch_shapes=[pltpu.VMEM((tm, d), BF16)],
        compiler_params=_cp(("parallel", "arbitrary")),
        name=name,
    )(x, g_row, w_bf)
    return res if side is not None else res[0]


PREP_TS = 512
VT_CHUNK = 256


def _prep_common(aq, ak, av, cq, ck, cv, mq, sm, gaq, gak, gcq, gck, gmq, fb, bd):
    bdv = bd[...]
    f32 = lambda ref: ref[...].astype(F32)
    return dict(
        a_qs=(_head_norm(f32(aq), bdv) * gaq[...]).astype(BF16),
        a_kn=_head_norm(f32(ak), bdv) * gak[...],
        a_v=f32(av),
        c_qs=(_head_norm(f32(cq), bdv) * gcq[...]).astype(BF16),
        c_kn=_head_norm(f32(ck), bdv) * gck[...],
        c_v=f32(cv),
        m_qs=(_head_norm(f32(mq), bdv) * gmq[...]).astype(BF16),
        lf=_log_sigmoid(sm[...] + fb[...]))


def _prep_prompt_kernel(aq, ak, av, cq, ck, cv, mq, sm, gaq, gak, gcq, gck, gmq, fb, bd,
                        o_aq, o_akb, o_avtb, o_akt, o_avt, o_cq, o_ckb, o_ckt, o_cvt, o_cvtb, o_mq, o_lf, o_lft,
                        *, steps_per_seq):
    r = _prep_common(aq, ak, av, cq, ck, cv, mq, sm, gaq, gak, gcq, gck, gmq, fb, bd)
    o_aq[...] = r['a_qs']
    o_akb[...] = r['a_kn'].astype(BF16)
    avt = r['a_v'].T
    avtb = avt.astype(BF16)
    for c in range(PREP_TS // VT_CHUNK):
        o_avtb[c] = avtb[:, c * VT_CHUNK:(c + 1) * VT_CHUNK]

    @pl.when(pl.program_id(0) % steps_per_seq == steps_per_seq - 1)
    def _():
        o_akt[...] = r['a_kn'].T
        o_avt[...] = avt

    o_cq[...] = r['c_qs']
    o_ckb[...] = r['c_kn'].astype(BF16)
    o_ckt[...] = r['c_kn'].T
    cvt = r['c_v'].T
    o_cvt[...] = cvt
    cvtb = cvt.astype(BF16)
    for c in range(PREP_TS // VT_CHUNK):
        o_cvtb[c] = cvtb[:, c * VT_CHUNK:(c + 1) * VT_CHUNK]
    o_mq[...] = r['m_qs']
    o_lf[...] = r['lf']
    o_lft[...] = r['lf'].T[CF_LANE:CF_LANE + C_HEADS, :]


def _prep_sample_kernel(aq, ak, av, cq, ck, cv, mq, sm, gaq, gak, gcq, gck, gmq, fb, bd,
                        o_aq, o_akb, o_avb, o_ak4, o_av4, o_cq, o_ckb, o_cvb, o_ck4, o_cv4, o_mq, o_lf):
    r = _prep_common(aq, ak, av, cq, ck, cv, mq, sm, gaq, gak, gcq, gck, gmq, fb, bd)
    ts = o_aq.shape[0]
    o_aq[...] = r['a_qs']
    o_akb[...] = r['a_kn'].astype(BF16)
    o_avb[...] = r['a_v'].astype(BF16)
    o_ak4[...] = r['a_kn'].reshape(ts, A_HEADS, HEAD_DIM)
    o_av4[...] = r['a_v'].reshape(ts, A_HEADS, HEAD_DIM)
    o_cq[...] = r['c_qs']
    o_ckb[...] = r['c_kn'].astype(BF16)
    o_cvb[...] = r['c_v'].astype(BF16)
    o_ck4[...] = r['c_kn'].reshape(ts, C_HEADS, HEAD_DIM)
    o_cv4[...] = r['c_v'].reshape(ts, C_HEADS, HEAD_DIM)
    o_mq[...] = r['m_qs']
    o_lf[...] = r['lf']


def _prep_in_specs(ts):
    def ucol(name, width):
        idx = _U_OFF[name] // width
        assert idx * width == _U_OFF[name]
        return pl.BlockSpec((ts, width), lambda i: (i, idx))

    def row(width):
        return pl.BlockSpec((1, width), lambda i: (0, 0))

    return [ucol('a_q', 512), ucol('a_k', 512), ucol('a_v', 512),
            ucol('c_q', 512), ucol('c_k', 512), ucol('c_v', 512),
            ucol('m_q', 256), pl.BlockSpec((ts, LANES), lambda i: (i, 0)),
            row(512), row(512), row(512), row(512), row(256), row(LANES),
            pl.BlockSpec((MXU_WIDTH, MXU_WIDTH), lambda i: (0, 0))]


def _prep_prompt(u, small, gains, nb, seq):
    ts = PREP_TS
    assert seq % ts == 0 and min(A_WIN, seq) == ts
    sps = seq // ts
    m = nb * seq
    sds = jax.ShapeDtypeStruct
    rows = lambda width: pl.BlockSpec((ts, width), lambda i: (i, 0))
    last = pl.BlockSpec((None, 512, ts), lambda i: (i // sps, 0, 0))
    feat = pl.BlockSpec((None, 512, ts), lambda i: (i // sps, 0, i % sps))
    nvc = ts // VT_CHUNK
    chunks = pl.BlockSpec((None, nvc, 512, VT_CHUNK), lambda i: (i // sps, i % sps, 0, 0))
    out_specs = [rows(512), rows(512), chunks, last, last,
                 rows(512), rows(512), feat, feat, chunks,
                 rows(256), rows(LANES),
                 pl.BlockSpec((None, C_HEADS, ts), lambda i: (i // sps, 0, i % sps))]
    out_shape = [sds((m, 512), BF16), sds((m, 512), BF16), sds((nb, seq // VT_CHUNK, 512, VT_CHUNK), BF16),
                 sds((nb, 512, ts), F32), sds((nb, 512, ts), F32),
                 sds((m, 512), BF16), sds((m, 512), BF16), sds((nb, 512, seq), F32), sds((nb, 512, seq), F32),
                 sds((nb, seq // VT_CHUNK, 512, VT_CHUNK), BF16),
                 sds((m, 256), BF16), sds((m, LANES), F32), sds((nb, C_HEADS, seq), F32)]
    return pl.pallas_call(
        functools.partial(_prep_prompt_kernel, steps_per_seq=sps),
        grid=(m // ts,),
        in_specs=_prep_in_specs(ts),
        out_specs=out_specs,
        out_shape=out_shape,
        compiler_params=_cp(("arbitrary",)),
        name="prep_prompt",
    )(*([u] * 7), small, *gains)


def _prep_sample(u, small, gains):
    m = u.shape[0]
    ts = _pick(m, (512, 256, 128, 64))
    sds = jax.ShapeDtypeStruct
    rows = lambda width: pl.BlockSpec((ts, width), lambda i: (i, 0))
    heads = pl.BlockSpec((ts, A_HEADS, HEAD_DIM), lambda i: (i, 0, 0))
    return pl.pallas_call(
        _prep_sample_kernel,
        grid=(m // ts,),
        in_specs=_prep_in_specs(ts),
        out_specs=[rows(512), rows(512), rows(512), heads, heads,
                   rows(512), rows(512), rows(512), heads, heads, rows(256), rows(LANES)],
        out_shape=[sds((m, 512), BF16), sds((m, 512), BF16), sds((m, 512), BF16),
                   sds((m, A_HEADS, HEAD_DIM), F32), sds((m, A_HEADS, HEAD_DIM), F32),
                   sds((m, 512), BF16), sds((m, 512), BF16), sds((m, 512), BF16),
                   sds((m, C_HEADS, HEAD_DIM), F32), sds((m, C_HEADS, HEAD_DIM), F32),
                   sds((m, 256), BF16), sds((m, LANES), F32)],
        compiler_params=_cp(("parallel",)),
        name="prep_sample",
    )(*([u] * 7), small, *gains)


def _memkv_kernel(kv, g, bd, o_kt, o_vt):
    o_kt[...] = (_head_norm(kv[:, :M_WIDTH], bd[...]) * g[...]).T
    o_vt[...] = kv[:, M_WIDTH:].T


def _memkv(kv, g_row, bd256, nb, n_mem):
    spec = pl.BlockSpec((None, M_WIDTH, n_mem), lambda b: (b, 0, 0))
    return pl.pallas_call(
        _memkv_kernel,
        grid=(nb,),
        in_specs=[pl.BlockSpec((n_mem, 2 * M_WIDTH), lambda b: (b, 0)),
                  pl.BlockSpec((1, M_WIDTH), lambda b: (0, 0)),
                  pl.BlockSpec((M_WIDTH, M_WIDTH), lambda b: (0, 0))],
        out_specs=[spec, spec],
        out_shape=[jax.ShapeDtypeStruct((nb, M_WIDTH, n_mem), F32)] * 2,
        compiler_params=_cp(("parallel",)),
        name="memkv",
    )(kv, g_row, bd256)


def _cum_kernel(lf_ref, tri_ref, selq_ref, selk_ref, qc_ref, kc_ref, low_ref, cq_ref, ck_ref, carry_ref,
                *, nchunk, unroll):
    ncp = carry_ref.shape[0]
    totals = lf_ref[...].reshape(nchunk, LANES, LANES).sum(axis=1)
    if ncp > nchunk:
        totals = jnp.concatenate([totals, jnp.zeros((ncp - nchunk, LANES), F32)], axis=0)
    carry_ref[...] = _sel_dot(low_ref[...], totals)
    tri = tri_ref[...]

    def body(cc, carry):
        cs = [cc * unroll + k for k in range(unroll)]
        rs = [pl.multiple_of(c * LANES, LANES) for c in cs]
        xs = [lf_ref[pl.ds(r, LANES), :] for r in rs]
        zs = [(_sel_dot(tri, x) + carry_ref[pl.ds(c, 1), :]) * LOG2E for x, c in zip(xs, cs)]
        parts = [_split3(z) for z in zs]
        qs = [_dot(p[0], selq_ref[0]) + _dot(p[1], selq_ref[1]) + _dot(p[2], selq_ref[2]) + qc_ref[...]
              for p in parts]
        ks = [_dot(p[0], selk_ref[0]) + _dot(p[1], selk_ref[1]) + _dot(p[2], selk_ref[2]) + kc_ref[...]
              for p in parts]
        for r, qv, kv in zip(rs, qs, ks):
            cq_ref[pl.ds(r, LANES), :] = qv.astype(BF16)
            ck_ref[pl.ds(r, LANES), :] = kv.astype(BF16)
        return carry

    lax.fori_loop(0, nchunk // unroll, body, 0)


CUM_UNROLLS = (8, 4, 3, 2, 1)


def _cum(lf, consts):
    nseq, length, _ = lf.shape
    assert length % LANES == 0
    nchunk = length // LANES
    unroll = _pick(nchunk, CUM_UNROLLS)
    ncp = -(-nchunk // SUBLANES) * SUBLANES
    tri, selq, selk, qc, kc = consts
    low = jnp.asarray(np.tril(np.ones((ncp, ncp), np.float32), -1), BF16)
    full = lambda shape: pl.BlockSpec(shape, lambda s: (0,) * len(shape))
    seq = pl.BlockSpec((None, length, LANES), lambda s: (s, 0, 0))
    return pl.pallas_call(
        functools.partial(_cum_kernel, nchunk=nchunk, unroll=unroll),
        grid=(nseq,),
        in_specs=[seq, full((LANES, LANES)), full((3, LANES, LANES)), full((3, LANES, LANES)),
                  full((1, LANES)), full((1, LANES)), full((ncp, ncp))],
        out_specs=[seq, seq],
        out_shape=[jax.ShapeDtypeStruct(lf.shape, BF16)] * 2,
        scratch_shapes=[pltpu.VMEM((ncp, LANES), F32)],
        compiler_params=_cp(("parallel",)),
        name="cum",
    )(lf, tri, selq, selk, qc, kc, low)


def _cum_consts():
    tri = np.tril(np.ones((LANES, LANES), np.float32))
    selq = np.zeros((3, LANES, LANES), np.float32)
    selk = np.zeros((3, LANES, LANES), np.float32)
    qc = np.zeros((1, LANES), np.float32)
    kc = np.zeros((1, LANES), np.float32)
    for h in range(C_HEADS):
        for p in range(3):
            selq[p, CF_LANE + h, AUG_PER_HEAD * h + p] = 1.0
            selk[p, CF_LANE + h, AUG_PER_HEAD * h + 3 + p] = -1.0
            qc[0, AUG_PER_HEAD * h + 3 + p] = 1.0
            kc[0, AUG_PER_HEAD * h + p] = 1.0
    return (jnp.asarray(tri, BF16), jnp.asarray(selq, BF16), jnp.asarray(selk, BF16),
            jnp.asarray(qc), jnp.asarray(kc))


BAND_Q = 4 * CHUNK
BAND_K = 3 * BAND_Q
TAB_PAD = 384
DIAG_PAD = 1024


def _bias_kernel(tab_ref, idx_ref, neg_ref, o_ref):
    nrows, ncols = neg_ref.shape
    onehot = (lax.broadcasted_iota(jnp.int32, (TAB_PAD, DIAG_PAD), 0) == idx_ref[...]).astype(BF16)
    diag = _dot_sel(tab_ref[...], onehot) * LOG2E
    for h in range(A_HEADS):
        rows = jnp.broadcast_to(diag[h:h + 1, :], (nrows, DIAG_PAD))
        o_ref[h] = pltpu.roll(rows, 0, 1, stride=1, stride_axis=0)[:, :ncols] + neg_ref[...]


def _band_bias(table, nq, nk, key_major):
    assert nq + nk <= DIAG_PAD
    r = np.arange(nq)[None, :] if key_major else np.arange(nq)[:, None]
    t = np.arange(nk)[:, None] if key_major else np.arange(nk)[None, :]
    cb = t // CHUNK - r // CHUNK
    neg = np.where((cb >= 0) & (cb <= A_LEFT_CHUNKS), 0.0, NEG).astype(np.float32)
    p = np.arange(DIAG_PAD)
    ncols = nq if key_major else nk
    col_minus_row = np.where(p < ncols, p, p - DIAG_PAD)
    q_minus_k = col_minus_row if key_major else -col_minus_row
    idx = (np.clip(A_WIN + q_minus_k, -A_REL_CLIP, A_REL_CLIP) + A_REL_CLIP).astype(np.int32).reshape(1, -1)
    tab = jnp.zeros((A_HEADS, TAB_PAD), F32).at[:, :table.shape[0]].set(table.T)
    full = lambda shape: pl.BlockSpec(shape, lambda: (0,) * len(shape))
    return pl.pallas_call(
        _bias_kernel,
        in_specs=[full((A_HEADS, TAB_PAD)), full((1, DIAG_PAD)), full(neg.shape)],
        out_specs=full((A_HEADS,) + neg.shape),
        out_shape=jax.ShapeDtypeStruct((A_HEADS,) + neg.shape, F32),
        compiler_params=pltpu.CompilerParams(vmem_limit_bytes=VMEM_LIMIT),
        name="band_bias",
    )(tab, jnp.asarray(idx), jnp.asarray(neg))


def _half_mask(par):
    lane = _lane_iota((1, LANES))
    return (lane < HEAD_DIM) if par == 0 else (lane >= HEAD_DIM)


def _softmax_pv_many(s_lists, v_lists, v_feature_major=None):
    n = len(s_lists)
    nblk = len(s_lists[0])
    if v_feature_major is None:
        v_feature_major = (False,) * nblk
    ms = []
    for sb in s_lists:
        m = sb[0].max(axis=1, keepdims=True)
        for s in sb[1:]:
            m = jnp.maximum(m, s.max(axis=1, keepdims=True))
        ms.append(m)
    ps = [[jnp.exp2(s - ms[i]) for s in s_lists[i]] for i in range(n)]
    ls = []
    for i in range(n):
        l = ps[i][0].sum(axis=1, keepdims=True)
        for p in ps[i][1:]:
            l = l + p.sum(axis=1, keepdims=True)
        ls.append(l)
    outs = []
    for i in range(n):
        acc = None
        for p, v, fm in zip(ps[i], v_lists[i], v_feature_major):
            pv = _dot_nt(p.astype(BF16), v) if fm else _dot(p.astype(BF16), v)
            acc = pv if acc is None else acc + pv
        outs.append(acc)
    return [o / l for o, l in zip(outs, ls)]


ONES_ROWS = 16


def _band_prompt_kernel(q_ref, k0, k1, k2, v0, v1, v2, bias_ref, z_ref, o_ref):
    g = pl.program_id(1)
    krefs = (k0, k1, k2)
    vrefs = (v0, v1, v2)
    s_lists = []
    for hp in range(A_HEADS // 2):
        cols = slice(hp * LANES, (hp + 1) * LANES)
        qp = q_ref[:, cols]
        ks = [kr[:, cols] for kr in krefs]
        for par in range(2):
            h = 2 * hp + par
            qm = jnp.where(_half_mask(par), qp, jnp.zeros_like(qp))
            sb = []
            for j in range(3):
                s = _dot_nt(ks[j], qm) + bias_ref[h, j * BAND_Q:(j + 1) * BAND_Q, :]
                if j < 2:
                    s = jnp.where(g + j >= 2, s, NEG)
                sb.append(s)
            s_lists.append(sb)
    ms = []
    for sb in s_lists:
        m = sb[0].max(axis=0, keepdims=True)
        for s in sb[1:]:
            m = jnp.maximum(m, s.max(axis=0, keepdims=True))
        ms.append(m)
    ps = [[jnp.exp2(s - m).astype(BF16) for s in sb] for sb, m in zip(s_lists, ms)]
    ones = jnp.ones((ONES_ROWS, BAND_Q), BF16)
    outs = []
    for h in range(A_HEADS):
        acc = None
        for j in range(3):
            vt = jnp.concatenate([vrefs[j][h * HEAD_DIM:(h + 1) * HEAD_DIM, :], ones], axis=0)
            d = _dot(vt, ps[h][j])
            acc = d if acc is None else acc + d
        outs.append(acc[:HEAD_DIM] / acc[HEAD_DIM:HEAD_DIM + 1])
    o_ref[...] = (jnp.concatenate(outs, axis=0).T * _silu(z_ref[...].astype(F32))).astype(BF16)


def _band_prompt(qs, kb, vtb, bias_t, u, nb, seq):
    assert VT_CHUNK == BAND_Q
    ng = seq // BAND_Q

    def kblk(j):
        return pl.BlockSpec((BAND_Q, A_WIDTH), lambda b, g: (b * ng + jnp.maximum(g - 2 + j, 0), 0))

    def vblk(j):
        return pl.BlockSpec((None, None, A_WIDTH, BAND_Q), lambda b, g: (b, jnp.maximum(g - 2 + j, 0), 0, 0))

    cur = pl.BlockSpec((BAND_Q, A_WIDTH), lambda b, g: (b * ng + g, 0))
    zidx = _U_OFF['a_z'] // A_WIDTH
    return pl.pallas_call(
        _band_prompt_kernel,
        grid=(nb, ng),
        in_specs=[cur, kblk(0), kblk(1), kblk(2), vblk(0), vblk(1), vblk(2),
                  pl.BlockSpec((A_HEADS, BAND_K, BAND_Q), lambda b, g: (0, 0, 0)),
                  pl.BlockSpec((BAND_Q, A_WIDTH), lambda b, g: (b * ng + g, zidx))],
        out_specs=cur,
        out_shape=jax.ShapeDtypeStruct((nb * seq, A_WIDTH), BF16),
        compiler_params=_cp(("parallel", "parallel")),
        name="band_prompt",
    )(qs, kb, kb, kb, vtb, vtb, vtb, bias_t, u)


def _band_sample_kernel(q_ref, kc_ref, kn_ref, vc_ref, vn_ref, bias_ref, z_ref, o_ref, *, t):
    lane = _lane_iota((1, LANES))
    lc = kc_ref.shape[1]
    pad = jnp.zeros((LANES - t, LANES), BF16)
    s_lists, v_lists = [], []
    for hp in range(A_HEADS // 2):
        cols = slice(hp * LANES, (hp + 1) * LANES)
        qp = q_ref[:, cols]
        kc = kc_ref[cols, :].astype(BF16)
        vc = vc_ref[cols, :].astype(BF16)
        kn = jnp.concatenate([kn_ref[:, cols], pad], axis=0)
        vn = jnp.concatenate([vn_ref[:, cols], pad], axis=0)
        for par in range(2):
            h = 2 * hp + par
            qm = jnp.where(_half_mask(par), qp, jnp.zeros_like(qp))
            s_lists.append([_dot(qm, kc) + bias_ref[h, :, :lc], _dot_nt(qm, kn) + bias_ref[h, :, lc:]])
            v_lists.append([vc, vn])
    outs = _softmax_pv_many(s_lists, v_lists, (True, False))
    for hp in range(A_HEADS // 2):
        cols = slice(hp * LANES, (hp + 1) * LANES)
        o = jnp.where(lane < HEAD_DIM, outs[2 * hp], outs[2 * hp + 1])
        o_ref[:, cols] = (o * _silu(z_ref[:, cols].astype(F32))).astype(BF16)


def _band_sample(qs, kb, vb, cache_kt, cache_vt, layer, bias_s, u, nseq, t):
    lc = cache_kt.shape[3]
    new = pl.BlockSpec((t, A_WIDTH), lambda s: (s, 0))
    cache = pl.BlockSpec((None, None, A_WIDTH, lc), lambda s: (layer, s, 0, 0))
    return pl.pallas_call(
        functools.partial(_band_sample_kernel, t=t),
        grid=(nseq,),
        in_specs=[new, cache, new, cache, new,
                  pl.BlockSpec(bias_s.shape, lambda s: (0, 0, 0)),
                  pl.BlockSpec((t, A_WIDTH), lambda s: (s, _U_OFF['a_z'] // A_WIDTH))],
        out_specs=new,
        out_shape=jax.ShapeDtypeStruct((nseq * t, A_WIDTH), BF16),
        compiler_params=_cp(("parallel",)),
        name="band_sample",
    )(qs, cache_kt, kb, cache_vt, vb, bias_s, u)


FOX_TQ = 512
FOX_TK = 512


def _aug_mask(h):
    lane = _lane_iota((1, LANES))
    return (lane >= AUG_PER_HEAD * h) & (lane < AUG_PER_HEAD * (h + 1))


def _fox_prompt_kernel(q_ref, cq_ref, k_ref, ck_ref, vt_ref, z_ref, o_ref, st_ref, m_ref, acc_ref):
    hp = pl.program_id(1)
    i = pl.program_id(2)
    qp = q_ref[...]
    cq = cq_ref[...]
    qcs = []
    for par in range(2):
        qm = jnp.where(_half_mask(par), qp, jnp.zeros_like(qp))
        qa = jnp.where(_aug_mask(2 * hp + par), cq, jnp.zeros_like(cq))
        qcs.append(jnp.concatenate([qm, qa], axis=1))
    qcat = jnp.concatenate(qcs, axis=0)
    m_ref[...] = jnp.full(m_ref.shape, NEG, F32)
    acc_ref[...] = jnp.zeros(acc_ref.shape, F32)
    ratio = FOX_TK // FOX_TQ
    nvc = FOX_TK // VT_CHUNK
    ones = jnp.ones((ONES_ROWS, VT_CHUNK), BF16)

    def scores(j, slot):
        off = pl.multiple_of(j * FOX_TK, FOX_TK)
        kc = jnp.concatenate([k_ref[pl.ds(off, FOX_TK), :], ck_ref[pl.ds(off, FOX_TK), :]], axis=1)
        st_ref[slot] = _dot_nt(kc, qcat)

    def reduce(j, slot, masked):
        st = st_ref[slot]
        if masked:
            shape = (FOX_TK, 2 * FOX_TQ)
            qpos = (_lane_iota(shape) & (FOX_TQ - 1)) + (i % ratio) * FOX_TQ
            st = jnp.where(_row_iota(shape) <= qpos, st, NEG)
        m_old = m_ref[...]
        m_new = jnp.maximum(m_old, st.max(axis=0, keepdims=True))
        alpha = jnp.exp2(m_old - m_new)
        pb = jnp.exp2(st - m_new).astype(BF16)
        for par in range(2):
            cols = slice(par * FOX_TQ, (par + 1) * FOX_TQ)
            pv = None
            for c in range(nvc):
                vt = jnp.concatenate([vt_ref[j * nvc + c, par * HEAD_DIM:(par + 1) * HEAD_DIM, :], ones], axis=0)
                d = _dot(vt, pb[c * VT_CHUNK:(c + 1) * VT_CHUNK, cols])
                pv = d if pv is None else pv + d
            acc_ref[par] = alpha[:, cols] * acc_ref[par] + pv
        m_ref[...] = m_new

    nfull = i // ratio
    scores(0, 0)

    def body(jj, carry):
        j = 2 * jj
        scores(j + 1, 1)
        reduce(j, 0, False)
        scores(j + 2, 0)
        reduce(j + 1, 1, False)
        return carry

    lax.fori_loop(0, nfull // 2, body, 0)

    @pl.when(nfull % 2 == 0)
    def _():
        reduce(nfull, 0, True)

    @pl.when(nfull % 2 == 1)
    def _():
        scores(nfull, 1)
        reduce(nfull - 1, 0, False)
        reduce(nfull, 1, True)

    ot = jnp.concatenate([acc_ref[par, :HEAD_DIM] / acc_ref[par, HEAD_DIM:HEAD_DIM + 1] for par in range(2)],
                         axis=0)
    o_ref[...] = (ot.T * _silu(z_ref[...].astype(F32))).astype(BF16)


def _fox_prompt(qs, kb, vtb, cq, ck, u, nb, seq):
    nq = seq // FOX_TQ
    qspec = pl.BlockSpec((FOX_TQ, LANES), lambda b, hp, i: (b * nq + i, hp))
    cqspec = pl.BlockSpec((FOX_TQ, LANES), lambda b, hp, i: (b * nq + i, 0))
    kspec = pl.BlockSpec((seq, LANES), lambda b, hp, i: (b, hp))
    ckspec = pl.BlockSpec((seq, LANES), lambda b, hp, i: (b, 0))
    vspec = pl.BlockSpec((None, seq // VT_CHUNK, LANES, VT_CHUNK), lambda b, hp, i: (b, 0, hp, 0))
    z0 = _U_OFF['c_z'] // LANES
    zspec = pl.BlockSpec((FOX_TQ, LANES), lambda b, hp, i: (b * nq + i, z0 + hp))
    return pl.pallas_call(
        _fox_prompt_kernel,
        grid=(nb, C_HEADS // 2, nq),
        in_specs=[qspec, cqspec, kspec, ckspec, vspec, zspec],
        out_specs=qspec,
        out_shape=jax.ShapeDtypeStruct((nb * seq, C_WIDTH), BF16),
        scratch_shapes=[pltpu.VMEM((2, FOX_TK, 2 * FOX_TQ), F32),
                        pltpu.VMEM((1, 2 * FOX_TQ), F32),
                        pltpu.VMEM((2, HEAD_DIM + ONES_ROWS, FOX_TQ), F32)],
        compiler_params=_cp(("parallel", "parallel", "arbitrary")),
        name="fox_prompt",
    )(qs, cq, kb, ck, vtb, u)


def _fox_sample_kernel(q_ref, cq_ref, kc_ref, kn_ref, ck_ref, vc_ref, vn_ref, z_ref, o_ref, *, t, past):
    lane = _lane_iota((1, LANES))
    cq = cq_ref[past:past + t, :]
    pad = jnp.zeros((LANES - t, LANES), BF16)
    ck_cache_t = ck_ref[:past, :].astype(F32).T.astype(BF16)
    ck_new = ck_ref[past:, :]
    vis = _lane_iota((t, LANES)) <= _row_iota((t, LANES))
    s_lists, v_lists = [], []
    for hp in range(C_HEADS // 2):
        cols = slice(hp * LANES, (hp + 1) * LANES)
        qp = q_ref[:, cols]
        kc = jnp.concatenate([kc_ref[cols, :].astype(BF16), ck_cache_t], axis=0)
        kn = jnp.concatenate([jnp.concatenate([kn_ref[:, cols], pad], axis=0), ck_new], axis=1)
        vc = vc_ref[cols, :].astype(BF16)
        vn = jnp.concatenate([vn_ref[:, cols], pad], axis=0)
        for par in range(2):
            qm = jnp.where(_half_mask(par), qp, jnp.zeros_like(qp))
            qa = jnp.where(_aug_mask(2 * hp + par), cq, jnp.zeros_like(cq))
            qc = jnp.concatenate([qm, qa], axis=1)
            s_lists.append([_dot(qc, kc), jnp.where(vis, _dot_nt(qc, kn), NEG)])
            v_lists.append([vc, vn])
    outs = _softmax_pv_many(s_lists, v_lists, (True, False))
    for hp in range(C_HEADS // 2):
        cols = slice(hp * LANES, (hp + 1) * LANES)
        o = jnp.where(lane < HEAD_DIM, outs[2 * hp], outs[2 * hp + 1])
        o_ref[:, cols] = (o * _silu(z_ref[:, cols].astype(F32))).astype(BF16)


def _fox_sample(qs, kb, vb, cq, ck, cache_kt, cache_vt, layer, u, nseq, t):
    past = cache_kt.shape[3]
    new = pl.BlockSpec((t, C_WIDTH), lambda s: (s, 0))
    cache = pl.BlockSpec((None, None, C_WIDTH, past), lambda s: (layer, s, 0, 0))
    aug = pl.BlockSpec((None, past + LANES, LANES), lambda s: (s, 0, 0))
    return pl.pallas_call(
        functools.partial(_fox_sample_kernel, t=t, past=past),
        grid=(nseq,),
        in_specs=[new, aug, cache, new, aug, cache, new,
                  pl.BlockSpec((t, C_WIDTH), lambda s: (s, _U_OFF['c_z'] // C_WIDTH))],
        out_specs=new,
        out_shape=jax.ShapeDtypeStruct((nseq * t, C_WIDTH), BF16),
        compiler_params=_cp(("parallel",)),
        name="fox_sample",
    )(qs, cq, cache_kt, kb, ck, cache_vt, vb, u)


def _cross_kernel(q_ref, k_ref, v_ref, z_ref, o_ref):
    lane = _lane_iota((1, LANES))
    s_lists, v_lists = [], []
    for hp in range(M_HEADS // 2):
        cols = slice(hp * LANES, (hp + 1) * LANES)
        qp = q_ref[:, cols]
        k = k_ref[cols, :].astype(BF16)
        v = v_ref[cols, :].astype(BF16)
        for par in range(2):
            qm = jnp.where(_half_mask(par), qp, jnp.zeros_like(qp))
            s_lists.append([_dot(qm, k)])
            v_lists.append([v])
    outs = _softmax_pv_many(s_lists, v_lists, (True,))
    for hp in range(M_HEADS // 2):
        cols = slice(hp * LANES, (hp + 1) * LANES)
        o = jnp.where(lane < HEAD_DIM, outs[2 * hp], outs[2 * hp + 1])
        o_ref[:, cols] = (o * _silu(z_ref[:, cols].astype(F32))).astype(BF16)


def _cross(qs, mkt, mvt, layer, u, rows_per_seq, name):
    nrows = qs.shape[0]
    n_mem = mkt.shape[3]
    tq = _pick(rows_per_seq, (512, 256, 128, 64))
    per = rows_per_seq // tq
    mem = pl.BlockSpec((None, None, M_WIDTH, n_mem), lambda i: (layer, i // per, 0, 0))
    rows = pl.BlockSpec((tq, M_WIDTH), lambda i: (i, 0))
    return pl.pallas_call(
        _cross_kernel,
        grid=(nrows // tq,),
        in_specs=[rows, mem, mem, pl.BlockSpec((tq, M_WIDTH), lambda i: (i, _U_OFF['m_z'] // M_WIDTH))],
        out_specs=rows,
        out_shape=jax.ShapeDtypeStruct((nrows, M_WIDTH), BF16),
        compiler_params=_cp(("parallel",)),
        name=name,
    )(qs, mkt, mvt, u)


SSD_Q = 128
CONV_PAD = SUBLANES
SSD_GROUP = 4


def _ssd_kernel(xbc_ref, z_ref, sm_ref, cs0_ref, hs0_ref, cw_ref, cb_ref, dtb_ref, alog_ref, dsk_ref,
                gn_ref, exp_ref, tri_ref, qm_ref,
                o_ref, hs_out_ref, cs_out_ref, hs_ref, xext_ref, *, nv, nchunk):
    c = pl.program_id(1)
    q = SSD_Q
    grp_ids = range(SSD_GROUP)
    lane = _lane_iota((1, LANES))
    head_lane = (lane >= DT_LANE) & (lane < DT_LANE + B_HEADS)
    a_row = jnp.where(head_lane, -jnp.exp(alog_ref[...]), 0.0)
    causal = _lane_iota((q, q)) <= _row_iota((q, q))
    heads_per_group = B_HEADS // B_GROUPS
    expand = exp_ref[...]
    tri = tri_ref[...]
    qm = qm_ref[...]

    @pl.when(c == 0)
    def _():
        for g in grp_ids:
            hs_ref[g] = hs0_ref[g].T * qm
            xext_ref[g, 0:CONV_PAD, :] = cs0_ref[g]

    for g in grp_ids:
        xext_ref[g, CONV_PAD:CONV_PAD + nv, :] = xbc_ref[g].astype(F32)
        if nv < q:
            xext_ref[g, CONV_PAD + nv:CONV_PAD + q, :] = jnp.zeros((q - nv, B_XBC), F32)
    xa = []
    for g in grp_ids:
        y = cb_ref[...]
        for tap in range(B_CONV):
            y = y + xext_ref[g, pl.ds(CONV_PAD - (B_CONV - 1) + tap, q), :] * cw_ref[tap:tap + 1, :]
        xa.append(_silu(y))
    for g in grp_ids:
        tail = xext_ref[g, nv:nv + CONV_PAD, :]
        xext_ref[g, 0:CONV_PAD, :] = tail
        cs_out_ref[g] = tail
    xs = [x[:, :B_WIDTH] for x in xa]
    bm = [x[:, B_WIDTH:B_WIDTH + LANES].astype(BF16) for x in xa]
    cm = [x[:, B_WIDTH + LANES:] for x in xa]

    dt = []
    for g in grp_ids:
        d = _softplus(sm_ref[g] + dtb_ref[...])
        if nv < q:
            d = jnp.concatenate([d, jnp.zeros((q - nv, LANES), F32)], axis=0)
        dt.append(jnp.where(head_lane, d, 0.0))
    acum = [_sel_dot(tri, d * a_row) for d in dt]
    dt_e = [_dot_sel(d, expand) for d in dt]
    acum_e = [_dot_sel(a, expand) for a in acum]
    alast_e = [a[q - 1:q, :] for a in acum_e]
    xdt = [x * d for x, d in zip(xs, dt_e)]

    hs = [hs_ref[g] for g in grp_ids]
    y_off = [_dot(cm[g].astype(BF16), hs[g].astype(BF16)) * jnp.exp(acum_e[g]) for g in grp_ids]
    upd = [_dot_tn(bm[g], (xdt[g] * jnp.exp(alast_e[g] - acum_e[g])).astype(BF16)) for g in grp_ids]
    hs_new = [(jnp.exp(alast_e[g]) * hs[g] + upd[g]) * qm for g in grp_ids]
    for g in grp_ids:
        hs_ref[g] = hs_new[g]

    @pl.when(c == nchunk - 1)
    def _():
        for g in grp_ids:
            hs_out_ref[g] = hs_new[g].T

    acum_t = [a.T for a in acum]
    xdt_b = [x.astype(BF16) for x in xdt]
    cbm = [[None] * B_GROUPS for _ in grp_ids]
    for grp in range(B_GROUPS):
        for g in grp_ids:
            cg = jnp.where(_half_mask(grp), cm[g], 0.0).astype(BF16)
            cbm[g][grp] = _dot_nt(cg, bm[g])
    y_pairs = [[] for _ in grp_ids]
    for hp in range(B_HEADS // 2):
        pair = [[] for _ in grp_ids]
        for par in range(2):
            h = 2 * hp + par
            for g in grp_ids:
                seg = acum[g][:, DT_LANE + h:DT_LANE + h + 1] - acum_t[g][DT_LANE + h:DT_LANE + h + 1, :]
                lmat = jnp.exp(jnp.where(causal, seg, -jnp.inf))
                m = (cbm[g][h // heads_per_group] * lmat).astype(BF16)
                pair[g].append(_dot(m, xdt_b[g][:, hp * LANES:(hp + 1) * LANES]))
        for g in grp_ids:
            y_pairs[g].append(jnp.where(lane < HEAD_DIM, pair[g][0], pair[g][1]))

    for g in grp_ids:
        yt = y_off[g] + jnp.concatenate(y_pairs[g], axis=1) + dsk_ref[...] * xs[g]
        if nv < q:
            yt = yt[:nv]
        yz = yt * _silu(z_ref[g].astype(F32))
        ms = jnp.mean(yz * yz, axis=-1, keepdims=True)
        o_ref[g] = (yz * lax.rsqrt(ms + EPS) * gn_ref[...]).astype(BF16)


def _ssd(u, small, cs0, hs0, consts, nseq, rows_per_seq):
    nv = min(SSD_Q, rows_per_seq)
    nchunk = rows_per_seq // nv
    gsz = SSD_GROUP
    assert nseq % gsz == 0
    cw, cb, dtb, alog, dsk, gn, expand, tri, qmask = consts
    u3 = u.reshape(nseq, rows_per_seq, U_WIDTH)
    small3 = small.reshape(nseq, rows_per_seq, LANES)

    def ucol(name, width):
        idx = _U_OFF[name] // width
        assert idx * width == _U_OFF[name]
        return pl.BlockSpec((gsz, nv, width), lambda s, c: (s, c, idx))

    full = lambda a: pl.BlockSpec(a.shape, lambda s, c: (0,) * a.ndim)
    cs_spec = pl.BlockSpec((gsz, CONV_PAD, B_XBC), lambda s, c: (s, 0, 0))
    hs_spec = pl.BlockSpec((gsz, B_WIDTH, LANES), lambda s, c: (s, 0, 0))
    o, hs, cs = pl.pallas_call(
        functools.partial(_ssd_kernel, nv=nv, nchunk=nchunk),
        grid=(nseq // gsz, nchunk),
        in_specs=[ucol('b_xbc', B_XBC), ucol('b_z', B_WIDTH),
                  pl.BlockSpec((gsz, nv, LANES), lambda s, c: (s, c, 0)), cs_spec, hs_spec,
                  full(cw), full(cb), full(dtb), full(alog), full(dsk), full(gn), full(expand), full(tri),
                  full(qmask)],
        out_specs=[pl.BlockSpec((gsz, nv, B_WIDTH), lambda s, c: (s, c, 0)), hs_spec, cs_spec],
        out_shape=[jax.ShapeDtypeStruct((nseq, rows_per_seq, B_WIDTH), BF16),
                   jax.ShapeDtypeStruct((nseq, B_WIDTH, LANES), F32),
                   jax.ShapeDtypeStruct((nseq, CONV_PAD, B_XBC), F32)],
        scratch_shapes=[pltpu.VMEM((gsz, LANES, B_WIDTH), F32),
                        pltpu.VMEM((gsz, SSD_Q + CONV_PAD, B_XBC), F32)],
        compiler_params=_cp(("parallel", "arbitrary")),
        name="ssd",
    )(u3, u3, small3, cs0, hs0, cw, cb, dtb, alog, dsk, gn, expand, tri, qmask)
    return o.reshape(nseq * rows_per_seq, B_WIDTH), hs, cs


def _ssd_static_consts():
    expand = np.zeros((LANES, B_WIDTH), np.float32)
    for h in range(B_HEADS):
        expand[DT_LANE + h, h * HEAD_DIM:(h + 1) * HEAD_DIM] = 1.0
    tri = np.tril(np.ones((SSD_Q, SSD_Q), np.float32))
    qmask = np.zeros((LANES, B_WIDTH), np.float32)
    half = B_WIDTH // B_GROUPS
    for g in range(B_GROUPS):
        qmask[g * B_STATE:(g + 1) * B_STATE, g * half:(g + 1) * half] = 1.0
    return jnp.asarray(expand, BF16), jnp.asarray(tri, BF16), jnp.asarray(qmask)


def _stack_state(h):
    r = h.reshape(h.shape[0], B_WIDTH, B_STATE)
    return jnp.concatenate([r, r], axis=2)


def _unstack_state(hst):
    n = hst.shape[0]
    first = (np.arange(B_WIDTH) < B_WIDTH // B_GROUPS)[None, :, None]
    return jnp.where(first, hst[:, :, :B_STATE], hst[:, :, B_STATE:]).reshape(n, B_HEADS, HEAD_DIM, B_STATE)


def _merge_kernel(x, ya, yb, yc, ym, gt, wpa, wpb, wpc, wpm, wout, o):
    d = x.shape[1]
    pa = _dot(ya[...], wpa[...])
    pb = _dot(yb[...], wpb[...])
    pc = _dot(yc[...], wpc[...])
    pm = _dot(ym[...], wpm[...])
    mix = (gt[:, 0:d].astype(F32) * pa + gt[:, d:2 * d].astype(F32) * pb
           + gt[:, 2 * d:3 * d].astype(F32) * pc + gt[:, 3 * d:4 * d].astype(F32) * pm)
    o[...] = x[...] + _dot(mix.astype(BF16), wout[...])


def _merge(x, gates, ya, yb, yc, ym, wpa, wpb, wpc, wpm, wout):
    m, d = x.shape
    tm = _pick(m, (1024, 512, 256, 128))

    def rows(width):
        return pl.BlockSpec((tm, width), lambda i: (i, 0))

    full = lambda a: pl.BlockSpec(a.shape, lambda i: (0, 0))
    return pl.pallas_call(
        _merge_kernel,
        grid=(m // tm,),
        in_specs=[rows(d), rows(A_WIDTH), rows(B_WIDTH), rows(C_WIDTH), rows(M_WIDTH), rows(N_BRANCH * d),
                  full(wpa), full(wpb), full(wpc), full(wpm), full(wout)],
        out_specs=rows(d),
        out_shape=jax.ShapeDtypeStruct((m, d), F32),
        compiler_params=_cp(("parallel",)),
        name="merge",
    )(x, ya, yb, yc, ym, gates, wpa, wpb, wpc, wpm, wout)


def _row(v):
    return v.reshape(1, -1).astype(F32)


def _lane_row(v, lane0):
    return jnp.zeros((1, LANES), F32).at[0, lane0:lane0 + v.shape[0]].set(v.astype(F32))


def _block_diag_mean(width):
    seg = np.arange(width) // HEAD_DIM
    return jnp.asarray((seg[:, None] == seg[None, :]).astype(np.float32) / HEAD_DIM, BF16)


def kernel(x_prompt, x_sample, mem_prompt, cache_a_k, cache_a_v, cache_c_k, cache_c_v, cache_c_logf, state_b_ssm, state_b_conv, cache_mem_k, cache_mem_v, g_norm, w_in, a_qnorm, a_knorm, a_rel, b_conv_w, b_conv_b, b_dt_bias, b_a_log, b_d, b_norm, c_qnorm, c_knorm, c_fbias, m_norm, w_mkv, m_qnorm, m_knorm, w_pa, w_pb, w_pc, w_pm, w_out):
    nb, seq, d = x_prompt.shape
    ns, tdec, _ = x_sample.shape
    depth = g_norm.shape[0]
    n_mem = mem_prompt.shape[1]
    past = cache_c_k.shape[2]
    la_cache = cache_a_k.shape[2]
    mp = nb * seq
    md = ns * tdec
    assert seq % FOX_TK == 0 and seq % BAND_Q == 0 and seq % SSD_Q == 0
    assert tdec == CHUNK and la_cache == A_WIN and past % LANES == 0

    xp = x_prompt.reshape(mp, d)
    xs = x_sample.reshape(md, d)
    mem = mem_prompt.reshape(nb * n_mem, d)

    def feature_major(c):
        dd, n, t, h, e = c.shape
        return jnp.transpose(c, (0, 1, 3, 4, 2)).reshape(dd, n, h * e, t)

    cak, cav = feature_major(cache_a_k), feature_major(cache_a_v)
    cck, ccv = feature_major(cache_c_k), feature_major(cache_c_v)
    cmk, cmv = feature_major(cache_mem_k), feature_major(cache_mem_v)

    bd256 = _block_diag_mean(MXU_WIDTH)
    cum_consts = _cum_consts()
    expand, tri_q, qmask = _ssd_static_consts()
    scale = HEAD_DIM ** -0.5 * LOG2E

    outs = {k: [] for k in ('pa_k', 'pa_v', 'pc_k', 'pc_v', 'pc_f', 'pb_s', 'pb_c', 'pm_k', 'pm_v',
                            'sa_k', 'sa_v', 'sc_k', 'sc_v', 'sc_f', 'sb_s', 'sb_c')}

    for l in range(depth):
        wt = jnp.transpose(w_in[l])
        pieces = [wt[_SRC[n][0]:_SRC[n][0] + _SRC[n][1]] for n in _U_ORDER]
        used = sum(p.shape[0] for p in pieces)
        w_u = jnp.concatenate(pieces + [jnp.zeros((U_WIDTH - used, d), F32)], axis=0).astype(BF16)
        g0, gw = _SRC['gate']
        w_g = wt[g0:g0 + gw].astype(BF16)
        g_row = _row(g_norm[l])
        gains = (_row(jnp.tile(a_qnorm[l], A_HEADS)) * scale, _row(jnp.tile(a_knorm[l], A_HEADS)),
                 _row(jnp.tile(c_qnorm[l], C_HEADS)) * scale, _row(jnp.tile(c_knorm[l], C_HEADS)),
                 _row(jnp.tile(m_qnorm[l], M_HEADS)) * scale, _lane_row(c_fbias[l], CF_LANE), bd256)

        u_p, small_p = _proj(xp, g_row, w_u, act=None, out_dtype=BF16, name="proj_u", w_rows=True,
                             side_col=_U_OFF['b_dt'])
        gates_p = _proj(xp, g_row, w_g, act='sigmoid', out_dtype=BF16, name="proj_gate", w_rows=True)
        (a_qs_p, a_kb_p, a_vtb_p, pa_kt, pa_vt, c_qs_p, c_kb_p, pc_kt, pc_vt, c_vtb_p, m_qs_p, lf_p,
         pc_ft) = _prep_prompt(u_p, small_p, gains, nb, seq)
        u_s, small_s = _proj(xs, g_row, w_u, act=None, out_dtype=BF16, name="proj_u", w_rows=True,
                             side_col=_U_OFF['b_dt'])
        gates_s = _proj(xs, g_row, w_g, act='sigmoid', out_dtype=BF16, name="proj_gate", w_rows=True)
        (a_qs_s, a_kb_s, a_vb_s, sa_k4, sa_v4, c_qs_s, c_kb_s, c_vb_s, sc_k4, sc_v4, m_qs_s,
         lf_s) = _prep_sample(u_s, small_s, gains)

        kv = _proj(mem, _row(m_norm[l]), w_mkv[l].astype(BF16), act=None, out_dtype=F32, name="proj_mem",
                   w_rows=False)
        mkt, mvt = _memkv(kv, _row(jnp.tile(m_knorm[l], M_HEADS)), bd256, nb, n_mem)

        o_a_p = _band_prompt(a_qs_p, a_kb_p, a_vtb_p, _band_bias(a_rel[l], BAND_Q, BAND_K, True), u_p, nb, seq)
        bias_s = _band_bias(a_rel[l], tdec, la_cache + LANES, False)
        o_a_s = _band_sample(a_qs_s, a_kb_s, a_vb_s, cak, cav, l, bias_s, u_s, ns, tdec)

        cq_p, ck_p = _cum(lf_p.reshape(nb, seq, LANES), cum_consts)
        o_c_p = _fox_prompt(c_qs_p, c_kb_p, c_vtb_p, cq_p.reshape(mp, LANES), ck_p.reshape(mp, LANES), u_p, nb, seq)
        lf_cache = jnp.pad(cache_c_logf[l].astype(F32), ((0, 0), (0, 0), (CF_LANE, LANES - CF_LANE - C_HEADS)))
        lf_cat = jnp.concatenate([lf_cache, lf_s.reshape(ns, tdec, LANES),
                                  jnp.zeros((ns, LANES - tdec, LANES), F32)], axis=1)
        cq_s, ck_s = _cum(lf_cat, cum_consts)
        o_c_s = _fox_sample(c_qs_s, c_kb_s, c_vb_s, cq_s, ck_s, cck, ccv, l, u_s, ns, tdec)

        ssd_consts = (jnp.pad(b_conv_w[l], ((0, SUBLANES - B_CONV), (0, 0))), _row(b_conv_b[l]),
                      _lane_row(b_dt_bias[l], DT_LANE), _lane_row(b_a_log[l], DT_LANE),
                      _row(jnp.repeat(b_d[l], HEAD_DIM)), _row(b_norm[l]), expand, tri_q, qmask)
        o_b_p, hs_p, cs_p = _ssd(u_p, small_p, jnp.zeros((nb, CONV_PAD, B_XBC), F32),
                                 jnp.zeros((nb, B_WIDTH, LANES), F32), ssd_consts, nb, seq)
        cs0 = jnp.pad(state_b_conv[l].astype(F32), ((0, 0), (CONV_PAD - (B_CONV - 1), 0), (0, 0)))
        o_b_s, hs_s, cs_s = _ssd(u_s, small_s, cs0, _stack_state(state_b_ssm[l].astype(F32)), ssd_consts,
                                 ns, tdec)

        o_m_p = _cross(m_qs_p, mkt[None], mvt[None], 0, u_p, seq, "cross_prompt")
        o_m_s = _cross(m_qs_s, cmk, cmv, l, u_s, tdec, "cross_sample")

        w_merge = (w_pa[l].astype(BF16), w_pb[l].astype(BF16), w_pc[l].astype(BF16), w_pm[l].astype(BF16),
                   w_out[l].astype(BF16))
        xp = _merge(xp, gates_p, o_a_p, o_b_p, o_c_p, o_m_p, *w_merge)
        xs = _merge(xs, gates_s, o_a_s, o_b_s, o_c_s, o_m_s, *w_merge)

        outs['pa_k'].append(pa_kt)
        outs['pa_v'].append(pa_vt)
        outs['pc_k'].append(pc_kt)
        outs['pc_v'].append(pc_vt)
        outs['pc_f'].append(pc_ft)
        outs['pb_s'].append(_unstack_state(hs_p))
        outs['pb_c'].append(cs_p[:, CONV_PAD - (B_CONV - 1):])
        outs['pm_k'].append(mkt)
        outs['pm_v'].append(mvt)
        outs['sa_k'].append(sa_k4)
        outs['sa_v'].append(sa_v4)
        outs['sc_k'].append(sc_k4)
        outs['sc_v'].append(sc_v4)
        outs['sc_f'].append(lf_s[:, CF_LANE:CF_LANE + C_HEADS].reshape(ns, tdec, C_HEADS))
        outs['sb_s'].append(_unstack_state(hs_s))
        outs['sb_c'].append(cs_s[:, CONV_PAD - (B_CONV - 1):])

    st = jnp.stack

    def token_major(name, heads):
        a = st(outs[name])
        dd, n, _, t = a.shape
        return jnp.transpose(a.reshape(dd, n, heads, HEAD_DIM, t), (0, 1, 4, 2, 3))

    def sample_heads(name):
        return st(outs[name]).reshape(depth, ns, tdec, A_HEADS, HEAD_DIM)

    return (xp.reshape(nb, seq, d), xs.reshape(ns, tdec, d),
            token_major('pa_k', A_HEADS), token_major('pa_v', A_HEADS),
            token_major('pc_k', C_HEADS), token_major('pc_v', C_HEADS),
            jnp.transpose(st(outs['pc_f']), (0, 1, 3, 2)),
            st(outs['pb_s']), st(outs['pb_c']), token_major('pm_k', M_HEADS), token_major('pm_v', M_HEADS),
            sample_heads('sa_k'), sample_heads('sa_v'), sample_heads('sc_k'), sample_heads('sc_v'),
            st(outs['sc_f']), st(outs['sb_s']), st(outs['sb_c']))
```

```python
import functools

import numpy as np
import jax
import jax.numpy as jnp
from jax import lax
from jax.experimental import pallas as pl
from jax.experimental.pallas import tpu as pltpu

F32 = jnp.float32
BF16 = jnp.bfloat16

EPS = 1e-6
NEG = -1e30
LOG2E = 1.4426950408889634

HEAD_DIM = 64
CHUNK = 64
A_HEADS = 8
A_WIDTH = 512
A_LEFT_CHUNKS = 8
A_WIN = A_LEFT_CHUNKS * CHUNK
A_REL_CLIP = 128
B_HEADS = 8
B_WIDTH = 512
B_GROUPS = 2
B_STATE = 64
B_CONV = 4
B_XBC = B_WIDTH + 2 * B_GROUPS * B_STATE
C_HEADS = 8
C_WIDTH = 512
M_HEADS = 4
M_WIDTH = 256
N_BRANCH = 4

LANES = 128
SUBLANES = 8
MXU_WIDTH = 256
VMEM_LIMIT = 48 * 1024 * 1024

_SPLITS = (('a_q', A_WIDTH), ('a_k', A_WIDTH), ('a_v', A_WIDTH), ('a_z', A_WIDTH),
           ('b_z', B_WIDTH), ('b_xbc', B_XBC), ('b_dt', B_HEADS),
           ('c_q', C_WIDTH), ('c_k', C_WIDTH), ('c_v', C_WIDTH), ('c_f', C_HEADS), ('c_z', C_WIDTH),
           ('m_q', M_WIDTH), ('m_z', M_WIDTH), ('gate', N_BRANCH * 1024))
_SRC = {}
_off = 0
for _n, _w in _SPLITS:
    _SRC[_n] = (_off, _w)
    _off += _w

_U_ORDER = ('a_q', 'a_k', 'a_v', 'a_z', 'c_q', 'c_k', 'c_v', 'c_z', 'b_z', 'b_xbc', 'm_q', 'm_z', 'b_dt', 'c_f')
_U_OFF = {}
_off = 0
for _n in _U_ORDER:
    _U_OFF[_n] = _off
    _off += _SRC[_n][1]
U_WIDTH = 6144
DT_LANE = 0
CF_LANE = B_HEADS
AUG_PER_HEAD = 6


def _cp(sem):
    return pltpu.CompilerParams(dimension_semantics=sem, vmem_limit_bytes=VMEM_LIMIT)


def _pick(n, cands):
    for c in cands:
        if n % c == 0:
            return c
    raise ValueError(f"no tile for {n} in {cands}")


def _dot(a, b):
    return jnp.dot(a, b, preferred_element_type=F32)


def _dot_nt(a, b):
    return lax.dot_general(a, b, (((1,), (1,)), ((), ())), preferred_element_type=F32)


def _dot_tn(a, b):
    return lax.dot_general(a, b, (((0,), (0,)), ((), ())), preferred_element_type=F32)


def _split3(x):
    hi = x.astype(BF16)
    r = x - hi.astype(F32)
    mid = r.astype(BF16)
    lo = (r - mid.astype(F32)).astype(BF16)
    return hi, mid, lo


def _dot_sel(x, sel):
    hi, mid, lo = _split3(x)
    return _dot(hi, sel) + _dot(mid, sel) + _dot(lo, sel)


def _sel_dot(sel, x):
    hi, mid, lo = _split3(x)
    return _dot(sel, hi) + _dot(sel, mid) + _dot(sel, lo)


def _sigmoid(x):
    return 0.5 * jnp.tanh(0.5 * x) + 0.5


def _silu(x):
    return x * _sigmoid(x)


def _softplus(x):
    return jnp.maximum(x, 0.0) + jnp.log1p(jnp.exp(-jnp.abs(x)))


def _log_sigmoid(x):
    return jnp.minimum(x, 0.0) - jnp.log1p(jnp.exp(-jnp.abs(x)))


def _head_norm(x, bd):
    x2 = x * x
    hi = x2.astype(BF16)
    lo = (x2 - hi.astype(F32)).astype(BF16)
    w = bd.shape[0]
    ms = [_dot(hi[:, c:c + w], bd) + _dot(lo[:, c:c + w], bd) for c in range(0, x.shape[1], w)]
    ms = ms[0] if len(ms) == 1 else jnp.concatenate(ms, axis=1)
    return x * lax.rsqrt(ms + EPS)


def _lane_iota(shape):
    return lax.broadcasted_iota(jnp.int32, shape, len(shape) - 1)


def _row_iota(shape):
    return lax.broadcasted_iota(jnp.int32, shape, len(shape) - 2)


def _proj_kernel(x_ref, g_ref, w_ref, o_ref, *rest, act, w_rows, side):
    h_ref = rest[-1]

    @pl.when(pl.program_id(1) == 0)
    def _():
        x = x_ref[...]
        ms = jnp.mean(x * x, axis=-1, keepdims=True)
        h_ref[...] = (x * lax.rsqrt(ms + EPS) * g_ref[...]).astype(BF16)

    u = _dot_nt(h_ref[...], w_ref[...]) if w_rows else _dot(h_ref[...], w_ref[...])
    if side is not None:
        tile, col = side

        @pl.when(pl.program_id(1) == tile)
        def _():
            rest[0][...] = u[:, col:col + LANES]
    if act == 'sigmoid':
        u = _sigmoid(u)
    o_ref[...] = u.astype(o_ref.dtype)


def _proj(x, g_row, w_bf, *, act, out_dtype, name, w_rows, side_col=None):
    m, d = x.shape
    n = w_bf.shape[0] if w_rows else w_bf.shape[1]
    tm = _pick(m, (2048, 1024, 512, 256, 128))
    tn = _pick(n, (1024, 512, 256, 128))
    w_spec = pl.BlockSpec((tn, d), lambda i, j: (j, 0)) if w_rows else pl.BlockSpec((d, tn), lambda i, j: (0, j))
    side = None if side_col is None else (side_col // tn, side_col % tn)
    out_specs = [pl.BlockSpec((tm, tn), lambda i, j: (i, j))]
    out_shape = [jax.ShapeDtypeStruct((m, n), out_dtype)]
    if side is not None:
        out_specs.append(pl.BlockSpec((tm, LANES), lambda i, j: (i, 0)))
        out_shape.append(jax.ShapeDtypeStruct((m, LANES), F32))
    res = pl.pallas_call(
        functools.partial(_proj_kernel, act=act, w_rows=w_rows, side=side),
        grid=(m // tm, n // tn),
        in_specs=[pl.BlockSpec((tm, d), lambda i, j: (i, 0)),
                  pl.BlockSpec((1, d), lambda i, j: (0, 0)),
                  w_spec],
        out_specs=out_specs,
        out_shape=out_shape,
        scratch_shapes=[pltpu.VMEM((tm, d), BF16)],
        compiler_params=_cp(("parallel", "arbitrary")),
        name=name,
    )(x, g_row, w_bf)
    return res if side is not None else res[0]


PREP_TS = 512
VT_CHUNK = 256


def _prep_common(aq, ak, av, cq, ck, cv, mq, sm, gaq, gak, gcq, gck, gmq, fb, bd):
    bdv = bd[...]
    f32 = lambda ref: ref[...].astype(F32)
    return dict(
        a_qs=(_head_norm(f32(aq), bdv) * gaq[...]).astype(BF16),
        a_kn=_head_norm(f32(ak), bdv) * gak[...],
        a_v=f32(av),
        c_qs=(_head_norm(f32(cq), bdv) * gcq[...]).astype(BF16),
        c_kn=_head_norm(f32(ck), bdv) * gck[...],
        c_v=f32(cv),
        m_qs=(_head_norm(f32(mq), bdv) * gmq[...]).astype(BF16),
        lf=_log_sigmoid(sm[...] + fb[...]))


def _prep_prompt_kernel(aq, ak, av, cq, ck, cv, mq, sm, gaq, gak, gcq, gck, gmq, fb, bd,
                        o_aq, o_akb, o_avtb, o_akt, o_avt, o_cq, o_ckb, o_ckt, o_cvt, o_cvtb, o_mq, o_lf, o_lft,
                        *, steps_per_seq):
    r = _prep_common(aq, ak, av, cq, ck, cv, mq, sm, gaq, gak, gcq, gck, gmq, fb, bd)
    o_aq[...] = r['a_qs']
    o_akb[...] = r['a_kn'].astype(BF16)
    avt = r['a_v'].T
    avtb = avt.astype(BF16)
    for c in range(PREP_TS // VT_CHUNK):
        o_avtb[c] = avtb[:, c * VT_CHUNK:(c + 1) * VT_CHUNK]

    @pl.when(pl.program_id(0) % steps_per_seq == steps_per_seq - 1)
    def _():
        o_akt[...] = r['a_kn'].T
        o_avt[...] = avt

    o_cq[...] = r['c_qs']
    o_ckb[...] = r['c_kn'].astype(BF16)
    o_ckt[...] = r['c_kn'].T
    cvt = r['c_v'].T
    o_cvt[...] = cvt
    cvtb = cvt.astype(BF16)
    for c in range(PREP_TS // VT_CHUNK):
        o_cvtb[c] = cvtb[:, c * VT_CHUNK:(c + 1) * VT_CHUNK]
    o_mq[...] = r['m_qs']
    o_lf[...] = r['lf']
    o_lft[...] = r['lf'].T[CF_LANE:CF_LANE + C_HEADS, :]


def _prep_sample_kernel(aq, ak, av, cq, ck, cv, mq, sm, gaq, gak, gcq, gck, gmq, fb, bd,
                        o_aq, o_akb, o_avb, o_ak4, o_av4, o_cq, o_ckb, o_cvb, o_ck4, o_cv4, o_mq, o_lf):
    r = _prep_common(aq, ak, av, cq, ck, cv, mq, sm, gaq, gak, gcq, gck, gmq, fb, bd)
    ts = o_aq.shape[0]
    o_aq[...] = r['a_qs']
    o_akb[...] = r['a_kn'].astype(BF16)
    o_avb[...] = r['a_v'].astype(BF16)
    o_ak4[...] = r['a_kn'].reshape(ts, A_HEADS, HEAD_DIM)
    o_av4[...] = r['a_v'].reshape(ts, A_HEADS, HEAD_DIM)
    o_cq[...] = r['c_qs']
    o_ckb[...] = r['c_kn'].astype(BF16)
    o_cvb[...] = r['c_v'].astype(BF16)
    o_ck4[...] = r['c_kn'].reshape(ts, C_HEADS, HEAD_DIM)
    o_cv4[...] = r['c_v'].reshape(ts, C_HEADS, HEAD_DIM)
    o_mq[...] = r['m_qs']
    o_lf[...] = r['lf']


def _prep_in_specs(ts):
    def ucol(name, width):
        idx = _U_OFF[name] // width
        assert idx * width == _U_OFF[name]
        return pl.BlockSpec((ts, width), lambda i: (i, idx))

    def row(width):
        return pl.BlockSpec((1, width), lambda i: (0, 0))

    return [ucol('a_q', 512), ucol('a_k', 512), ucol('a_v', 512),
            ucol('c_q', 512), ucol('c_k', 512), ucol('c_v', 512),
            ucol('m_q', 256), pl.BlockSpec((ts, LANES), lambda i: (i, 0)),
            row(512), row(512), row(512), row(512), row(256), row(LANES),
            pl.BlockSpec((MXU_WIDTH, MXU_WIDTH), lambda i: (0, 0))]


def _prep_prompt(u, small, gains, nb, seq):
    ts = PREP_TS
    assert seq % ts == 0 and min(A_WIN, seq) == ts
    sps = seq // ts
    m = nb * seq
    sds = jax.ShapeDtypeStruct
    rows = lambda width: pl.BlockSpec((ts, width), lambda i: (i, 0))
    last = pl.BlockSpec((None, 512, ts), lambda i: (i // sps, 0, 0))
    feat = pl.BlockSpec((None, 512, ts), lambda i: (i // sps, 0, i % sps))
    nvc = ts // VT_CHUNK
    chunks = pl.BlockSpec((None, nvc, 512, VT_CHUNK), lambda i: (i // sps, i % sps, 0, 0))
    out_specs = [rows(512), rows(512), chunks, last, last,
                 rows(512), rows(512), feat, feat, chunks,
                 rows(256), rows(LANES),
                 pl.BlockSpec((None, C_HEADS, ts), lambda i: (i // sps, 0, i % sps))]
    out_shape = [sds((m, 512), BF16), sds((m, 512), BF16), sds((nb, seq // VT_CHUNK, 512, VT_CHUNK), BF16),
                 sds((nb, 512, ts), F32), sds((nb, 512, ts), F32),
                 sds((m, 512), BF16), sds((m, 512), BF16), sds((nb, 512, seq), F32), sds((nb, 512, seq), F32),
                 sds((nb, seq // VT_CHUNK, 512, VT_CHUNK), BF16),
                 sds((m, 256), BF16), sds((m, LANES), F32), sds((nb, C_HEADS, seq), F32)]
    return pl.pallas_call(
        functools.partial(_prep_prompt_kernel, steps_per_seq=sps),
        grid=(m // ts,),
        in_specs=_prep_in_specs(ts),
        out_specs=out_specs,
        out_shape=out_shape,
        compiler_params=_cp(("arbitrary",)),
        name="prep_prompt",
    )(*([u] * 7), small, *gains)


def _prep_sample(u, small, gains):
    m = u.shape[0]
    ts = _pick(m, (512, 256, 128, 64))
    sds = jax.ShapeDtypeStruct
    rows = lambda width: pl.BlockSpec((ts, width), lambda i: (i, 0))
    heads = pl.BlockSpec((ts, A_HEADS, HEAD_DIM), lambda i: (i, 0, 0))
    return pl.pallas_call(
        _prep_sample_kernel,
        grid=(m // ts,),
        in_specs=_prep_in_specs(ts),
        out_specs=[rows(512), rows(512), rows(512), heads, heads,
                   rows(512), rows(512), rows(512), heads, heads, rows(256), rows(LANES)],
        out_shape=[sds((m, 512), BF16), sds((m, 512), BF16), sds((m, 512), BF16),
                   sds((m, A_HEADS, HEAD_DIM), F32), sds((m, A_HEADS, HEAD_DIM), F32),
                   sds((m, 512), BF16), sds((m, 512), BF16), sds((m, 512), BF16),
                   sds((m, C_HEADS, HEAD_DIM), F32), sds((m, C_HEADS, HEAD_DIM), F32),
                   sds((m, 256), BF16), sds((m, LANES), F32)],
        compiler_params=_cp(("parallel",)),
        name="prep_sample",
    )(*([u] * 7), small, *gains)


def _memkv_kernel(kv, g, bd, o_kt, o_vt):
    o_kt[...] = (_head_norm(kv[:, :M_WIDTH], bd[...]) * g[...]).T
    o_vt[...] = kv[:, M_WIDTH:].T


def _memkv(kv, g_row, bd256, nb, n_mem):
    spec = pl.BlockSpec((None, M_WIDTH, n_mem), lambda b: (b, 0, 0))
    return pl.pallas_call(
        _memkv_kernel,
        grid=(nb,),
        in_specs=[pl.BlockSpec((n_mem, 2 * M_WIDTH), lambda b: (b, 0)),
                  pl.BlockSpec((1, M_WIDTH), lambda b: (0, 0)),
                  pl.BlockSpec((M_WIDTH, M_WIDTH), lambda b: (0, 0))],
        out_specs=[spec, spec],
        out_shape=[jax.ShapeDtypeStruct((nb, M_WIDTH, n_mem), F32)] * 2,
        compiler_params=_cp(("parallel",)),
        name="memkv",
    )(kv, g_row, bd256)


def _cum_kernel(lf_ref, tri_ref, selq_ref, selk_ref, qc_ref, kc_ref, low_ref, cq_ref, ck_ref, carry_ref,
                *, nchunk, unroll):
    ncp = carry_ref.shape[0]
    totals = lf_ref[...].reshape(nchunk, LANES, LANES).sum(axis=1)
    if ncp > nchunk:
        totals = jnp.concatenate([totals, jnp.zeros((ncp - nchunk, LANES), F32)], axis=0)
    carry_ref[...] = _sel_dot(low_ref[...], totals)
    tri = tri_ref[...]

    def body(cc, carry):
        cs = [cc * unroll + k for k in range(unroll)]
        rs = [pl.multiple_of(c * LANES, LANES) for c in cs]
        xs = [lf_ref[pl.ds(r, LANES), :] for r in rs]
        zs = [(_sel_dot(tri, x) + carry_ref[pl.ds(c, 1), :]) * LOG2E for x, c in zip(xs, cs)]
        parts = [_split3(z) for z in zs]
        qs = [_dot(p[0], selq_ref[0]) + _dot(p[1], selq_ref[1]) + _dot(p[2], selq_ref[2]) + qc_ref[...]
              for p in parts]
        ks = [_dot(p[0], selk_ref[0]) + _dot(p[1], selk_ref[1]) + _dot(p[2], selk_ref[2]) + kc_ref[...]
              for p in parts]
        for r, qv, kv in zip(rs, qs, ks):
            cq_ref[pl.ds(r, LANES), :] = qv.astype(BF16)
            ck_ref[pl.ds(r, LANES), :] = kv.astype(BF16)
        return carry

    lax.fori_loop(0, nchunk // unroll, body, 0)


CUM_UNROLLS = (8, 4, 3, 2, 1)


def _cum(lf, consts):
    nseq, length, _ = lf.shape
    assert length % LANES == 0
    nchunk = length // LANES
    unroll = _pick(nchunk, CUM_UNROLLS)
    ncp = -(-nchunk // SUBLANES) * SUBLANES
    tri, selq, selk, qc, kc = consts
    low = jnp.asarray(np.tril(np.ones((ncp, ncp), np.float32), -1), BF16)
    full = lambda shape: pl.BlockSpec(shape, lambda s: (0,) * len(shape))
    seq = pl.BlockSpec((None, length, LANES), lambda s: (s, 0, 0))
    return pl.pallas_call(
        functools.partial(_cum_kernel, nchunk=nchunk, unroll=unroll),
        grid=(nseq,),
        in_specs=[seq, full((LANES, LANES)), full((3, LANES, LANES)), full((3, LANES, LANES)),
                  full((1, LANES)), full((1, LANES)), full((ncp, ncp))],
        out_specs=[seq, seq],
        out_shape=[jax.ShapeDtypeStruct(lf.shape, BF16)] * 2,
        scratch_shapes=[pltpu.VMEM((ncp, LANES), F32)],
        compiler_params=_cp(("parallel",)),
        name="cum",
    )(lf, tri, selq, selk, qc, kc, low)


def _cum_consts():
    tri = np.tril(np.ones((LANES, LANES), np.float32))
    selq = np.zeros((3, LANES, LANES), np.float32)
    selk = np.zeros((3, LANES, LANES), np.float32)
    qc = np.zeros((1, LANES), np.float32)
    kc = np.zeros((1, LANES), np.float32)
    for h in range(C_HEADS):
        for p in range(3):
            selq[p, CF_LANE + h, AUG_PER_HEAD * h + p] = 1.0
            selk[p, CF_LANE + h, AUG_PER_HEAD * h + 3 + p] = -1.0
            qc[0, AUG_PER_HEAD * h + 3 + p] = 1.0
            kc[0, AUG_PER_HEAD * h + p] = 1.0
    return (jnp.asarray(tri, BF16), jnp.asarray(selq, BF16), jnp.asarray(selk, BF16),
            jnp.asarray(qc), jnp.asarray(kc))


BAND_Q = 4 * CHUNK
BAND_K = 3 * BAND_Q
TAB_PAD = 384
DIAG_PAD = 1024


def _bias_kernel(tab_ref, idx_ref, neg_ref, o_ref):
    nrows, ncols = neg_ref.shape
    onehot = (lax.broadcasted_iota(jnp.int32, (TAB_PAD, DIAG_PAD), 0) == idx_ref[...]).astype(BF16)
    diag = _dot_sel(tab_ref[...], onehot) * LOG2E
    for h in range(A_HEADS):
        rows = jnp.broadcast_to(diag[h:h + 1, :], (nrows, DIAG_PAD))
        o_ref[h] = pltpu.roll(rows, 0, 1, stride=1, stride_axis=0)[:, :ncols] + neg_ref[...]


def _band_bias(table, nq, nk, key_major):
    assert nq + nk <= DIAG_PAD
    r = np.arange(nq)[None, :] if key_major else np.arange(nq)[:, None]
    t = np.arange(nk)[:, None] if key_major else np.arange(nk)[None, :]
    cb = t // CHUNK - r // CHUNK
    neg = np.where((cb >= 0) & (cb <= A_LEFT_CHUNKS), 0.0, NEG).astype(np.float32)
    p = np.arange(DIAG_PAD)
    ncols = nq if key_major else nk
    col_minus_row = np.where(p < ncols, p, p - DIAG_PAD)
    q_minus_k = col_minus_row if key_major else -col_minus_row
    idx = (np.clip(A_WIN + q_minus_k, -A_REL_CLIP, A_REL_CLIP) + A_REL_CLIP).astype(np.int32).reshape(1, -1)
    tab = jnp.zeros((A_HEADS, TAB_PAD), F32).at[:, :table.shape[0]].set(table.T)
    full = lambda shape: pl.BlockSpec(shape, lambda: (0,) * len(shape))
    return pl.pallas_call(
        _bias_kernel,
        in_specs=[full((A_HEADS, TAB_PAD)), full((1, DIAG_PAD)), full(neg.shape)],
        out_specs=full((A_HEADS,) + neg.shape),
        out_shape=jax.ShapeDtypeStruct((A_HEADS,) + neg.shape, F32),
        compiler_params=pltpu.CompilerParams(vmem_limit_bytes=VMEM_LIMIT),
        name="band_bias",
    )(tab, jnp.asarray(idx), jnp.asarray(neg))


def _half_mask(par):
    lane = _lane_iota((1, LANES))
    return (lane < HEAD_DIM) if par == 0 else (lane >= HEAD_DIM)


def _softmax_pv_many(s_lists, v_lists, v_feature_major=None):
    n = len(s_lists)
    nblk = len(s_lists[0])
    if v_feature_major is None:
        v_feature_major = (False,) * nblk
    ms = []
    for sb in s_lists:
        m = sb[0].max(axis=1, keepdims=True)
        for s in sb[1:]:
            m = jnp.maximum(m, s.max(axis=1, keepdims=True))
        ms.append(m)
    ps = [[jnp.exp2(s - ms[i]) for s in s_lists[i]] for i in range(n)]
    ls = []
    for i in range(n):
        l = ps[i][0].sum(axis=1, keepdims=True)
        for p in ps[i][1:]:
            l = l + p.sum(axis=1, keepdims=True)
        ls.append(l)
    outs = []
    for i in range(n):
        acc = None
        for p, v, fm in zip(ps[i], v_lists[i], v_feature_major):
            pv = _dot_nt(p.astype(BF16), v) if fm else _dot(p.astype(BF16), v)
            acc = pv if acc is None else acc + pv
        outs.append(acc)
    return [o / l for o, l in zip(outs, ls)]


ONES_ROWS = 16


def _band_prompt_kernel(q_ref, k0, k1, k2, v0, v1, v2, bias_ref, z_ref, o_ref):
    g = pl.program_id(1)
    krefs = (k0, k1, k2)
    vrefs = (v0, v1, v2)
    s_lists = []
    for hp in range(A_HEADS // 2):
        cols = slice(hp * LANES, (hp + 1) * LANES)
        qp = q_ref[:, cols]
        ks = [kr[:, cols] for kr in krefs]
        for par in range(2):
            h = 2 * hp + par
            qm = jnp.where(_half_mask(par), qp, jnp.zeros_like(qp))
            sb = []
            for j in range(3):
                s = _dot_nt(ks[j], qm) + bias_ref[h, j * BAND_Q:(j + 1) * BAND_Q, :]
                if j < 2:
                    s = jnp.where(g + j >= 2, s, NEG)
                sb.append(s)
            s_lists.append(sb)
    ms = []
    for sb in s_lists:
        m = sb[0].max(axis=0, keepdims=True)
        for s in sb[1:]:
            m = jnp.maximum(m, s.max(axis=0, keepdims=True))
        ms.append(m)
    ps = [[jnp.exp2(s - m).astype(BF16) for s in sb] for sb, m in zip(s_lists, ms)]
    ones = jnp.ones((ONES_ROWS, BAND_Q), BF16)
    outs = []
    for h in range(A_HEADS):
        acc = None
        for j in range(3):
            vt = jnp.concatenate([vrefs[j][h * HEAD_DIM:(h + 1) * HEAD_DIM, :], ones], axis=0)
            d = _dot(vt, ps[h][j])
            acc = d if acc is None else acc + d
        outs.append(acc[:HEAD_DIM] / acc[HEAD_DIM:HEAD_DIM + 1])
    o_ref[...] = (jnp.concatenate(outs, axis=0).T * _silu(z_ref[...].astype(F32))).astype(BF16)


def _band_prompt(qs, kb, vtb, bias_t, u, nb, seq):
    assert VT_CHUNK == BAND_Q
    ng = seq // BAND_Q

    def kblk(j):
        return pl.BlockSpec((BAND_Q, A_WIDTH), lambda b, g: (b * ng + jnp.maximum(g - 2 + j, 0), 0))

    def vblk(j):
        return pl.BlockSpec((None, None, A_WIDTH, BAND_Q), lambda b, g: (b, jnp.maximum(g - 2 + j, 0), 0, 0))

    cur = pl.BlockSpec((BAND_Q, A_WIDTH), lambda b, g: (b * ng + g, 0))
    zidx = _U_OFF['a_z'] // A_WIDTH
    return pl.pallas_call(
        _band_prompt_kernel,
        grid=(nb, ng),
        in_specs=[cur, kblk(0), kblk(1), kblk(2), vblk(0), vblk(1), vblk(2),
                  pl.BlockSpec((A_HEADS, BAND_K, BAND_Q), lambda b, g: (0, 0, 0)),
                  pl.BlockSpec((BAND_Q, A_WIDTH), lambda b, g: (b * ng + g, zidx))],
        out_specs=cur,
        out_shape=jax.ShapeDtypeStruct((nb * seq, A_WIDTH), BF16),
        compiler_params=_cp(("parallel", "parallel")),
        name="band_prompt",
    )(qs, kb, kb, kb, vtb, vtb, vtb, bias_t, u)


def _band_sample_kernel(q_ref, kc_ref, kn_ref, vc_ref, vn_ref, bias_ref, z_ref, o_ref, *, t):
    lane = _lane_iota((1, LANES))
    lc = kc_ref.shape[1]
    pad = jnp.zeros((LANES - t, LANES), BF16)
    s_lists, v_lists = [], []
    for hp in range(A_HEADS // 2):
        cols = slice(hp * LANES, (hp + 1) * LANES)
        qp = q_ref[:, cols]
        kc = kc_ref[cols, :].astype(BF16)
        vc = vc_ref[cols, :].astype(BF16)
        kn = jnp.concatenate([kn_ref[:, cols], pad], axis=0)
        vn = jnp.concatenate([vn_ref[:, cols], pad], axis=0)
        for par in range(2):
            h = 2 * hp + par
            qm = jnp.where(_half_mask(par), qp, jnp.zeros_like(qp))
            s_lists.append([_dot(qm, kc) + bias_ref[h, :, :lc], _dot_nt(qm, kn) + bias_ref[h, :, lc:]])
            v_lists.append([vc, vn])
    outs = _softmax_pv_many(s_lists, v_lists, (True, False))
    for hp in range(A_HEADS // 2):
        cols = slice(hp * LANES, (hp + 1) * LANES)
        o = jnp.where(lane < HEAD_DIM, outs[2 * hp], outs[2 * hp + 1])
        o_ref[:, cols] = (o * _silu(z_ref[:, cols].astype(F32))).astype(BF16)


def _band_sample(qs, kb, vb, cache_kt, cache_vt, layer, bias_s, u, nseq, t):
    lc = cache_kt.shape[3]
    new = pl.BlockSpec((t, A_WIDTH), lambda s: (s, 0))
    cache = pl.BlockSpec((None, None, A_WIDTH, lc), lambda s: (layer, s, 0, 0))
    return pl.pallas_call(
        functools.partial(_band_sample_kernel, t=t),
        grid=(nseq,),
        in_specs=[new, cache, new, cache, new,
                  pl.BlockSpec(bias_s.shape, lambda s: (0, 0, 0)),
                  pl.BlockSpec((t, A_WIDTH), lambda s: (s, _U_OFF['a_z'] // A_WIDTH))],
        out_specs=new,
        out_shape=jax.ShapeDtypeStruct((nseq * t, A_WIDTH), BF16),
        compiler_params=_cp(("parallel",)),
        name="band_sample",
    )(qs, cache_kt, kb, cache_vt, vb, bias_s, u)


FOX_TQ = 512
FOX_TK = 512


def _aug_mask(h):
    lane = _lane_iota((1, LANES))
    return (lane >= AUG_PER_HEAD * h) & (lane < AUG_PER_HEAD * (h + 1))


def _fox_prompt_kernel(q_ref, cq_ref, k_ref, ck_ref, vt_ref, z_ref, o_ref, st_ref, m_ref, acc_ref):
    hp = pl.program_id(1)
    i = pl.program_id(2)
    qp = q_ref[...]
    cq = cq_ref[...]
    qcs = []
    for par in range(2):
        qm = jnp.where(_half_mask(par), qp, jnp.zeros_like(qp))
        qa = jnp.where(_aug_mask(2 * hp + par), cq, jnp.zeros_like(cq))
        qcs.append(jnp.concatenate([qm, qa], axis=1))
    qcat = jnp.concatenate(qcs, axis=0)
    m_ref[...] = jnp.full(m_ref.shape, NEG, F32)
    acc_ref[...] = jnp.zeros(acc_ref.shape, F32)
    ratio = FOX_TK // FOX_TQ
    nvc = FOX_TK // VT_CHUNK
    ones = jnp.ones((ONES_ROWS, VT_CHUNK), BF16)

    def scores(j, slot):
        off = pl.multiple_of(j * FOX_TK, FOX_TK)
        kc = jnp.concatenate([k_ref[pl.ds(off, FOX_TK), :], ck_ref[pl.ds(off, FOX_TK), :]], axis=1)
        st_ref[slot] = _dot_nt(kc, qcat)

    def reduce(j, slot, masked):
        st = st_ref[slot]
        if masked:
            shape = (FOX_TK, 2 * FOX_TQ)
            qpos = (_lane_iota(shape) & (FOX_TQ - 1)) + (i % ratio) * FOX_TQ
            st = jnp.where(_row_iota(shape) <= qpos, st, NEG)
        m_old = m_ref[...]
        m_new = jnp.maximum(m_old, st.max(axis=0, keepdims=True))
        alpha = jnp.exp2(m_old - m_new)
        pb = jnp.exp2(st - m_new).astype(BF16)
        for par in range(2):
            cols = slice(par * FOX_TQ, (par + 1) * FOX_TQ)
            pv = None
            for c in range(nvc):
                vt = jnp.concatenate([vt_ref[j * nvc + c, par * HEAD_DIM:(par + 1) * HEAD_DIM, :], ones], axis=0)
                d = _dot(vt, pb[c * VT_CHUNK:(c + 1) * VT_CHUNK, cols])
                pv = d if pv is None else pv + d
            acc_ref[par] = alpha[:, cols] * acc_ref[par] + pv
        m_ref[...] = m_new

    nfull = i // ratio
    scores(0, 0)

    def body(jj, carry):
        j = 2 * jj
        scores(j + 1, 1)
        reduce(j, 0, False)
        scores(j + 2, 0)
        reduce(j + 1, 1, False)
        return carry

    lax.fori_loop(0, nfull // 2, body, 0)

    @pl.when(nfull % 2 == 0)
    def _():
        reduce(nfull, 0, True)

    @pl.when(nfull % 2 == 1)
    def _():
        scores(nfull, 1)
        reduce(nfull - 1, 0, False)
        reduce(nfull, 1, True)

    ot = jnp.concatenate([acc_ref[par, :HEAD_DIM] / acc_ref[par, HEAD_DIM:HEAD_DIM + 1] for par in range(2)],
                         axis=0)
    o_ref[...] = (ot.T * _silu(z_ref[...].astype(F32))).astype(BF16)


def _fox_prompt(qs, kb, vtb, cq, ck, u, nb, seq):
    nq = seq // FOX_TQ
    qspec = pl.BlockSpec((FOX_TQ, LANES), lambda b, hp, i: (b * nq + i, hp))
    cqspec = pl.BlockSpec((FOX_TQ, LANES), lambda b, hp, i: (b * nq + i, 0))
    kspec = pl.BlockSpec((seq, LANES), lambda b, hp, i: (b, hp))
    ckspec = pl.BlockSpec((seq, LANES), lambda b, hp, i: (b, 0))
    vspec = pl.BlockSpec((None, seq // VT_CHUNK, LANES, VT_CHUNK), lambda b, hp, i: (b, 0, hp, 0))
    z0 = _U_OFF['c_z'] // LANES
    zspec = pl.BlockSpec((FOX_TQ, LANES), lambda b, hp, i: (b * nq + i, z0 + hp))
    return pl.pallas_call(
        _fox_prompt_kernel,
        grid=(nb, C_HEADS // 2, nq),
        in_specs=[qspec, cqspec, kspec, ckspec, vspec, zspec],
        out_specs=qspec,
        out_shape=jax.ShapeDtypeStruct((nb * seq, C_WIDTH), BF16),
        scratch_shapes=[pltpu.VMEM((2, FOX_TK, 2 * FOX_TQ), F32),
                        pltpu.VMEM((1, 2 * FOX_TQ), F32),
                        pltpu.VMEM((2, HEAD_DIM + ONES_ROWS, FOX_TQ), F32)],
        compiler_params=_cp(("parallel", "parallel", "arbitrary")),
        name="fox_prompt",
    )(qs, cq, kb, ck, vtb, u)


def _fox_sample_kernel(q_ref, cq_ref, kc_ref, kn_ref, ck_ref, vc_ref, vn_ref, z_ref, o_ref, *, t, past):
    lane = _lane_iota((1, LANES))
    cq = cq_ref[past:past + t, :]
    pad = jnp.zeros((LANES - t, LANES), BF16)
    ck_cache_t = ck_ref[:past, :].astype(F32).T.astype(BF16)
    ck_new = ck_ref[past:, :]
    vis = _lane_iota((t, LANES)) <= _row_iota((t, LANES))
    s_lists, v_lists = [], []
    for hp in range(C_HEADS // 2):
        cols = slice(hp * LANES, (hp + 1) * LANES)
        qp = q_ref[:, cols]
        kc = jnp.concatenate([kc_ref[cols, :].astype(BF16), ck_cache_t], axis=0)
        kn = jnp.concatenate([jnp.concatenate([kn_ref[:, cols], pad], axis=0), ck_new], axis=1)
        vc = vc_ref[cols, :].astype(BF16)
        vn = jnp.concatenate([vn_ref[:, cols], pad], axis=0)
        for par in range(2):
            qm = jnp.where(_half_mask(par), qp, jnp.zeros_like(qp))
            qa = jnp.where(_aug_mask(2 * hp + par), cq, jnp.zeros_like(cq))
            qc = jnp.concatenate([qm, qa], axis=1)
            s_lists.append([_dot(qc, kc), jnp.where(vis, _dot_nt(qc, kn), NEG)])
            v_lists.append([vc, vn])
    outs = _softmax_pv_many(s_lists, v_lists, (True, False))
    for hp in range(C_HEADS // 2):
        cols = slice(hp * LANES, (hp + 1) * LANES)
        o = jnp.where(lane < HEAD_DIM, outs[2 * hp], outs[2 * hp + 1])
        o_ref[:, cols] = (o * _silu(z_ref[:, cols].astype(F32))).astype(BF16)


def _fox_sample(qs, kb, vb, cq, ck, cache_kt, cache_vt, layer, u, nseq, t):
    past = cache_kt.shape[3]
    new = pl.BlockSpec((t, C_WIDTH), lambda s: (s, 0))
    cache = pl.BlockSpec((None, None, C_WIDTH, past), lambda s: (layer, s, 0, 0))
    aug = pl.BlockSpec((None, past + LANES, LANES), lambda s: (s, 0, 0))
    return pl.pallas_call(
        functools.partial(_fox_sample_kernel, t=t, past=past),
        grid=(nseq,),
        in_specs=[new, aug, cache, new, aug, cache, new,
                  pl.BlockSpec((t, C_WIDTH), lambda s: (s, _U_OFF['c_z'] // C_WIDTH))],
        out_specs=new,
        out_shape=jax.ShapeDtypeStruct((nseq * t, C_WIDTH), BF16),
        compiler_params=_cp(("parallel",)),
        name="fox_sample",
    )(qs, cq, cache_kt, kb, ck, cache_vt, vb, u)


def _cross_kernel(q_ref, k_ref, v_ref, z_ref, o_ref):
    lane = _lane_iota((1, LANES))
    s_lists, v_lists = [], []
    for hp in range(M_HEADS // 2):
        cols = slice(hp * LANES, (hp + 1) * LANES)
        qp = q_ref[:, cols]
        k = k_ref[cols, :].astype(BF16)
        v = v_ref[cols, :].astype(BF16)
        for par in range(2):
            qm = jnp.where(_half_mask(par), qp, jnp.zeros_like(qp))
            s_lists.append([_dot(qm, k)])
            v_lists.append([v])
    outs = _softmax_pv_many(s_lists, v_lists, (True,))
    for hp in range(M_HEADS // 2):
        cols = slice(hp * LANES, (hp + 1) * LANES)
        o = jnp.where(lane < HEAD_DIM, outs[2 * hp], outs[2 * hp + 1])
        o_ref[:, cols] = (o * _silu(z_ref[:, cols].astype(F32))).astype(BF16)


def _cross(qs, mkt, mvt, layer, u, rows_per_seq, name):
    nrows = qs.shape[0]
    n_mem = mkt.shape[3]
    tq = _pick(rows_per_seq, (512, 256, 128, 64))
    per = rows_per_seq // tq
    mem = pl.BlockSpec((None, None, M_WIDTH, n_mem), lambda i: (layer, i // per, 0, 0))
    rows = pl.BlockSpec((tq, M_WIDTH), lambda i: (i, 0))
    return pl.pallas_call(
        _cross_kernel,
        grid=(nrows // tq,),
        in_specs=[rows, mem, mem, pl.BlockSpec((tq, M_WIDTH), lambda i: (i, _U_OFF['m_z'] // M_WIDTH))],
        out_specs=rows,
        out_shape=jax.ShapeDtypeStruct((nrows, M_WIDTH), BF16),
        compiler_params=_cp(("parallel",)),
        name=name,
    )(qs, mkt, mvt, u)


SSD_Q = 128
CONV_PAD = SUBLANES
SSD_GROUP = 4


def _ssd_kernel(xbc_ref, z_ref, sm_ref, cs0_ref, hs0_ref, cw_ref, cb_ref, dtb_ref, alog_ref, dsk_ref,
                gn_ref, exp_ref, tri_ref, qm_ref,
                o_ref, hs_out_ref, cs_out_ref, hs_ref, xext_ref, *, nv, nchunk):
    c = pl.program_id(1)
    q = SSD_Q
    grp_ids = range(SSD_GROUP)
    lane = _lane_iota((1, LANES))
    head_lane = (lane >= DT_LANE) & (lane < DT_LANE + B_HEADS)
    a_row = jnp.where(head_lane, -jnp.exp(alog_ref[...]) * LOG2E, 0.0)
    causal = _lane_iota((q, q)) <= _row_iota((q, q))
    heads_per_group = B_HEADS // B_GROUPS
    expand = exp_ref[...]
    tri = tri_ref[...]
    qm = qm_ref[...]

    @pl.when(c == 0)
    def _():
        for g in grp_ids:
            hs_ref[g] = hs0_ref[g].T * qm
            xext_ref[g, 0:CONV_PAD, :] = cs0_ref[g]

    for g in grp_ids:
        xext_ref[g, CONV_PAD:CONV_PAD + nv, :] = xbc_ref[g].astype(F32)
        if nv < q:
            xext_ref[g, CONV_PAD + nv:CONV_PAD + q, :] = jnp.zeros((q - nv, B_XBC), F32)
    xa = []
    for g in grp_ids:
        y = cb_ref[...]
        for tap in range(B_CONV):
            y = y + xext_ref[g, pl.ds(CONV_PAD - (B_CONV - 1) + tap, q), :] * cw_ref[tap:tap + 1, :]
        xa.append(_silu(y))
    for g in grp_ids:
        tail = xext_ref[g, nv:nv + CONV_PAD, :]
        xext_ref[g, 0:CONV_PAD, :] = tail
        cs_out_ref[g] = tail
    xs = [x[:, :B_WIDTH] for x in xa]
    bm = [x[:, B_WIDTH:B_WIDTH + LANES].astype(BF16) for x in xa]
    cm = [x[:, B_WIDTH + LANES:] for x in xa]

    dt = []
    for g in grp_ids:
        d = _softplus(sm_ref[g] + dtb_ref[...])
        if nv < q:
            d = jnp.concatenate([d, jnp.zeros((q - nv, LANES), F32)], axis=0)
        dt.append(jnp.where(head_lane, d, 0.0))
    acum = [_sel_dot(tri, d * a_row) for d in dt]
    dt_e = [_dot_sel(d, expand) for d in dt]
    acum_e = [_dot_sel(a, expand) for a in acum]
    alast_e = [a[q - 1:q, :] for a in acum_e]
    xdt = [x * d for x, d in zip(xs, dt_e)]

    hs = [hs_ref[g] for g in grp_ids]
    y_off = [_dot(cm[g].astype(BF16), hs[g].astype(BF16)) * jnp.exp2(acum_e[g]) for g in grp_ids]
    upd = [_dot_tn(bm[g], (xdt[g] * jnp.exp2(alast_e[g] - acum_e[g])).astype(BF16)) for g in grp_ids]
    hs_new = [(jnp.exp2(alast_e[g]) * hs[g] + upd[g]) * qm for g in grp_ids]
    for g in grp_ids:
        hs_ref[g] = hs_new[g]

    @pl.when(c == nchunk - 1)
    def _():
        for g in grp_ids:
            hs_out_ref[g] = hs_new[g].T

    acum_t = [a.T for a in acum]
    xdt_b = [x.astype(BF16) for x in xdt]
    cbm = [[None] * B_GROUPS for _ in grp_ids]
    for grp in range(B_GROUPS):
        for g in grp_ids:
            cg = jnp.where(_half_mask(grp), cm[g], 0.0).astype(BF16)
            cbm[g][grp] = _dot_nt(cg, bm[g])
    y_pairs = [[] for _ in grp_ids]
    for hp in range(B_HEADS // 2):
        pair = [[] for _ in grp_ids]
        for par in range(2):
            h = 2 * hp + par
            for g in grp_ids:
                seg = acum[g][:, DT_LANE + h:DT_LANE + h + 1] - acum_t[g][DT_LANE + h:DT_LANE + h + 1, :]
                lmat = jnp.exp2(jnp.where(causal, seg, -jnp.inf))
                m = (cbm[g][h // heads_per_group] * lmat).astype(BF16)
                pair[g].append(_dot(m, xdt_b[g][:, hp * LANES:(hp + 1) * LANES]))
        for g in grp_ids:
            y_pairs[g].append(jnp.where(lane < HEAD_DIM, pair[g][0], pair[g][1]))

    for g in grp_ids:
        yt = y_off[g] + jnp.concatenate(y_pairs[g], axis=1) + dsk_ref[...] * xs[g]
        if nv < q:
            yt = yt[:nv]
        yz = yt * _silu(z_ref[g].astype(F32))
        ms = jnp.mean(yz * yz, axis=-1, keepdims=True)
        o_ref[g] = (yz * lax.rsqrt(ms + EPS) * gn_ref[...]).astype(BF16)


def _ssd(u, small, cs0, hs0, consts, nseq, rows_per_seq):
    nv = min(SSD_Q, rows_per_seq)
    nchunk = rows_per_seq // nv
    gsz = SSD_GROUP
    assert nseq % gsz == 0
    cw, cb, dtb, alog, dsk, gn, expand, tri, qmask = consts
    u3 = u.reshape(nseq, rows_per_seq, U_WIDTH)
    small3 = small.reshape(nseq, rows_per_seq, LANES)

    def ucol(name, width):
        idx = _U_OFF[name] // width
        assert idx * width == _U_OFF[name]
        return pl.BlockSpec((gsz, nv, width), lambda s, c: (s, c, idx))

    full = lambda a: pl.BlockSpec(a.shape, lambda s, c: (0,) * a.ndim)
    cs_spec = pl.BlockSpec((gsz, CONV_PAD, B_XBC), lambda s, c: (s, 0, 0))
    hs_spec = pl.BlockSpec((gsz, B_WIDTH, LANES), lambda s, c: (s, 0, 0))
    o, hs, cs = pl.pallas_call(
        functools.partial(_ssd_kernel, nv=nv, nchunk=nchunk),
        grid=(nseq // gsz, nchunk),
        in_specs=[ucol('b_xbc', B_XBC), ucol('b_z', B_WIDTH),
                  pl.BlockSpec((gsz, nv, LANES), lambda s, c: (s, c, 0)), cs_spec, hs_spec,
                  full(cw), full(cb), full(dtb), full(alog), full(dsk), full(gn), full(expand), full(tri),
                  full(qmask)],
        out_specs=[pl.BlockSpec((gsz, nv, B_WIDTH), lambda s, c: (s, c, 0)), hs_spec, cs_spec],
        out_shape=[jax.ShapeDtypeStruct((nseq, rows_per_seq, B_WIDTH), BF16),
                   jax.ShapeDtypeStruct((nseq, B_WIDTH, LANES), F32),
                   jax.ShapeDtypeStruct((nseq, CONV_PAD, B_XBC), F32)],
        scratch_shapes=[pltpu.VMEM((gsz, LANES, B_WIDTH), F32),
                        pltpu.VMEM((gsz, SSD_Q + CONV_PAD, B_XBC), F32)],
        compiler_params=_cp(("parallel", "arbitrary")),
        name="ssd",
    )(u3, u3, small3, cs0, hs0, cw, cb, dtb, alog, dsk, gn, expand, tri, qmask)
    return o.reshape(nseq * rows_per_seq, B_WIDTH), hs, cs


def _ssd_static_consts():
    expand = np.zeros((LANES, B_WIDTH), np.float32)
    for h in range(B_HEADS):
        expand[DT_LANE + h, h * HEAD_DIM:(h + 1) * HEAD_DIM] = 1.0
    tri = np.tril(np.ones((SSD_Q, SSD_Q), np.float32))
    qmask = np.zeros((LANES, B_WIDTH), np.float32)
    half = B_WIDTH // B_GROUPS
    for g in range(B_GROUPS):
        qmask[g * B_STATE:(g + 1) * B_STATE, g * half:(g + 1) * half] = 1.0
    return jnp.asarray(expand, BF16), jnp.asarray(tri, BF16), jnp.asarray(qmask)


def _stack_state(h):
    r = h.reshape(h.shape[0], B_WIDTH, B_STATE)
    return jnp.concatenate([r, r], axis=2)


def _unstack_state(hst):
    n = hst.shape[0]
    first = (np.arange(B_WIDTH) < B_WIDTH // B_GROUPS)[None, :, None]
    return jnp.where(first, hst[:, :, :B_STATE], hst[:, :, B_STATE:]).reshape(n, B_HEADS, HEAD_DIM, B_STATE)


def _merge_kernel(x, ya, yb, yc, ym, gt, wpa, wpb, wpc, wpm, wout, o):
    d = x.shape[1]
    pa = _dot(ya[...], wpa[...])
    pb = _dot(yb[...], wpb[...])
    pc = _dot(yc[...], wpc[...])
    pm = _dot(ym[...], wpm[...])
    mix = (gt[:, 0:d].astype(F32) * pa + gt[:, d:2 * d].astype(F32) * pb
           + gt[:, 2 * d:3 * d].astype(F32) * pc + gt[:, 3 * d:4 * d].astype(F32) * pm)
    o[...] = x[...] + _dot(mix.astype(BF16), wout[...])


def _merge(x, gates, ya, yb, yc, ym, wpa, wpb, wpc, wpm, wout):
    m, d = x.shape
    tm = _pick(m, (1024, 512, 256, 128))

    def rows(width):
        return pl.BlockSpec((tm, width), lambda i: (i, 0))

    full = lambda a: pl.BlockSpec(a.shape, lambda i: (0, 0))
    return pl.pallas_call(
        _merge_kernel,
        grid=(m // tm,),
        in_specs=[rows(d), rows(A_WIDTH), rows(B_WIDTH), rows(C_WIDTH), rows(M_WIDTH), rows(N_BRANCH * d),
                  full(wpa), full(wpb), full(wpc), full(wpm), full(wout)],
        out_specs=rows(d),
        out_shape=jax.ShapeDtypeStruct((m, d), F32),
        compiler_params=_cp(("parallel",)),
        name="merge",
    )(x, ya, yb, yc, ym, gates, wpa, wpb, wpc, wpm, wout)


def _row(v):
    return v.reshape(1, -1).astype(F32)


def _lane_row(v, lane0):
    return jnp.zeros((1, LANES), F32).at[0, lane0:lane0 + v.shape[0]].set(v.astype(F32))


def _block_diag_mean(width):
    seg = np.arange(width) // HEAD_DIM
    return jnp.asarray((seg[:, None] == seg[None, :]).astype(np.float32) / HEAD_DIM, BF16)


def kernel(x_prompt, x_sample, mem_prompt, cache_a_k, cache_a_v, cache_c_k, cache_c_v, cache_c_logf, state_b_ssm, state_b_conv, cache_mem_k, cache_mem_v, g_norm, w_in, a_qnorm, a_knorm, a_rel, b_conv_w, b_conv_b, b_dt_bias, b_a_log, b_d, b_norm, c_qnorm, c_knorm, c_fbias, m_norm, w_mkv, m_qnorm, m_knorm, w_pa, w_pb, w_pc, w_pm, w_out):
    nb, seq, d = x_prompt.shape
    ns, tdec, _ = x_sample.shape
    depth = g_norm.shape[0]
    n_mem = mem_prompt.shape[1]
    past = cache_c_k.shape[2]
    la_cache = cache_a_k.shape[2]
    mp = nb * seq
    md = ns * tdec
    assert seq % FOX_TK == 0 and seq % BAND_Q == 0 and seq % SSD_Q == 0
    assert tdec == CHUNK and la_cache == A_WIN and past % LANES == 0

    xp = x_prompt.reshape(mp, d)
    xs = x_sample.reshape(md, d)
    mem = mem_prompt.reshape(nb * n_mem, d)

    def feature_major(c):
        dd, n, t, h, e = c.shape
        return jnp.transpose(c, (0, 1, 3, 4, 2)).reshape(dd, n, h * e, t)

    cak, cav = feature_major(cache_a_k), feature_major(cache_a_v)
    cck, ccv = feature_major(cache_c_k), feature_major(cache_c_v)
    cmk, cmv = feature_major(cache_mem_k), feature_major(cache_mem_v)

    bd256 = _block_diag_mean(MXU_WIDTH)
    cum_consts = _cum_consts()
    expand, tri_q, qmask = _ssd_static_consts()
    scale = HEAD_DIM ** -0.5 * LOG2E

    outs = {k: [] for k in ('pa_k', 'pa_v', 'pc_k', 'pc_v', 'pc_f', 'pb_s', 'pb_c', 'pm_k', 'pm_v',
                            'sa_k', 'sa_v', 'sc_k', 'sc_v', 'sc_f', 'sb_s', 'sb_c')}

    for l in range(depth):
        wt = jnp.transpose(w_in[l])
        pieces = [wt[_SRC[n][0]:_SRC[n][0] + _SRC[n][1]] for n in _U_ORDER]
        used = sum(p.shape[0] for p in pieces)
        w_u = jnp.concatenate(pieces + [jnp.zeros((U_WIDTH - used, d), F32)], axis=0).astype(BF16)
        g0, gw = _SRC['gate']
        w_g = wt[g0:g0 + gw].astype(BF16)
        g_row = _row(g_norm[l])
        gains = (_row(jnp.tile(a_qnorm[l], A_HEADS)) * scale, _row(jnp.tile(a_knorm[l], A_HEADS)),
                 _row(jnp.tile(c_qnorm[l], C_HEADS)) * scale, _row(jnp.tile(c_knorm[l], C_HEADS)),
                 _row(jnp.tile(m_qnorm[l], M_HEADS)) * scale, _lane_row(c_fbias[l], CF_LANE), bd256)

        u_p, small_p = _proj(xp, g_row, w_u, act=None, out_dtype=BF16, name="proj_u", w_rows=True,
                             side_col=_U_OFF['b_dt'])
        gates_p = _proj(xp, g_row, w_g, act='sigmoid', out_dtype=BF16, name="proj_gate", w_rows=True)
        (a_qs_p, a_kb_p, a_vtb_p, pa_kt, pa_vt, c_qs_p, c_kb_p, pc_kt, pc_vt, c_vtb_p, m_qs_p, lf_p,
         pc_ft) = _prep_prompt(u_p, small_p, gains, nb, seq)
        u_s, small_s = _proj(xs, g_row, w_u, act=None, out_dtype=BF16, name="proj_u", w_rows=True,
                             side_col=_U_OFF['b_dt'])
        gates_s = _proj(xs, g_row, w_g, act='sigmoid', out_dtype=BF16, name="proj_gate", w_rows=True)
        (a_qs_s, a_kb_s, a_vb_s, sa_k4, sa_v4, c_qs_s, c_kb_s, c_vb_s, sc_k4, sc_v4, m_qs_s,
         lf_s) = _prep_sample(u_s, small_s, gains)

        kv = _proj(mem, _row(m_norm[l]), w_mkv[l].astype(BF16), act=None, out_dtype=F32, name="proj_mem",
                   w_rows=False)
        mkt, mvt = _memkv(kv, _row(jnp.tile(m_knorm[l], M_HEADS)), bd256, nb, n_mem)

        o_a_p = _band_prompt(a_qs_p, a_kb_p, a_vtb_p, _band_bias(a_rel[l], BAND_Q, BAND_K, True), u_p, nb, seq)
        bias_s = _band_bias(a_rel[l], tdec, la_cache + LANES, False)
        o_a_s = _band_sample(a_qs_s, a_kb_s, a_vb_s, cak, cav, l, bias_s, u_s, ns, tdec)

        cq_p, ck_p = _cum(lf_p.reshape(nb, seq, LANES), cum_consts)
        o_c_p = _fox_prompt(c_qs_p, c_kb_p, c_vtb_p, cq_p.reshape(mp, LANES), ck_p.reshape(mp, LANES), u_p, nb, seq)
        lf_cache = jnp.pad(cache_c_logf[l].astype(F32), ((0, 0), (0, 0), (CF_LANE, LANES - CF_LANE - C_HEADS)))
        lf_cat = jnp.concatenate([lf_cache, lf_s.reshape(ns, tdec, LANES),
                                  jnp.zeros((ns, LANES - tdec, LANES), F32)], axis=1)
        cq_s, ck_s = _cum(lf_cat, cum_consts)
        o_c_s = _fox_sample(c_qs_s, c_kb_s, c_vb_s, cq_s, ck_s, cck, ccv, l, u_s, ns, tdec)

        ssd_consts = (jnp.pad(b_conv_w[l], ((0, SUBLANES - B_CONV), (0, 0))), _row(b_conv_b[l]),
                      _lane_row(b_dt_bias[l], DT_LANE), _lane_row(b_a_log[l], DT_LANE),
                      _row(jnp.repeat(b_d[l], HEAD_DIM)), _row(b_norm[l]), expand, tri_q, qmask)
        o_b_p, hs_p, cs_p = _ssd(u_p, small_p, jnp.zeros((nb, CONV_PAD, B_XBC), F32),
                                 jnp.zeros((nb, B_WIDTH, LANES), F32), ssd_consts, nb, seq)
        cs0 = jnp.pad(state_b_conv[l].astype(F32), ((0, 0), (CONV_PAD - (B_CONV - 1), 0), (0, 0)))
        o_b_s, hs_s, cs_s = _ssd(u_s, small_s, cs0, _stack_state(state_b_ssm[l].astype(F32)), ssd_consts,
                                 ns, tdec)

        o_m_p = _cross(m_qs_p, mkt[None], mvt[None], 0, u_p, seq, "cross_prompt")
        o_m_s = _cross(m_qs_s, cmk, cmv, l, u_s, tdec, "cross_sample")

        w_merge = (w_pa[l].astype(BF16), w_pb[l].astype(BF16), w_pc[l].astype(BF16), w_pm[l].astype(BF16),
                   w_out[l].astype(BF16))
        xp = _merge(xp, gates_p, o_a_p, o_b_p, o_c_p, o_m_p, *w_merge)
        xs = _merge(xs, gates_s, o_a_s, o_b_s, o_c_s, o_m_s, *w_merge)

        outs['pa_k'].append(pa_kt)
        outs['pa_v'].append(pa_vt)
        outs['pc_k'].append(pc_kt)
        outs['pc_v'].append(pc_vt)
        outs['pc_f'].append(pc_ft)
        outs['pb_s'].append(_unstack_state(hs_p))
        outs['pb_c'].append(cs_p[:, CONV_PAD - (B_CONV - 1):])
        outs['pm_k'].append(mkt)
        outs['pm_v'].append(mvt)
        outs['sa_k'].append(sa_k4)
        outs['sa_v'].append(sa_v4)
        outs['sc_k'].append(sc_k4)
        outs['sc_v'].append(sc_v4)
        outs['sc_f'].append(lf_s[:, CF_LANE:CF_LANE + C_HEADS].reshape(ns, tdec, C_HEADS))
        outs['sb_s'].append(_unstack_state(hs_s))
        outs['sb_c'].append(cs_s[:, CONV_PAD - (B_CONV - 1):])

    st = jnp.stack

    def token_major(name, heads):
        a = st(outs[name])
        dd, n, _, t = a.shape
        return jnp.transpose(a.reshape(dd, n, heads, HEAD_DIM, t), (0, 1, 4, 2, 3))

    def sample_heads(name):
        return st(outs[name]).reshape(depth, ns, tdec, A_HEADS, HEAD_DIM)

    return (xp.reshape(nb, seq, d), xs.reshape(ns, tdec, d),
            token_major('pa_k', A_HEADS), token_major('pa_v', A_HEADS),
            token_major('pc_k', C_HEADS), token_major('pc_v', C_HEADS),
            jnp.transpose(st(outs['pc_f']), (0, 1, 3, 2)),
            st(outs['pb_s']), st(outs['pb_c']), token_major('pm_k', M_HEADS), token_major('pm_v', M_HEADS),
            sample_heads('sa_k'), sample_heads('sa_v'), sample_heads('sc_k'), sample_heads('sc_v'),
            st(outs['sc_f']), st(outs['sb_s']), st(outs['sb_c']))
```
